```python
import math
import jax
import jax.numpy as jnp
from jax import lax
import numpy as np

D_MODEL = 1024
BATCH = 8
SEQ = 4096
DEPTH = 2

GRID_W = 64
CTX_LEN = 256
NORM_EPS = 1e-6
ROPE_BASE = 10000.0
NEG_INF = -1e30

NA_HEADS = 4
NA_DIM = 64
NA_WIN_ROWS = 8
NA_WIN_COLS = 16
NA_QCOLS = 16
NA_KCOLS = 32
DIFF_HEADS = 4
DIFF_DIM = 32
DIFF_VDIM = 2 * DIFF_DIM
DIFF_QBLOCK = 128
RET_HEADS = 4
RET_QK = 64
RET_V = 128
RET_CHUNK = 128

NA_W = NA_HEADS * NA_DIM
DIFF_QK_W = DIFF_HEADS * 2 * DIFF_DIM
DIFF_W = DIFF_HEADS * DIFF_VDIM
RET_QK_W = RET_HEADS * RET_QK
RET_W = RET_HEADS * RET_V
MIX_W = NA_W + DIFF_W + RET_W
IN_SPLITS = (NA_W, NA_W, NA_W, DIFF_QK_W, DIFF_QK_W, DIFF_W, RET_QK_W, RET_QK_W, RET_W, RET_W)
IN_COLS = 3 * NA_W + 2 * DIFF_QK_W + DIFF_W + 2 * RET_QK_W + 2 * RET_W

PEER_HEADS = 8
PEER_NKEYS = 128
PEER_EXPERTS = PEER_NKEYS * PEER_NKEYS
PEER_KDIM = 256
PEER_TOPK = 16
PEER_CHUNK = 128

kernel_name = "hybrid_natten_diffattn_retnet_peer_dit"


def _rms(x):
    xf = x.astype(jnp.float32)
    return xf * lax.rsqrt(jnp.mean(xf * xf, axis=-1, keepdims=True) + NORM_EPS)


def rmsnorm(x, g):
    return _rms(x) * g.astype(jnp.float32)


def split_cols(p):
    out = []
    o = 0
    for s in IN_SPLITS:
        out.append(p[..., o:o + s])
        o += s
    return out


def split_heads(t, h):
    b, l, _ = t.shape
    return t.reshape(b, l, h, -1).transpose(0, 2, 1, 3)


def merge_heads(t):
    b, h, l, d = t.shape
    return t.transpose(0, 2, 1, 3).reshape(b, l, h * d)


def diff_split(t):
    b, l, _ = t.shape
    return t.reshape(b, l, DIFF_HEADS, 2, DIFF_DIM).transpose(0, 2, 3, 1, 4)


def flip_seq(t):
    return jnp.flip(t, axis=2)


def axial_rope_tables(n_tokens, dim):
    t = jnp.arange(n_tokens)
    rows = (t // GRID_W).astype(jnp.float32)
    cols = (t % GRID_W).astype(jnp.float32)
    half = dim // 2
    inv = jnp.power(ROPE_BASE, -jnp.arange(0, half, 2, dtype=jnp.float32) / half)
    ang = jnp.concatenate([rows[:, None] * inv, cols[:, None] * inv], axis=-1)
    return jnp.cos(ang), jnp.sin(ang)


def apply_rope(x, cos, sin):
    xf = x.astype(jnp.float32).reshape(x.shape[:-1] + (x.shape[-1] // 2, 2))
    x1, x2 = xf[..., 0], xf[..., 1]
    out = jnp.stack([x1 * cos - x2 * sin, x1 * sin + x2 * cos], axis=-1)
    return out.reshape(x.shape)


def dense_attention(q, k, v):
    s = jnp.einsum('bhqd,bhkd->bhqk', q.astype(jnp.float32), k.astype(jnp.float32)) * q.shape[-1] ** -0.5
    p = jax.nn.softmax(s, axis=-1)
    return jnp.einsum('bhqk,bhkd->bhqd', p, v.astype(jnp.float32))


def neighbourhood_attention(q, k, v, kc, vc, rpb):
    b, h, l, d = q.shape
    f32 = jnp.float32
    rows = l // GRID_W
    wr = min(NA_WIN_ROWS, rows)
    r = np.arange(rows)
    row_idx = np.clip(r - wr // 2, 0, rows - wr)[:, None] + np.arange(wr)[None, :]
    dr = row_idx - r[:, None] + (NA_WIN_ROWS - 1)
    qg = q.astype(f32).reshape(b, h, rows, GRID_W, d) * d ** -0.5
    kband = k.reshape(b, h, rows, GRID_W, d)[:, :, row_idx]
    vband = v.reshape(b, h, rows, GRID_W, d)[:, :, row_idx]
    kc = kc.astype(f32)
    vc = vc.astype(f32)
    rpb = rpb.astype(f32)
    n_win = wr * NA_KCOLS
    outs = []
    for j in range(GRID_W // NA_QCOLS):
        q0 = j * NA_QCOLS
        k0 = min(max(q0 - (NA_KCOLS - NA_QCOLS) // 2, 0), GRID_W - NA_KCOLS)
        qcol = q0 + np.arange(NA_QCOLS)
        kcol = k0 + np.arange(NA_KCOLS)
        cs = np.clip(qcol - NA_WIN_COLS // 2, 0, GRID_W - NA_WIN_COLS)
        valid = (kcol[None, :] >= cs[:, None]) & (kcol[None, :] < cs[:, None] + NA_WIN_COLS)
        dc = np.clip(kcol[None, :] - qcol[:, None], 1 - NA_WIN_COLS, NA_WIN_COLS - 1) + NA_WIN_COLS - 1
        bias = rpb[:, dr[:, None, :, None], dc[None, :, None, :]]
        bias = jnp.where(valid[None, None, :, None, :], bias, NEG_INF).reshape(h, rows, NA_QCOLS, n_win)
        qb = qg[:, :, :, q0:q0 + NA_QCOLS]
        kb = kband[:, :, :, :, k0:k0 + NA_KCOLS].reshape(b, h, rows, n_win, d).astype(f32)
        vb = vband[:, :, :, :, k0:k0 + NA_KCOLS].reshape(b, h, rows, n_win, d).astype(f32)
        s = jnp.concatenate([jnp.einsum('bhrqd,bhrkd->bhrqk', qb, kb) + bias[None],
                             jnp.einsum('bhrqd,bhkd->bhrqk', qb, kc)], axis=-1)
        p = jax.nn.softmax(s, axis=-1)
        o = (jnp.einsum('bhrqk,bhrkd->bhrqd', p[..., :n_win], vb)
             + jnp.einsum('bhrqk,bhkd->bhrqd', p[..., n_win:], vc))
        outs.append(o)
    return jnp.concatenate(outs, axis=3).reshape(b, h, l, d)


def diff_attend(q, k, v, lam):
    s = jnp.einsum('bhcqd,bhckd->bhcqk', q.astype(jnp.float32), k.astype(jnp.float32)) * DIFF_DIM ** -0.5
    p = jax.nn.softmax(s, axis=-1)
    a = p[:, :, 0] - lam * p[:, :, 1]
    return jnp.einsum('bhqk,bhkd->bhqd', a, v.astype(jnp.float32))


def diff_attention_blocks(q, k_all, v_all, lam):
    b, h, _, l, d = q.shape
    nb = l // DIFF_QBLOCK
    qb = q.reshape(b, h, 2, nb, DIFF_QBLOCK, d).transpose(3, 0, 1, 2, 4, 5)
    o = lax.map(lambda qq: diff_attend(qq, k_all, v_all, lam), qb)
    return o.transpose(1, 2, 0, 3, 4).reshape(b, h, l, -1)


def diff_post(o, g, lam_init):
    return rmsnorm(o, g) * (1.0 - lam_init)


def retention_chunked(q, k, v, log_g, s0):
    f32 = jnp.float32
    b, h, l, dk = q.shape
    dv = v.shape[-1]
    c = RET_CHUNK
    n = l // c
    i = jnp.arange(c, dtype=f32)
    diff = i[:, None] - i[None, :]
    dmask = jnp.where(diff >= 0, jnp.exp(log_g[:, None, None] * jnp.maximum(diff, 0.0)), 0.0)
    q_dec = jnp.exp(log_g[:, None] * (i + 1.0)[None, :])
    k_dec = jnp.exp(log_g[:, None] * (c - 1.0 - i)[None, :])
    c_dec = jnp.exp(log_g * c)
    qc = q.astype(f32).reshape(b, h, n, c, dk)
    kc = k.astype(f32).reshape(b, h, n, c, dk)
    vc = v.astype(f32).reshape(b, h, n, c, dv)
    att = jnp.einsum('bhnid,bhnjd->bhnij', qc, kc) * dmask[None, :, None]
    o = jnp.einsum('bhnij,bhnje->bhnie', att, vc)
    kv = jnp.einsum('bhnjd,bhnje->bhnde', kc * k_dec[None, :, None, :, None], vc)

    def step(state, kv_c):
        return c_dec[None, :, None, None] * state + kv_c, state

    s_final, s_prev = lax.scan(step, s0.astype(f32), jnp.moveaxis(kv, 2, 0))
    s_prev = jnp.moveaxis(s_prev, 0, 2)
    o = o + jnp.einsum('bhnid,bhnde->bhnie', qc * q_dec[None, :, None, :, None], s_prev)
    return o.reshape(b, h, l, dv), s_final


def retention_final_state(k, v, log_g, reverse):
    l = k.shape[2]
    m = jnp.arange(l, dtype=jnp.float32)
    expo = m if reverse else (l - 1.0 - m)
    w = jnp.exp(log_g[:, None] * expo[None, :])
    return jnp.einsum('bhld,bhle->bhde', k.astype(jnp.float32) * w[None, :, :, None], v.astype(jnp.float32))


def retention_post(y, g):
    return merge_heads(_rms(y)) * jax.nn.silu(g.astype(jnp.float32))


def token_mixers(px, pc, rpb, lam_p, subln_g, decay_logit, layer_idx, rope_d, rope_r, need_ctx):
    f32 = jnp.float32
    na_q, na_k, na_v, df_q, df_k, df_v, rt_q, rt_k, rt_v, rt_g = px
    cna_q, cna_k, cna_v, cdf_q, cdf_k, cdf_v, crt_q, crt_k, crt_v, crt_g = pc
    b = na_q.shape[0]
    kc_na = split_heads(cna_k, NA_HEADS)
    vc_na = split_heads(cna_v, NA_HEADS)
    y_na = neighbourhood_attention(split_heads(na_q, NA_HEADS), split_heads(na_k, NA_HEADS),
                                   split_heads(na_v, NA_HEADS), kc_na, vc_na, rpb)
    lam_init = 0.8 - 0.6 * math.exp(-0.3 * layer_idx)
    lp = lam_p.astype(f32)
    lam = jnp.exp(jnp.sum(lp[0] * lp[1])) - jnp.exp(jnp.sum(lp[2] * lp[3])) + lam_init
    cos_d, sin_d = rope_d
    q_df = apply_rope(diff_split(df_q), cos_d, sin_d)
    k_df = apply_rope(diff_split(df_k), cos_d, sin_d)
    kc_df = diff_split(cdf_k).astype(f32)
    vc_df = split_heads(cdf_v, DIFF_HEADS).astype(f32)
    k_all = jnp.concatenate([kc_df, k_df], axis=3)
    v_all = jnp.concatenate([vc_df, split_heads(df_v, DIFF_HEADS).astype(f32)], axis=2)
    y_df = diff_post(diff_attention_blocks(q_df, k_all, v_all, lam), subln_g, lam_init)
    lg = jax.nn.log_sigmoid(decay_logit.astype(f32))
    cos_r, sin_r = rope_r
    q_rt = apply_rope(split_heads(rt_q, RET_HEADS), cos_r, sin_r)
    k_rt = apply_rope(split_heads(rt_k, RET_HEADS), cos_r, sin_r) * RET_QK ** -0.5
    v_rt = split_heads(rt_v, RET_HEADS).astype(f32)
    kc_rt = split_heads(crt_k, RET_HEADS).astype(f32) * RET_QK ** -0.5
    vc_rt = split_heads(crt_v, RET_HEADS).astype(f32)
    if need_ctx:
        z = jnp.zeros((b, RET_HEADS, RET_QK, RET_V), f32)
        qc_rt = split_heads(crt_q, RET_HEADS).astype(f32)
        oc_f, s_f = retention_chunked(qc_rt, kc_rt, vc_rt, lg[0], z)
        oc_b, s_b = retention_chunked(flip_seq(qc_rt), flip_seq(kc_rt), flip_seq(vc_rt), lg[1], z)
        yc_rt = retention_post(oc_f + flip_seq(oc_b), crt_g)
    else:
        s_f = retention_final_state(kc_rt, vc_rt, lg[0], False)
        s_b = retention_final_state(kc_rt, vc_rt, lg[1], True)
    o_f, _ = retention_chunked(q_rt, k_rt, v_rt, lg[0], s_f)
    o_b, _ = retention_chunked(flip_seq(q_rt), flip_seq(k_rt), flip_seq(v_rt), lg[1], s_b)
    y_rt = retention_post(o_f + flip_seq(o_b), rt_g)
    y_lat = jnp.concatenate([merge_heads(y_na), merge_heads(y_df), y_rt], axis=-1)
    if not need_ctx:
        return y_lat, None
    yc_na = dense_attention(split_heads(cna_q, NA_HEADS), kc_na, vc_na)
    yc_df = diff_post(diff_attend(diff_split(cdf_q), kc_df, vc_df, lam), subln_g, lam_init)
    y_ctx = jnp.concatenate([merge_heads(yc_na), merge_heads(yc_df), yc_rt], axis=-1)
    return y_lat, y_ctx


def peer(h, wq, subkeys, u, v):
    t, d = h.shape
    q = (h @ wq).astype(jnp.float32).reshape(t, PEER_HEADS, 2, PEER_KDIM // 2)
    s = jnp.einsum('thpd,hpnd->thpn', q, subkeys.astype(jnp.float32))
    s_top, i_top = lax.top_k(s, PEER_TOPK)
    cand = (s_top[:, :, 0, :, None] + s_top[:, :, 1, None, :]).reshape(t, PEER_HEADS, PEER_TOPK * PEER_TOPK)
    cand_idx = (i_top[:, :, 0, :, None] * PEER_NKEYS + i_top[:, :, 1, None, :]).reshape(t, PEER_HEADS, -1)
    best, pos = lax.top_k(cand, PEER_TOPK)
    expert = jnp.take_along_axis(cand_idx, pos, axis=-1)
    gate = jax.nn.softmax(best, axis=-1)
    nchunk = t // PEER_CHUNK

    def eval_chunk(args):
        hc, ec, gc = args
        a = jax.nn.gelu(jnp.einsum('cd,chkd->chk', hc, u[ec]).astype(jnp.float32), approximate=False)
        w = (gc * a).astype(v.dtype)
        return jnp.einsum('chk,chkd->cd', w, v[ec])

    out = lax.map(eval_chunk, (h.reshape(nchunk, PEER_CHUNK, d),
                               expert.reshape(nchunk, PEER_CHUNK, PEER_HEADS, PEER_TOPK),
                               gate.reshape(nchunk, PEER_CHUNK, PEER_HEADS, PEER_TOPK)))
    return out.reshape(t, d)


def setup_inputs(seed: int = 0) -> dict:
    key = jax.random.key(seed)
    ks = jax.random.split(key, 20)
    f32 = jnp.float32
    nrm = jax.random.normal
    d = D_MODEL
    x = nrm(ks[0], (BATCH, SEQ, d), f32)
    c = nrm(ks[1], (BATCH, d), f32)
    ctx = nrm(ks[2], (BATCH, CTX_LEN, d), f32)
    c_ctx = nrm(ks[3], (d,), f32)
    w_ada = nrm(ks[4], (DEPTH, d, 6 * d), f32) * (0.5 * d ** -0.5)
    b_ada = 0.02 * nrm(ks[5], (DEPTH, 6 * d), f32)
    norm1_g = 1.0 + 0.02 * nrm(ks[6], (DEPTH, d), f32)
    w_in = nrm(ks[7], (DEPTH, d, IN_COLS), f32) * d ** -0.5
    na_rpb = 0.1 * nrm(ks[8], (DEPTH, NA_HEADS, 2 * NA_WIN_ROWS - 1, 2 * NA_WIN_COLS - 1), f32)
    diff_lambda = 0.1 * nrm(ks[9], (DEPTH, 4, DIFF_DIM), f32)
    diff_subln_g = 1.0 + 0.02 * nrm(ks[10], (DEPTH, DIFF_VDIM), f32)
    base_logit = jnp.log(jnp.power(2.0, 5.0 + jnp.arange(RET_HEADS, dtype=f32)) - 1.0)
    ret_decay_logit = base_logit + 0.1 * nrm(ks[11], (DEPTH, 2, RET_HEADS), f32)
    w_out = nrm(ks[12], (DEPTH, MIX_W, d), f32) * MIX_W ** -0.5
    norm2_g = 1.0 + 0.02 * nrm(ks[13], (DEPTH, d), f32)
    peer_wq = nrm(ks[14], (DEPTH, d, PEER_HEADS * PEER_KDIM), f32) * d ** -0.5
    peer_subkeys = nrm(ks[15], (DEPTH, PEER_HEADS, 2, PEER_NKEYS, PEER_KDIM // 2), f32) * (PEER_KDIM // 2) ** -0.5
    peer_u = nrm(ks[16], (DEPTH, PEER_EXPERTS, d), f32) * d ** -0.5
    peer_v = nrm(ks[17], (DEPTH, PEER_EXPERTS, d), f32) * PEER_HEADS ** -0.5
    final_g = 1.0 + 0.02 * nrm(ks[18], (d,), f32)
    return {"x": x, "c": c, "ctx": ctx, "c_ctx": c_ctx, "w_ada": w_ada, "b_ada": b_ada,
            "norm1_g": norm1_g, "w_in": w_in, "na_rpb": na_rpb, "diff_lambda": diff_lambda,
            "diff_subln_g": diff_subln_g, "ret_decay_logit": ret_decay_logit, "w_out": w_out,
            "norm2_g": norm2_g, "peer_wq": peer_wq, "peer_subkeys": peer_subkeys,
            "peer_u": peer_u, "peer_v": peer_v, "final_g": final_g}


def reference(x, c, ctx, c_ctx, w_ada, b_ada, norm1_g, w_in, na_rpb, diff_lambda, diff_subln_g,
              ret_decay_logit, w_out, norm2_g, peer_wq, peer_subkeys, peer_u, peer_v, final_g):
    b, l, d = x.shape
    lc = ctx.shape[1]
    rope_d = axial_rope_tables(l, DIFF_DIM)
    rope_r = axial_rope_tables(l, RET_QK)
    silu_c = jax.nn.silu(c.astype(jnp.float32))
    silu_cc = jax.nn.silu(c_ctx.astype(jnp.float32))
    for layer in range(DEPTH):
        need_ctx = layer < DEPTH - 1
        mod = jnp.split((silu_c @ w_ada[layer] + b_ada[layer])[:, None, :], 6, axis=-1)
        mod_c = jnp.split(silu_cc @ w_ada[layer] + b_ada[layer], 6, axis=-1)
        hx = rmsnorm(x, norm1_g[layer]) * (1.0 + mod[1]) + mod[0]
        hc = rmsnorm(ctx, norm1_g[layer]) * (1.0 + mod_c[1]) + mod_c[0]
        y_lat, y_ctx = token_mixers(split_cols(hx @ w_in[layer]), split_cols(hc @ w_in[layer]),
                                    na_rpb[layer], diff_lambda[layer], diff_subln_g[layer],
                                    ret_decay_logit[layer], layer, rope_d, rope_r, need_ctx)
        x = x + mod[2] * (y_lat @ w_out[layer])
        h2 = rmsnorm(x, norm2_g[layer]) * (1.0 + mod[4]) + mod[3]
        x = x + mod[5] * peer(h2.reshape(b * l, d), peer_wq[layer], peer_subkeys[layer],
                              peer_u[layer], peer_v[layer]).reshape(b, l, d)
        if need_ctx:
            ctx = ctx + mod_c[2] * (y_ctx @ w_out[layer])
            h2c = rmsnorm(ctx, norm2_g[layer]) * (1.0 + mod_c[4]) + mod_c[3]
            ctx = ctx + mod_c[5] * peer(h2c.reshape(b * lc, d), peer_wq[layer], peer_subkeys[layer],
                                        peer_u[layer], peer_v[layer]).reshape(b, lc, d)
    return rmsnorm(x, final_g)
```

```python
import functools
import math

import numpy as np
import jax
import jax.numpy as jnp
from jax import lax
from jax.experimental import pallas as pl
from jax.experimental.pallas import tpu as pltpu

F32 = jnp.float32
BF16 = jnp.bfloat16

GRID_W = 64
NORM_EPS = 1e-6
ROPE_BASE = 10000.0
NEG_INF = -1e30

NA_HEADS = 4
NA_DIM = 64
NA_WIN_ROWS = 8
NA_WIN_COLS = 16
DIFF_HEADS = 4
DIFF_DIM = 32
DIFF_VDIM = 64
RET_HEADS = 4
RET_QK = 64
RET_V = 128
RET_CHUNK = 128

NA_W = NA_HEADS * NA_DIM
DIFF_QK_W = DIFF_HEADS * 2 * DIFF_DIM
DIFF_W = DIFF_HEADS * DIFF_VDIM
RET_QK_W = RET_HEADS * RET_QK
RET_W = RET_HEADS * RET_V
MIX_W = NA_W + DIFF_W + RET_W
IN_COLS = 3 * NA_W + 2 * DIFF_QK_W + DIFF_W + 2 * RET_QK_W + 2 * RET_W

PEER_HEADS = 8
PEER_NKEYS = 128
PEER_KDIM = 256
PEER_TOPK = 16
PEER_HALF = PEER_KDIM // 2

C_NAQ, C_NAK, C_NAV = 0, 256, 512
C_DFQ, C_DFK, C_DFV = 768, 1024, 1280
C_RTQ, C_RTK, C_RTV, C_RTG = 1536, 1792, 2048, 2560
C_DFQ_S, C_DFK_S, C_RTQ_S, C_RTK_S = 3072, 3328, 3584, 3840
IN_COLS_EXT = 4096

PEER_PAIRS = tuple((a, b) for a in range(PEER_TOPK) for b in range(PEER_TOPK // (a + 1)))
PEER_NCAND = 56

VMEM_LIMIT = 56 * 1024 * 1024


def _cparams(sem):
    return pltpu.CompilerParams(dimension_semantics=sem, vmem_limit_bytes=VMEM_LIMIT)


def _nt(a, b):
    return lax.dot_general(a, b, (((1,), (1,)), ((), ())), preferred_element_type=F32)


def _tn(a, b):
    return lax.dot_general(a, b, (((0,), (0,)), ((), ())), preferred_element_type=F32)


def _dot(a, b):
    return jnp.dot(a, b, preferred_element_type=F32)


def _rms(x):
    return x * lax.rsqrt(jnp.mean(x * x, axis=-1, keepdims=True) + NORM_EPS)


def _silu(x):
    return x * jax.nn.sigmoid(x)


def _ada_kernel(c_ref, w_ref, b_ref, o_ref):
    s = _silu(c_ref[...])
    o_ref[0] = jnp.dot(s, w_ref[0], preferred_element_type=F32,
                       precision=lax.Precision.HIGHEST) + b_ref[0]


def _ada(cpad, w_ada, b_ada):
    depth, d, n = w_ada.shape
    tn = 1536
    return pl.pallas_call(
        _ada_kernel,
        grid=(depth, n // tn),
        in_specs=[pl.BlockSpec((cpad.shape[0], d), lambda l, j: (0, 0)),
                  pl.BlockSpec((1, d, tn), lambda l, j: (l, 0, j)),
                  pl.BlockSpec((1, 1, tn), lambda l, j: (l, 0, j))],
        out_specs=pl.BlockSpec((1, cpad.shape[0], tn), lambda l, j: (l, 0, j)),
        out_shape=jax.ShapeDtypeStruct((depth, cpad.shape[0], n), F32),
        compiler_params=_cparams(("parallel", "parallel")),
        name="ada",
    )(cpad, w_ada, b_ada.reshape(depth, 1, n))


def _inproj_kernel(*refs, rope, d):
    if rope:
        (x_ref, mod_ref, g_ref, w_ref, cd_ref, sd_ref, cr_ref, sr_ref,
         naq, nak, nav, dfq, dfk, dfv, rtq, rtk, rtv, rtg) = refs
    else:
        (x_ref, mod_ref, g_ref, w_ref,
         naq, nak, nav, dfq, dfk, dfv, rtq, rtk, rtv, rtg) = refs
    mod = mod_ref[0]
    h = _rms(x_ref[0]) * g_ref[...] * (1.0 + mod[:, d:2 * d]) + mod[:, 0:d]
    hb = h.astype(BF16)

    def proj(c0, n):
        return _dot(hb, w_ref[:, c0:c0 + n])

    def roped(c0, c0s, cos_ref, sin_ref):
        if rope:
            return proj(c0, 256) * cos_ref[...] + proj(c0s, 256) * sin_ref[...]
        return proj(c0, 256)

    naq[0] = (proj(C_NAQ, 256) * NA_DIM ** -0.5).astype(BF16)
    nak[0] = proj(C_NAK, 256).astype(BF16)
    nav[0] = proj(C_NAV, 256).astype(BF16)
    dfq[0] = (roped(C_DFQ, C_DFQ_S, cd_ref if rope else None, sd_ref if rope else None)
              * DIFF_DIM ** -0.5).astype(BF16)
    dfk[0] = roped(C_DFK, C_DFK_S, cd_ref if rope else None, sd_ref if rope else None).astype(BF16)
    dfv[0] = proj(C_DFV, 256).astype(BF16)
    rtq[0] = roped(C_RTQ, C_RTQ_S, cr_ref if rope else None, sr_ref if rope else None).astype(BF16)
    rtk[0] = (roped(C_RTK, C_RTK_S, cr_ref if rope else None, sr_ref if rope else None)
              * RET_QK ** -0.5).astype(BF16)
    rtv[0] = proj(C_RTV, 512).astype(BF16)
    rtg[0] = proj(C_RTG, 512)


def _inproj(x, mod3, mod_row, g1, w_ext, tables, tm):
    b, l, d = x.shape
    rope = tables is not None
    ncols = IN_COLS_EXT if rope else IN_COLS
    in_specs = [pl.BlockSpec((1, tm, d), lambda i, t: (i, t, 0)),
                pl.BlockSpec((1, 1, 6 * d), lambda i, t: (mod_row(i), 0, 0)),
                pl.BlockSpec((1, d), lambda i, t: (0, 0)),
                pl.BlockSpec((d, ncols), lambda i, t: (0, 0))]
    args = [x, mod3, g1, w_ext]
    if rope:
        in_specs += [pl.BlockSpec((tm, 256), lambda i, t: (t, 0))] * 4
        args += list(tables)
    widths = (256, 256, 256, 256, 256, 256, 256, 256, 512, 512)
    dtypes = (BF16,) * 9 + (F32,)
    out_specs = [pl.BlockSpec((1, tm, w), lambda i, t: (i, t, 0)) for w in widths]
    out_shape = [jax.ShapeDtypeStruct((b, l, w), dt) for w, dt in zip(widths, dtypes)]
    return pl.pallas_call(
        functools.partial(_inproj_kernel, rope=rope, d=d),
        grid=(b, l // tm),
        in_specs=in_specs, out_specs=out_specs, out_shape=out_shape,
        compiler_params=_cparams(("parallel", "parallel")),
        name="inproj_rope" if rope else "inproj_ctx",
    )(*args)


def _na_plan(rows, rb):
    wr = min(NA_WIN_ROWS, rows)
    kr = min(rb + wr - 1, rows)
    nblk = rows // rb
    win0 = np.clip(np.arange(rows) - wr // 2, 0, rows - wr)
    ks = np.clip(np.arange(nblk) * rb - wr // 2, 0, rows - kr)
    pats, pat_of = [], np.zeros(nblk, np.int32)
    for g in range(nblk):
        r = g * rb + np.arange(rb)
        krow = ks[g] + np.arange(kr)
        dr = krow[None, :] - r[:, None] + (NA_WIN_ROWS - 1)
        ok = (krow[None, :] >= win0[r][:, None]) & (krow[None, :] < win0[r][:, None] + wr)
        assert ok.sum(axis=1).min() == wr
        key = (np.where(ok, dr, 0).tobytes(), ok.tobytes())
        for p, (k2, _, _) in enumerate(pats):
            if k2 == key:
                pat_of[g] = p
                break
        else:
            pat_of[g] = len(pats)
            pats.append((key, np.where(ok, dr, 0), ok))
    dr_idx = np.stack([p[1] for p in pats])
    row_ok = np.stack([p[2] for p in pats])
    qcol = np.arange(GRID_W)
    kcol = np.arange(GRID_W)
    cs = np.clip(qcol - NA_WIN_COLS // 2, 0, GRID_W - NA_WIN_COLS)
    col_ok = (kcol[None, :] >= cs[:, None]) & (kcol[None, :] < cs[:, None] + NA_WIN_COLS)
    dc_idx = np.clip(kcol[None, :] - qcol[:, None], 1 - NA_WIN_COLS, NA_WIN_COLS - 1) + NA_WIN_COLS - 1
    return kr, ks.astype(np.int32), pat_of, dr_idx, row_ok, dc_idx, col_ok


def _na_bias(rpb, plan, rb):
    kr, _, _, dr_idx, row_ok, dc_idx, col_ok = plan
    p = dr_idx.shape[0]
    bias = rpb.astype(F32)[:, dr_idx[:, :, None, :, None], dc_idx[None, None, :, None, :]]
    ok = row_ok[:, :, None, :, None] & col_ok[None, None, :, None, :]
    bias = jnp.where(ok[None], bias, NEG_INF)
    return bias.reshape(NA_HEADS, p, rb * GRID_W, kr * GRID_W)


def _na_kernel(ks_ref, pat_ref, q_ref, kx_ref, vx_ref, kc_ref, vc_ref, bias_ref, o_ref, *, nkeys):
    g = pl.program_id(1)
    q = q_ref[0]
    k0 = pl.multiple_of(ks_ref[g] * GRID_W, GRID_W)
    kw = kx_ref[0, pl.ds(k0, nkeys), :]
    vw = vx_ref[0, pl.ds(k0, nkeys), :]
    kc = kc_ref[0]
    vc = vc_ref[0]
    pat = pat_ref[g]
    lane = lax.broadcasted_iota(jnp.int32, (1, NA_W), 1)
    out = jnp.zeros((q.shape[0], NA_W), F32)
    for h in range(NA_HEADS):
        hm = (lane >= NA_DIM * h) & (lane < NA_DIM * (h + 1))
        qh = jnp.where(hm, q, jnp.zeros_like(q))
        sw = _nt(qh, kw) + bias_ref[h, pat]
        sc = _nt(qh, kc)
        m = jnp.maximum(jnp.max(sw, axis=-1, keepdims=True), jnp.max(sc, axis=-1, keepdims=True))
        pw = jnp.exp(sw - m)
        pc = jnp.exp(sc - m)
        den = jnp.sum(pw, axis=-1, keepdims=True) + jnp.sum(pc, axis=-1, keepdims=True)
        o = _dot(pw.astype(BF16), vw) + _dot(pc.astype(BF16), vc)
        out = out + jnp.where(hm, o / den, 0.0)
    o_ref[0] = out.astype(o_ref.dtype)


def _na_latent(q, kx, vx, kc, vc, rpb, rb):
    b, l, _ = q.shape
    lc = kc.shape[1]
    rows = l // GRID_W
    plan = _na_plan(rows, rb)
    kr, ks, pat_of = plan[0], plan[1], plan[2]
    bias = _na_bias(rpb, plan, rb)
    tq, nkeys = rb * GRID_W, kr * GRID_W
    grid_spec = pltpu.PrefetchScalarGridSpec(
        num_scalar_prefetch=2,
        grid=(b, rows // rb),
        in_specs=[pl.BlockSpec((1, tq, NA_W), lambda i, g, *_: (i, g, 0)),
                  pl.BlockSpec((1, l, NA_W), lambda i, g, *_: (i, 0, 0)),
                  pl.BlockSpec((1, l, NA_W), lambda i, g, *_: (i, 0, 0)),
                  pl.BlockSpec((1, lc, NA_W), lambda i, g, *_: (i, 0, 0)),
                  pl.BlockSpec((1, lc, NA_W), lambda i, g, *_: (i, 0, 0)),
                  pl.BlockSpec(bias.shape, lambda i, g, *_: (0, 0, 0, 0))],
        out_specs=pl.BlockSpec((1, tq, NA_W), lambda i, g, *_: (i, g, 0)),
    )
    return pl.pallas_call(
        functools.partial(_na_kernel, nkeys=nkeys),
        grid_spec=grid_spec,
        out_shape=jax.ShapeDtypeStruct((b, l, NA_W), BF16),
        compiler_params=_cparams(("parallel", "arbitrary")),
        name="na_latent",
    )(jnp.asarray(ks), jnp.asarray(pat_of), q, kx, vx, kc, vc, bias)


def _na_ctx_kernel(q_ref, kc_ref, vc_ref, o_ref):
    q = q_ref[0]
    kc = kc_ref[0]
    vc = vc_ref[0]
    lane = lax.broadcasted_iota(jnp.int32, (1, NA_W), 1)
    out = jnp.zeros((q.shape[0], NA_W), F32)
    for h in range(NA_HEADS):
        hm = (lane >= NA_DIM * h) & (lane < NA_DIM * (h + 1))
        qh = jnp.where(hm, q, jnp.zeros_like(q))
        sc = _nt(qh, kc)
        pc = jnp.exp(sc - jnp.max(sc, axis=-1, keepdims=True))
        o = _dot(pc.astype(BF16), vc)
        out = out + jnp.where(hm, o / jnp.sum(pc, axis=-1, keepdims=True), 0.0)
    o_ref[0] = out.astype(o_ref.dtype)


def _na_ctx(q, kc, vc):
    b, lc, _ = q.shape
    spec = pl.BlockSpec((1, lc, NA_W), lambda i: (i, 0, 0))
    return pl.pallas_call(
        _na_ctx_kernel, grid=(b,), in_specs=[spec, spec, spec], out_specs=spec,
        out_shape=jax.ShapeDtypeStruct((b, lc, NA_W), BF16),
        compiler_params=_cparams(("parallel",)), name="na_ctx",
    )(q, kc, vc)


def _diff_kernel(*refs, lam_init, has_x):
    if has_x:
        q_ref, kc_ref, vc_ref, kx_ref, vx_ref, lam_ref, g_ref, o_ref = refs
    else:
        q_ref, kc_ref, vc_ref, lam_ref, g_ref, o_ref = refs
    q = q_ref[0]
    kc = kc_ref[0]
    vc = vc_ref[0]
    lp = lam_ref[...]
    lam = (jnp.exp(jnp.sum(lp[0:1] * lp[1:2], axis=-1, keepdims=True))
           - jnp.exp(jnp.sum(lp[2:3] * lp[3:4], axis=-1, keepdims=True)) + lam_init)
    lane = lax.broadcasted_iota(jnp.int32, (1, DIFF_W), 1)
    out = jnp.zeros((q.shape[0], DIFF_W), F32)
    for h in range(DIFF_HEADS):
        parts = []
        for c in range(2):
            lo = DIFF_VDIM * h + DIFF_DIM * c
            qm = jnp.where((lane >= lo) & (lane < lo + DIFF_DIM), q, jnp.zeros_like(q))
            sc = _nt(qm, kc)
            m = jnp.max(sc, axis=-1, keepdims=True)
            if has_x:
                sx = _nt(qm, kx_ref[0])
                m = jnp.maximum(m, jnp.max(sx, axis=-1, keepdims=True))
            ec = jnp.exp(sc - m)
            den = jnp.sum(ec, axis=-1, keepdims=True)
            ex = None
            if has_x:
                ex = jnp.exp(sx - m)
                den = den + jnp.sum(ex, axis=-1, keepdims=True)
            parts.append((ec, ex, 1.0 / den))
        (ec0, ex0, r0), (ec1, ex1, r1) = parts
        r1 = lam * r1
        o = _dot((ec0 * r0 - ec1 * r1).astype(BF16), vc)
        if has_x:
            o = o + _dot((ex0 * r0 - ex1 * r1).astype(BF16), vx_ref[0])
        hm = (lane >= DIFF_VDIM * h) & (lane < DIFF_VDIM * (h + 1))
        oh = jnp.where(hm, o, 0.0)
        ms = jnp.sum(oh * oh, axis=-1, keepdims=True) * (1.0 / DIFF_VDIM)
        out = out + oh * lax.rsqrt(ms + NORM_EPS)
    o_ref[0] = (out * g_ref[...] * (1.0 - lam_init)).astype(o_ref.dtype)


def _diff(q, kc, vc, kx, vx, lam_p, g4, lam_init, tq):
    b, l, _ = q.shape
    lc = kc.shape[1]
    has_x = kx is not None
    cspec = pl.BlockSpec((1, lc, DIFF_W), lambda i, t: (i, 0, 0))
    in_specs = [pl.BlockSpec((1, tq, DIFF_W), lambda i, t: (i, t, 0)), cspec, cspec]
    args = [q, kc, vc]
    if has_x:
        xspec = pl.BlockSpec((1, kx.shape[1], DIFF_W), lambda i, t: (i, 0, 0))
        in_specs += [xspec, xspec]
        args += [kx, vx]
    in_specs += [pl.BlockSpec(lam_p.shape, lambda i, t: (0, 0)),
                 pl.BlockSpec((1, DIFF_W), lambda i, t: (0, 0))]
    args += [lam_p, g4]
    return pl.pallas_call(
        functools.partial(_diff_kernel, lam_init=lam_init, has_x=has_x),
        grid=(b, l // tq), in_specs=in_specs,
        out_specs=pl.BlockSpec((1, tq, DIFF_W), lambda i, t: (i, t, 0)),
        out_shape=jax.ShapeDtypeStruct((b, l, DIFF_W), BF16),
        compiler_params=_cparams(("parallel", "arbitrary")),
        name="diff_latent" if has_x else "diff_ctx",
    )(*args)


def _ret_kernel(dl_ref, cq_ref, ck_ref, cv_ref, cg_ref, xq_ref, xk_ref, xv_ref, xg_ref,
                yx_ref, yc_ref, s_ref, *, n_ctx, n_lat):
    h = pl.program_id(1)
    c = RET_CHUNK
    lane = lax.broadcasted_iota(jnp.int32, (1, RET_QK_W), 1)
    hm = (lane >= RET_QK * h) & (lane < RET_QK * (h + 1))
    row = lax.broadcasted_iota(jnp.int32, (c, c), 0).astype(F32)
    col = lax.broadcasted_iota(jnp.int32, (c, c), 1).astype(F32)
    roww = lax.broadcasted_iota(jnp.int32, (c, RET_QK_W), 0).astype(F32)

    def log_sigmoid(d):
        x = jnp.full((1, 1), dl_ref[d, h], F32)
        return jnp.minimum(x, 0.0) - jnp.log1p(jnp.exp(-jnp.abs(x)))

    lgf, lgb = log_sigmoid(0), log_sigmoid(1)
    diff = row - col
    fwd = (jnp.where(diff >= 0, jnp.exp(lgf * jnp.maximum(diff, 0.0)), 0.0),
           jnp.exp(lgf * (roww + 1.0)),
           jnp.exp(lgf * (c - 1.0 - roww)),
           jnp.exp(lgf * c))
    bwd = (jnp.where(diff <= 0, jnp.exp(lgb * jnp.maximum(-diff, 0.0)), 0.0),
           jnp.exp(lgb * (c - roww)),
           jnp.exp(lgb * roww),
           jnp.exp(lgb * c))

    def step(q_ref, k_ref, v_ref, g_ref, out_ref, ci, mats, second):
        dm, qd, kd, cd = mats
        sl = pl.ds(pl.multiple_of(ci * c, c), c)
        q = q_ref[0, sl, :]
        q = jnp.where(hm, q, jnp.zeros_like(q))
        k = k_ref[0, sl, :]
        v = v_ref[0, sl, :]
        att = _nt(q, k) * dm
        s = s_ref[...]
        o = _dot(att.astype(BF16), v) + _dot((q.astype(F32) * qd).astype(BF16), s.astype(BF16))
        s_ref[...] = cd * s + _tn((k.astype(F32) * kd).astype(BF16), v)
        if second:
            tot = out_ref[0, sl, :] + o
            out_ref[0, sl, :] = _rms(tot) * _silu(g_ref[0, sl, :])
        else:
            out_ref[0, sl, :] = o

    def sweep(mats, second, reverse):
        s_ref[...] = jnp.zeros_like(s_ref)
        for i in range(n_ctx):
            ci = n_ctx - 1 - i if reverse else i
            step(cq_ref, ck_ref, cv_ref, cg_ref, yc_ref, ci, mats, second)

        def body(i, carry):
            ci = n_lat - 1 - i if reverse else i
            step(xq_ref, xk_ref, xv_ref, xg_ref, yx_ref, ci, mats, second)
            return carry

        lax.fori_loop(0, n_lat, body, 0)

    sweep(fwd, False, False)
    sweep(bwd, True, True)


def _retention(decay_logit, cq, ck, cv, cg, xq, xk, xv, xg):
    b, l, _ = xq.shape
    lc = cq.shape[1]

    def qk_spec(n):
        return pl.BlockSpec((1, n, RET_QK_W), lambda i, h: (i, 0, 0))

    def v_spec(n):
        return pl.BlockSpec((1, n, RET_V), lambda i, h: (i, 0, h))

    return pl.pallas_call(
        functools.partial(_ret_kernel, n_ctx=lc // RET_CHUNK, n_lat=l // RET_CHUNK),
        grid=(b, RET_HEADS),
        in_specs=[pl.BlockSpec(memory_space=pltpu.SMEM),
                  qk_spec(lc), qk_spec(lc), v_spec(lc), v_spec(lc),
                  qk_spec(l), qk_spec(l), v_spec(l), v_spec(l)],
        out_specs=[v_spec(l), v_spec(lc)],
        out_shape=[jax.ShapeDtypeStruct((b, l, RET_W), F32), jax.ShapeDtypeStruct((b, lc, RET_W), F32)],
        scratch_shapes=[pltpu.VMEM((RET_QK_W, RET_V), F32)],
        compiler_params=_cparams(("parallel", "arbitrary")),
        name="retention",
    )(decay_logit.astype(F32), cq, ck, cv, cg, xq, xk, xv, xg)


def _outproj_kernel(x_ref, yna_ref, ydf_ref, yrt_ref, mod_ref, wo_ref, g2_ref, wq_ref, sk_ref,
                    xo_ref, h2_ref, st_ref, *, d):
    y = (_dot(yna_ref[0], wo_ref[0:NA_W, :])
         + _dot(ydf_ref[0], wo_ref[NA_W:NA_W + DIFF_W, :])
         + _dot(yrt_ref[0].astype(BF16), wo_ref[NA_W + DIFF_W:MIX_W, :]))
    mod = mod_ref[0]
    x = x_ref[0] + mod[:, 2 * d:3 * d] * y
    xo_ref[0] = x
    h2 = (_rms(x) * g2_ref[...] * (1.0 + mod[:, 4 * d:5 * d]) + mod[:, 3 * d:4 * d]).astype(BF16)
    h2_ref[0] = h2
    qp = _dot(h2, wq_ref[...]).astype(BF16)
    for hp in range(2 * PEER_HEADS):
        st_ref[0, hp] = _nt(sk_ref[hp], qp[:, hp * PEER_HALF:(hp + 1) * PEER_HALF])


def _outproj(x, yna, ydf, yrt, mod3, mod_row, wo, g2, wq, sk, tm):
    b, l, d = x.shape
    nhp = 2 * PEER_HEADS

    def tok(w):
        return pl.BlockSpec((1, tm, w), lambda i, t: (i, t, 0))

    return pl.pallas_call(
        functools.partial(_outproj_kernel, d=d),
        grid=(b, l // tm),
        in_specs=[tok(d), tok(NA_W), tok(DIFF_W), tok(RET_W),
                  pl.BlockSpec((1, 1, 6 * d), lambda i, t: (mod_row(i), 0, 0)),
                  pl.BlockSpec(wo.shape, lambda i, t: (0, 0)),
                  pl.BlockSpec((1, d), lambda i, t: (0, 0)),
                  pl.BlockSpec(wq.shape, lambda i, t: (0, 0)),
                  pl.BlockSpec(sk.shape, lambda i, t: (0, 0, 0))],
        out_specs=[tok(d), tok(d),
                   pl.BlockSpec((1, nhp, PEER_NKEYS, tm), lambda i, t: (i, 0, 0, t))],
        out_shape=[jax.ShapeDtypeStruct((b, l, d), F32), jax.ShapeDtypeStruct((b, l, d), BF16),
                   jax.ShapeDtypeStruct((b, nhp, PEER_NKEYS, l), F32)],
        compiler_params=_cparams(("parallel", "parallel")),
        name="outproj",
    )(x, yna, ydf, yrt, mod3, wo, g2, wq, sk)


def _topk_kernel(st_ref, a1_ref, a2_ref, tau_ref, cand_ref, cand2_ref):
    neg = -jnp.inf
    tl = st_ref.shape[-1]
    pad = jnp.full((PEER_NCAND - len(PEER_PAIRS), tl), neg, F32)
    cand_ref[len(PEER_PAIRS):, :] = pad
    cand2_ref[len(PEER_PAIRS):, :] = pad

    def top_rows(work):
        rows = []
        for k in range(PEER_TOPK):
            m = jnp.max(work, axis=0, keepdims=True)
            rows.append(m)
            if k + 1 < PEER_TOPK:
                work = jnp.where(work == m, neg, work)
        return rows

    for h in range(PEER_HEADS):
        s1 = st_ref[0, 2 * h]
        s2 = st_ref[0, 2 * h + 1]
        a1 = s1 - jnp.max(s1, axis=0, keepdims=True)
        a2 = s2 - jnp.max(s2, axis=0, keepdims=True)
        t1 = top_rows(a1)
        t2 = top_rows(a2)
        for r, (a, b) in enumerate(PEER_PAIRS):
            cand_ref[r:r + 1, :] = t1[a] + t2[b]
        cand = cand_ref[...]
        tau = top_rows(cand)[-1]
        sel = cand >= tau
        z = jnp.sum(jnp.where(sel, jnp.exp(cand), 0.0), axis=0, keepdims=True)
        lz = jnp.log(z)
        for r, (a, b) in enumerate(PEER_PAIRS):
            cand2_ref[r:r + 1, :] = (t1[a] - lz) + t2[b]
        tau2 = jnp.min(jnp.where(sel, cand2_ref[...], jnp.inf), axis=0, keepdims=True)
        a1_ref[0, h] = a1 - lz
        a2_ref[0, h] = a2
        tau_ref[0, h:h + 1, :] = tau2


def _topk(st, tl):
    b, nhp, nk, l = st.shape
    hspec = pl.BlockSpec((1, PEER_HEADS, nk, tl), lambda i, t: (i, 0, 0, t))
    return pl.pallas_call(
        _topk_kernel,
        grid=(b, l // tl),
        in_specs=[pl.BlockSpec((1, nhp, nk, tl), lambda i, t: (i, 0, 0, t))],
        out_specs=[hspec, hspec, pl.BlockSpec((1, PEER_HEADS, tl), lambda i, t: (i, 0, t))],
        out_shape=[jax.ShapeDtypeStruct((b, PEER_HEADS, nk, l), F32),
                   jax.ShapeDtypeStruct((b, PEER_HEADS, nk, l), F32),
                   jax.ShapeDtypeStruct((b, PEER_HEADS, l), F32)],
        scratch_shapes=[pltpu.VMEM((PEER_NCAND, tl), F32), pltpu.VMEM((PEER_NCAND, tl), F32)],
        compiler_params=_cparams(("parallel", "parallel")),
        name="peer_topk",
    )(st)


def _peer_kernel(*refs, final, d, ich):
    if final:
        h2_ref, u_ref, vt_ref, a1_ref, a2_ref, tau_ref, x_ref, mod_ref, fg_ref, o_ref, acc_ref, gw_ref = refs
    else:
        h2_ref, u_ref, vt_ref, a1_ref, a2_ref, tau_ref, x_ref, mod_ref, o_ref, acc_ref, gw_ref = refs
    c = pl.program_id(2)

    @pl.when(c == 0)
    def _():
        acc_ref[...] = jnp.zeros_like(acc_ref)

    at = _nt(u_ref[...], h2_ref[0])
    for ii in range(ich):
        a = at[ii * PEER_NKEYS:(ii + 1) * PEER_NKEYS, :]
        w = jnp.zeros_like(a)
        for h in range(PEER_HEADS):
            y = a2_ref[0, h] + a1_ref[0, h, ii:ii + 1, :]
            w = w + jnp.where(y >= tau_ref[0, h:h + 1, :], jnp.exp(y), 0.0)
        gelu = 0.5 * a * (1.0 + lax.erf(a * math.sqrt(0.5)))
        gw_ref[ii * PEER_NKEYS:(ii + 1) * PEER_NKEYS, :] = (w * gelu).astype(BF16)
    acc_ref[...] += _dot(vt_ref[...], gw_ref[...])

    @pl.when(c == pl.num_programs(2) - 1)
    def _():
        mod = mod_ref[0]
        xo = x_ref[0] + mod[:, 5 * d:6 * d] * acc_ref[...].T
        if final:
            xo = _rms(xo) * fg_ref[...]
        o_ref[0] = xo


def _peer(h2, u, vt, a1p, a2, tau, x, mod3, mod_row, final_g, tm, ich=8):
    b, l, d = x.shape
    ne = u.shape[0]
    ec = ich * PEER_NKEYS
    final = final_g is not None
    in_specs = [pl.BlockSpec((1, tm, d), lambda i, t, c: (i, t, 0)),
                pl.BlockSpec((ec, d), lambda i, t, c: (c, 0)),
                pl.BlockSpec((d, ec), lambda i, t, c: (0, c)),
                pl.BlockSpec((1, PEER_HEADS, ich, tm), lambda i, t, c: (i, 0, c, t)),
                pl.BlockSpec((1, PEER_HEADS, PEER_NKEYS, tm), lambda i, t, c: (i, 0, 0, t)),
                pl.BlockSpec((1, PEER_HEADS, tm), lambda i, t, c: (i, 0, t)),
                pl.BlockSpec((1, tm, d), lambda i, t, c: (i, t, 0)),
                pl.BlockSpec((1, 1, 6 * d), lambda i, t, c: (mod_row(i), 0, 0))]
    args = [h2, u, vt, a1p, a2, tau, x, mod3]
    if final:
        in_specs.append(pl.BlockSpec((1, d), lambda i, t, c: (0, 0)))
        args.append(final_g)
    return pl.pallas_call(
        functools.partial(_peer_kernel, final=final, d=d, ich=ich),
        grid=(b, l // tm, ne // ec),
        in_specs=in_specs,
        out_specs=pl.BlockSpec((1, tm, d), lambda i, t, c: (i, t, 0)),
        out_shape=jax.ShapeDtypeStruct((b, l, d), F32),
        scratch_shapes=[pltpu.VMEM((d, tm), F32), pltpu.VMEM((ec, tm), BF16)],
        compiler_params=_cparams(("parallel", "parallel", "arbitrary")),
        name="peer_final" if final else "peer",
    )(*args)


def _rope_tables(l, dim, width):
    t = jnp.arange(l)
    rows = (t // GRID_W).astype(F32)
    cols = (t % GRID_W).astype(F32)
    half = dim // 2
    inv = jnp.power(ROPE_BASE, -jnp.arange(0, half, 2, dtype=F32) / half)
    ang = jnp.concatenate([rows[:, None] * inv, cols[:, None] * inv], axis=-1)
    pair = (np.arange(width) % dim) // 2
    sign = np.where(np.arange(width) % 2 == 0, -1.0, 1.0).astype(np.float32)
    return jnp.cos(ang)[:, pair], jnp.sin(ang)[:, pair] * sign


def _swap_pairs(w):
    d, n = w.shape
    return w.reshape(d, n // 2, 2)[:, :, ::-1].reshape(d, n)


def _block_mixer_and_peer(x, ctx, mod3, layer, need_ctx, p, tables, final_g):
    b, l, d = x.shape
    lc = ctx.shape[1]
    row_x = lambda i: i
    row_c = lambda i: b
    lam_init = 0.8 - 0.6 * math.exp(-0.3 * layer)

    lat = _inproj(x, mod3, row_x, p["g1"], p["w_in"], tables, 256)
    cx = _inproj(ctx, mod3, row_c, p["g1"], p["w_in"], None, lc)
    naq, nak, nav, dfq, dfk, dfv, rtq, rtk, rtv, rtg = lat
    cnaq, cnak, cnav, cdfq, cdfk, cdfv, crtq, crtk, crtv, crtg = cx

    y_na = _na_latent(naq, nak, nav, cnak, cnav, p["rpb"], 4)
    y_df = _diff(dfq, cdfk, cdfv, dfk, dfv, p["lam"], p["subln"], lam_init, 256)
    y_rt, yc_rt = _retention(p["decay"], crtq, crtk, crtv, crtg, rtq, rtk, rtv, rtg)

    def channel_mix(xx, yna, ydf, yrt, row, tm, fg):
        xo, h2, st = _outproj(xx, yna, ydf, yrt, mod3, row, p["w_out"], p["g2"], p["wq"], p["sk"], tm)
        a1p, a2, tau = _topk(st, tm)
        return _peer(h2, p["u"], p["vt"], a1p, a2, tau, xo, mod3, row, fg, tm)

    x = channel_mix(x, y_na, y_df, y_rt, row_x, 256, final_g)
    if need_ctx:
        yc_na = _na_ctx(cnaq, cnak, cnav)
        yc_df = _diff(cdfq, cdfk, cdfv, None, None, p["lam"], p["subln"], lam_init, lc)
        ctx = channel_mix(ctx, yc_na, yc_df, yc_rt, row_c, lc, None)
    return x, ctx


def kernel(x, c, ctx, c_ctx, w_ada, b_ada, norm1_g, w_in, na_rpb, diff_lambda, diff_subln_g,
           ret_decay_logit, w_out, norm2_g, peer_wq, peer_subkeys, peer_u, peer_v, final_g):
    b, l, d = x.shape
    depth = w_ada.shape[0]
    x = x.astype(F32)
    ctx = ctx.astype(F32)
    npad = -(-(b + 1) // 8) * 8
    cpad = jnp.zeros((npad, d), F32).at[:b].set(c.astype(F32)).at[b].set(c_ctx.astype(F32))
    mod = _ada(cpad, w_ada.astype(F32), b_ada.astype(F32))
    tables = _rope_tables(l, DIFF_DIM, DIFF_QK_W) + _rope_tables(l, RET_QK, RET_QK_W)
    for layer in range(depth):
        wi = w_in[layer]
        w_ext = jnp.concatenate(
            [wi] + [_swap_pairs(wi[:, c0:c0 + 256]) for c0 in (C_DFQ, C_DFK, C_RTQ, C_RTK)], axis=1)
        p = {
            "g1": norm1_g[layer].astype(F32).reshape(1, d),
            "g2": norm2_g[layer].astype(F32).reshape(1, d),
            "w_in": w_ext.astype(BF16),
            "rpb": na_rpb[layer],
            "lam": diff_lambda[layer].astype(F32),
            "subln": jnp.tile(diff_subln_g[layer].astype(F32), DIFF_HEADS).reshape(1, DIFF_W),
            "decay": ret_decay_logit[layer],
            "w_out": w_out[layer].astype(BF16),
            "wq": peer_wq[layer].astype(BF16),
            "sk": peer_subkeys[layer].astype(BF16).reshape(2 * PEER_HEADS, PEER_NKEYS, PEER_HALF),
            "u": peer_u[layer].astype(BF16),
            "vt": peer_v[layer].astype(BF16).T,
        }
        last = layer == depth - 1
        x, ctx = _block_mixer_and_peer(x, ctx, mod[layer].reshape(npad, 1, 6 * d), layer, not last, p,
                                       tables, final_g.astype(F32).reshape(1, d) if last else None)
    return x
```

```python
import functools
import math

import numpy as np
import jax
import jax.numpy as jnp
from jax import lax
from jax.experimental import pallas as pl
from jax.experimental.pallas import tpu as pltpu

F32 = jnp.float32
BF16 = jnp.bfloat16

GRID_W = 64
NORM_EPS = 1e-6
ROPE_BASE = 10000.0
NEG_INF = -1e30

NA_HEADS = 4
NA_DIM = 64
NA_WIN_ROWS = 8
NA_WIN_COLS = 16
DIFF_HEADS = 4
DIFF_DIM = 32
DIFF_VDIM = 64
RET_HEADS = 4
RET_QK = 64
RET_V = 128
RET_CHUNK = 128

NA_W = NA_HEADS * NA_DIM
DIFF_QK_W = DIFF_HEADS * 2 * DIFF_DIM
DIFF_W = DIFF_HEADS * DIFF_VDIM
RET_QK_W = RET_HEADS * RET_QK
RET_W = RET_HEADS * RET_V
MIX_W = NA_W + DIFF_W + RET_W
IN_COLS = 3 * NA_W + 2 * DIFF_QK_W + DIFF_W + 2 * RET_QK_W + 2 * RET_W

PEER_HEADS = 8
PEER_NKEYS = 128
PEER_KDIM = 256
PEER_TOPK = 16
PEER_HALF = PEER_KDIM // 2

C_NAQ, C_NAK, C_NAV = 0, 256, 512
C_DFQ, C_DFK, C_DFV = 768, 1024, 1280
C_RTQ, C_RTK, C_RTV, C_RTG = 1536, 1792, 2048, 2560
C_DFQ_S, C_DFK_S, C_RTQ_S, C_RTK_S = 3072, 3328, 3584, 3840
IN_COLS_EXT = 4096

PEER_PAIRS = tuple((a, b) for a in range(PEER_TOPK) for b in range(PEER_TOPK // (a + 1)))
PEER_NCAND = 56

VMEM_LIMIT = 56 * 1024 * 1024


def _cparams(sem):
    return pltpu.CompilerParams(dimension_semantics=sem, vmem_limit_bytes=VMEM_LIMIT)


def _nt(a, b):
    return lax.dot_general(a, b, (((1,), (1,)), ((), ())), preferred_element_type=F32)


def _tn(a, b):
    return lax.dot_general(a, b, (((0,), (0,)), ((), ())), preferred_element_type=F32)


def _dot(a, b):
    return jnp.dot(a, b, preferred_element_type=F32)


def _rms(x):
    return x * lax.rsqrt(jnp.mean(x * x, axis=-1, keepdims=True) + NORM_EPS)


def _silu(x):
    return x * jax.nn.sigmoid(x)


def _ada_kernel(c_ref, w_ref, b_ref, o_ref):
    s = _silu(c_ref[...])
    o_ref[0] = jnp.dot(s, w_ref[0], preferred_element_type=F32,
                       precision=lax.Precision.HIGHEST) + b_ref[0]


def _ada(cpad, w_ada, b_ada):
    depth, d, n = w_ada.shape
    tn = 1536
    return pl.pallas_call(
        _ada_kernel,
        grid=(depth, n // tn),
        in_specs=[pl.BlockSpec((cpad.shape[0], d), lambda l, j: (0, 0)),
                  pl.BlockSpec((1, d, tn), lambda l, j: (l, 0, j)),
                  pl.BlockSpec((1, 1, tn), lambda l, j: (l, 0, j))],
        out_specs=pl.BlockSpec((1, cpad.shape[0], tn), lambda l, j: (l, 0, j)),
        out_shape=jax.ShapeDtypeStruct((depth, cpad.shape[0], n), F32),
        compiler_params=_cparams(("parallel", "parallel")),
        name="ada",
    )(cpad, w_ada, b_ada.reshape(depth, 1, n))


def _inproj_kernel(*refs, rope, d):
    if rope:
        (x_ref, mod_ref, g_ref, w_ref, cd_ref, sd_ref, cr_ref, sr_ref,
         naq, nak, nav, dfq, dfk, dfv, rtq, rtk, rtv, rtg) = refs
    else:
        (x_ref, mod_ref, g_ref, w_ref,
         naq, nak, nav, dfq, dfk, dfv, rtq, rtk, rtv, rtg) = refs
    mod = mod_ref[0]
    h = _rms(x_ref[0]) * g_ref[...] * (1.0 + mod[:, d:2 * d]) + mod[:, 0:d]
    hb = h.astype(BF16)

    def proj(c0, n):
        return _dot(hb, w_ref[:, c0:c0 + n])

    def roped(c0, c0s, cos_ref, sin_ref):
        if rope:
            return proj(c0, 256) * cos_ref[...] + proj(c0s, 256) * sin_ref[...]
        return proj(c0, 256)

    naq[0] = (proj(C_NAQ, 256) * NA_DIM ** -0.5).astype(BF16)
    nak[0] = proj(C_NAK, 256).astype(BF16)
    nav[0] = proj(C_NAV, 256).astype(BF16)
    dfq[0] = (roped(C_DFQ, C_DFQ_S, cd_ref if rope else None, sd_ref if rope else None)
              * DIFF_DIM ** -0.5).astype(BF16)
    dfk[0] = roped(C_DFK, C_DFK_S, cd_ref if rope else None, sd_ref if rope else None).astype(BF16)
    dfv[0] = proj(C_DFV, 256).astype(BF16)
    rtq[0] = roped(C_RTQ, C_RTQ_S, cr_ref if rope else None, sr_ref if rope else None).astype(BF16)
    rtk[0] = (roped(C_RTK, C_RTK_S, cr_ref if rope else None, sr_ref if rope else None)
              * RET_QK ** -0.5).astype(BF16)
    rtv[0] = proj(C_RTV, 512).astype(BF16)
    rtg[0] = proj(C_RTG, 512)


def _inproj(x, mod3, mod_row, g1, w_ext, tables, tm):
    b, l, d = x.shape
    rope = tables is not None
    ncols = IN_COLS_EXT if rope else IN_COLS
    in_specs = [pl.BlockSpec((1, tm, d), lambda i, t: (i, t, 0)),
                pl.BlockSpec((1, 1, 6 * d), lambda i, t: (mod_row(i), 0, 0)),
                pl.BlockSpec((1, d), lambda i, t: (0, 0)),
                pl.BlockSpec((d, ncols), lambda i, t: (0, 0))]
    args = [x, mod3, g1, w_ext]
    if rope:
        in_specs += [pl.BlockSpec((tm, 256), lambda i, t: (t, 0))] * 4
        args += list(tables)
    widths = (256, 256, 256, 256, 256, 256, 256, 256, 512, 512)
    dtypes = (BF16,) * 9 + (F32,)
    out_specs = [pl.BlockSpec((1, tm, w), lambda i, t: (i, t, 0)) for w in widths]
    out_shape = [jax.ShapeDtypeStruct((b, l, w), dt) for w, dt in zip(widths, dtypes)]
    return pl.pallas_call(
        functools.partial(_inproj_kernel, rope=rope, d=d),
        grid=(b, l // tm),
        in_specs=in_specs, out_specs=out_specs, out_shape=out_shape,
        compiler_params=_cparams(("parallel", "parallel")),
        name="inproj_rope" if rope else "inproj_ctx",
    )(*args)


def _na_plan(rows, rb):
    wr = min(NA_WIN_ROWS, rows)
    kr = min(rb + wr - 1, rows)
    nblk = rows // rb
    win0 = np.clip(np.arange(rows) - wr // 2, 0, rows - wr)
    ks = np.clip(np.arange(nblk) * rb - wr // 2, 0, rows - kr)
    pats, pat_of = [], np.zeros(nblk, np.int32)
    for g in range(nblk):
        r = g * rb + np.arange(rb)
        krow = ks[g] + np.arange(kr)
        dr = krow[None, :] - r[:, None] + (NA_WIN_ROWS - 1)
        ok = (krow[None, :] >= win0[r][:, None]) & (krow[None, :] < win0[r][:, None] + wr)
        assert ok.sum(axis=1).min() == wr
        key = (np.where(ok, dr, 0).tobytes(), ok.tobytes())
        for p, (k2, _, _) in enumerate(pats):
            if k2 == key:
                pat_of[g] = p
                break
        else:
            pat_of[g] = len(pats)
            pats.append((key, np.where(ok, dr, 0), ok))
    dr_idx = np.stack([p[1] for p in pats])
    row_ok = np.stack([p[2] for p in pats])
    qcol = np.arange(GRID_W)
    kcol = np.arange(GRID_W)
    cs = np.clip(qcol - NA_WIN_COLS // 2, 0, GRID_W - NA_WIN_COLS)
    col_ok = (kcol[None, :] >= cs[:, None]) & (kcol[None, :] < cs[:, None] + NA_WIN_COLS)
    dc_idx = np.clip(kcol[None, :] - qcol[:, None], 1 - NA_WIN_COLS, NA_WIN_COLS - 1) + NA_WIN_COLS - 1
    return kr, ks.astype(np.int32), pat_of, dr_idx, row_ok, dc_idx, col_ok


def _na_bias(rpb, plan, rb):
    kr, _, _, dr_idx, row_ok, dc_idx, col_ok = plan
    p = dr_idx.shape[0]
    nr, nc = 2 * NA_WIN_ROWS - 1, 2 * NA_WIN_COLS - 1
    row_hot = ((np.arange(nr) == dr_idx[..., None]) & row_ok[..., None]).astype(np.float32)
    col_hot = ((np.arange(nc) == dc_idx[..., None]) & col_ok[..., None]).astype(np.float32)
    bias = jnp.einsum("hrc,pqwr,xyc->hpqxwy", rpb.astype(F32), row_hot, col_hot,
                      precision=lax.Precision.HIGHEST)
    ok = row_ok[:, :, None, :, None] & col_ok[None, None, :, None, :]
    bias = jnp.where(ok[None], bias, NEG_INF)
    return bias.reshape(NA_HEADS, p, rb * GRID_W, kr * GRID_W)


def _na_kernel(ks_ref, pat_ref, q_ref, kx_ref, vx_ref, kc_ref, vc_ref, bias_ref, o_ref, *, nkeys):
    g = pl.program_id(1)
    q = q_ref[0]
    k0 = pl.multiple_of(ks_ref[g] * GRID_W, GRID_W)
    kw = kx_ref[0, pl.ds(k0, nkeys), :]
    vw = vx_ref[0, pl.ds(k0, nkeys), :]
    kc = kc_ref[0]
    vc = vc_ref[0]
    pat = pat_ref[g]
    lane = lax.broadcasted_iota(jnp.int32, (1, NA_W), 1)
    out = jnp.zeros((q.shape[0], NA_W), F32)
    for h in range(NA_HEADS):
        hm = (lane >= NA_DIM * h) & (lane < NA_DIM * (h + 1))
        qh = jnp.where(hm, q, jnp.zeros_like(q))
        sw = _nt(qh, kw) + bias_ref[h, pat]
        sc = _nt(qh, kc)
        m = jnp.maximum(jnp.max(sw, axis=-1, keepdims=True), jnp.max(sc, axis=-1, keepdims=True))
        pw = jnp.exp(sw - m)
        pc = jnp.exp(sc - m)
        den = jnp.sum(pw, axis=-1, keepdims=True) + jnp.sum(pc, axis=-1, keepdims=True)
        o = _dot(pw.astype(BF16), vw) + _dot(pc.astype(BF16), vc)
        out = out + jnp.where(hm, o / den, 0.0)
    o_ref[0] = out.astype(o_ref.dtype)


def _na_latent(q, kx, vx, kc, vc, rpb, rb):
    b, l, _ = q.shape
    lc = kc.shape[1]
    rows = l // GRID_W
    plan = _na_plan(rows, rb)
    kr, ks, pat_of = plan[0], plan[1], plan[2]
    bias = _na_bias(rpb, plan, rb)
    tq, nkeys = rb * GRID_W, kr * GRID_W
    grid_spec = pltpu.PrefetchScalarGridSpec(
        num_scalar_prefetch=2,
        grid=(b, rows // rb),
        in_specs=[pl.BlockSpec((1, tq, NA_W), lambda i, g, *_: (i, g, 0)),
                  pl.BlockSpec((1, l, NA_W), lambda i, g, *_: (i, 0, 0)),
                  pl.BlockSpec((1, l, NA_W), lambda i, g, *_: (i, 0, 0)),
                  pl.BlockSpec((1, lc, NA_W), lambda i, g, *_: (i, 0, 0)),
                  pl.BlockSpec((1, lc, NA_W), lambda i, g, *_: (i, 0, 0)),
                  pl.BlockSpec(bias.shape, lambda i, g, *_: (0, 0, 0, 0))],
        out_specs=pl.BlockSpec((1, tq, NA_W), lambda i, g, *_: (i, g, 0)),
    )
    return pl.pallas_call(
        functools.partial(_na_kernel, nkeys=nkeys),
        grid_spec=grid_spec,
        out_shape=jax.ShapeDtypeStruct((b, l, NA_W), BF16),
        compiler_params=_cparams(("parallel", "arbitrary")),
        name="na_latent",
    )(jnp.asarray(ks), jnp.asarray(pat_of), q, kx, vx, kc, vc, bias)


def _na_ctx_kernel(q_ref, kc_ref, vc_ref, o_ref):
    q = q_ref[0]
    kc = kc_ref[0]
    vc = vc_ref[0]
    lane = lax.broadcasted_iota(jnp.int32, (1, NA_W), 1)
    out = jnp.zeros((q.shape[0], NA_W), F32)
    for h in range(NA_HEADS):
        hm = (lane >= NA_DIM * h) & (lane < NA_DIM * (h + 1))
        qh = jnp.where(hm, q, jnp.zeros_like(q))
        sc = _nt(qh, kc)
        pc = jnp.exp(sc - jnp.max(sc, axis=-1, keepdims=True))
        o = _dot(pc.astype(BF16), vc)
        out = out + jnp.where(hm, o / jnp.sum(pc, axis=-1, keepdims=True), 0.0)
    o_ref[0] = out.astype(o_ref.dtype)


def _na_ctx(q, kc, vc):
    b, lc, _ = q.shape
    spec = pl.BlockSpec((1, lc, NA_W), lambda i: (i, 0, 0))
    return pl.pallas_call(
        _na_ctx_kernel, grid=(b,), in_specs=[spec, spec, spec], out_specs=spec,
        out_shape=jax.ShapeDtypeStruct((b, lc, NA_W), BF16),
        compiler_params=_cparams(("parallel",)), name="na_ctx",
    )(q, kc, vc)


def _diff_kernel(*refs, lam_init, has_x):
    if has_x:
        q_ref, kc_ref, vc_ref, kx_ref, vx_ref, lam_ref, g_ref, o_ref = refs
    else:
        q_ref, kc_ref, vc_ref, lam_ref, g_ref, o_ref = refs
    q = q_ref[0]
    kc = kc_ref[0]
    vc = vc_ref[0]
    lp = lam_ref[...]
    lam = (jnp.exp(jnp.sum(lp[0:1] * lp[1:2], axis=-1, keepdims=True))
           - jnp.exp(jnp.sum(lp[2:3] * lp[3:4], axis=-1, keepdims=True)) + lam_init)
    lane = lax.broadcasted_iota(jnp.int32, (1, DIFF_W), 1)
    out = jnp.zeros((q.shape[0], DIFF_W), F32)
    for h in range(DIFF_HEADS):
        parts = []
        for c in range(2):
            lo = DIFF_VDIM * h + DIFF_DIM * c
            qm = jnp.where((lane >= lo) & (lane < lo + DIFF_DIM), q, jnp.zeros_like(q))
            sc = _nt(qm, kc)
            m = jnp.max(sc, axis=-1, keepdims=True)
            if has_x:
                sx = _nt(qm, kx_ref[0])
                m = jnp.maximum(m, jnp.max(sx, axis=-1, keepdims=True))
            ec = jnp.exp(sc - m)
            den = jnp.sum(ec, axis=-1, keepdims=True)
            ex = None
            if has_x:
                ex = jnp.exp(sx - m)
                den = den + jnp.sum(ex, axis=-1, keepdims=True)
            parts.append((ec, ex, 1.0 / den))
        (ec0, ex0, r0), (ec1, ex1, r1) = parts
        r1 = lam * r1
        o = _dot((ec0 * r0 - ec1 * r1).astype(BF16), vc)
        if has_x:
            o = o + _dot((ex0 * r0 - ex1 * r1).astype(BF16), vx_ref[0])
        hm = (lane >= DIFF_VDIM * h) & (lane < DIFF_VDIM * (h + 1))
        oh = jnp.where(hm, o, 0.0)
        ms = jnp.sum(oh * oh, axis=-1, keepdims=True) * (1.0 / DIFF_VDIM)
        out = out + oh * lax.rsqrt(ms + NORM_EPS)
    o_ref[0] = (out * g_ref[...] * (1.0 - lam_init)).astype(o_ref.dtype)


def _diff(q, kc, vc, kx, vx, lam_p, g4, lam_init, tq):
    b, l, _ = q.shape
    lc = kc.shape[1]
    has_x = kx is not None
    cspec = pl.BlockSpec((1, lc, DIFF_W), lambda i, t: (i, 0, 0))
    in_specs = [pl.BlockSpec((1, tq, DIFF_W), lambda i, t: (i, t, 0)), cspec, cspec]
    args = [q, kc, vc]
    if has_x:
        xspec = pl.BlockSpec((1, kx.shape[1], DIFF_W), lambda i, t: (i, 0, 0))
        in_specs += [xspec, xspec]
        args += [kx, vx]
    in_specs += [pl.BlockSpec(lam_p.shape, lambda i, t: (0, 0)),
                 pl.BlockSpec((1, DIFF_W), lambda i, t: (0, 0))]
    args += [lam_p, g4]
    return pl.pallas_call(
        functools.partial(_diff_kernel, lam_init=lam_init, has_x=has_x),
        grid=(b, l // tq), in_specs=in_specs,
        out_specs=pl.BlockSpec((1, tq, DIFF_W), lambda i, t: (i, t, 0)),
        out_shape=jax.ShapeDtypeStruct((b, l, DIFF_W), BF16),
        compiler_params=_cparams(("parallel", "arbitrary")),
        name="diff_latent" if has_x else "diff_ctx",
    )(*args)


def _ret_kernel(dl_ref, cq_ref, ck_ref, cv_ref, cg_ref, xq_ref, xk_ref, xv_ref, xg_ref,
                yx_ref, yc_ref, s_ref, *, n_ctx, n_lat):
    h = pl.program_id(1)
    c = RET_CHUNK
    lane = lax.broadcasted_iota(jnp.int32, (1, RET_QK_W), 1)
    hm = (lane >= RET_QK * h) & (lane < RET_QK * (h + 1))
    row = lax.broadcasted_iota(jnp.int32, (c, c), 0).astype(F32)
    col = lax.broadcasted_iota(jnp.int32, (c, c), 1).astype(F32)
    roww = lax.broadcasted_iota(jnp.int32, (c, RET_QK_W), 0).astype(F32)

    def log_sigmoid(d):
        x = jnp.full((1, 1), dl_ref[d, h], F32)
        return jnp.minimum(x, 0.0) - jnp.log1p(jnp.exp(-jnp.abs(x)))

    lgf, lgb = log_sigmoid(0), log_sigmoid(1)
    diff = row - col
    fwd = (jnp.where(diff >= 0, jnp.exp(lgf * jnp.maximum(diff, 0.0)), 0.0),
           jnp.exp(lgf * (roww + 1.0)),
           jnp.exp(lgf * (c - 1.0 - roww)),
           jnp.exp(lgf * c))
    bwd = (jnp.where(diff <= 0, jnp.exp(lgb * jnp.maximum(-diff, 0.0)), 0.0),
           jnp.exp(lgb * (c - roww)),
           jnp.exp(lgb * roww),
           jnp.exp(lgb * c))

    def step(q_ref, k_ref, v_ref, g_ref, out_ref, ci, mats, second):
        dm, qd, kd, cd = mats
        sl = pl.ds(pl.multiple_of(ci * c, c), c)
        q = q_ref[0, sl, :]
        q = jnp.where(hm, q, jnp.zeros_like(q))
        k = k_ref[0, sl, :]
        v = v_ref[0, sl, :]
        att = _nt(q, k) * dm
        s = s_ref[...]
        o = _dot(att.astype(BF16), v) + _dot((q.astype(F32) * qd).astype(BF16), s.astype(BF16))
        s_ref[...] = cd * s + _tn((k.astype(F32) * kd).astype(BF16), v)
        if second:
            tot = out_ref[0, sl, :] + o
            out_ref[0, sl, :] = _rms(tot) * _silu(g_ref[0, sl, :])
        else:
            out_ref[0, sl, :] = o

    def sweep(mats, second, reverse):
        s_ref[...] = jnp.zeros_like(s_ref)
        for i in range(n_ctx):
            ci = n_ctx - 1 - i if reverse else i
            step(cq_ref, ck_ref, cv_ref, cg_ref, yc_ref, ci, mats, second)

        def body(i, carry):
            ci = n_lat - 1 - i if reverse else i
            step(xq_ref, xk_ref, xv_ref, xg_ref, yx_ref, ci, mats, second)
            return carry

        lax.fori_loop(0, n_lat, body, 0)

    sweep(fwd, False, False)
    sweep(bwd, True, True)


def _retention(decay_logit, cq, ck, cv, cg, xq, xk, xv, xg):
    b, l, _ = xq.shape
    lc = cq.shape[1]

    def qk_spec(n):
        return pl.BlockSpec((1, n, RET_QK_W), lambda i, h: (i, 0, 0))

    def v_spec(n):
        return pl.BlockSpec((1, n, RET_V), lambda i, h: (i, 0, h))

    return pl.pallas_call(
        functools.partial(_ret_kernel, n_ctx=lc // RET_CHUNK, n_lat=l // RET_CHUNK),
        grid=(b, RET_HEADS),
        in_specs=[pl.BlockSpec(memory_space=pltpu.SMEM),
                  qk_spec(lc), qk_spec(lc), v_spec(lc), v_spec(lc),
                  qk_spec(l), qk_spec(l), v_spec(l), v_spec(l)],
        out_specs=[v_spec(l), v_spec(lc)],
        out_shape=[jax.ShapeDtypeStruct((b, l, RET_W), F32), jax.ShapeDtypeStruct((b, lc, RET_W), F32)],
        scratch_shapes=[pltpu.VMEM((RET_QK_W, RET_V), F32)],
        compiler_params=_cparams(("parallel", "arbitrary")),
        name="retention",
    )(decay_logit.astype(F32), cq, ck, cv, cg, xq, xk, xv, xg)


def _outproj_kernel(x_ref, yna_ref, ydf_ref, yrt_ref, mod_ref, wo_ref, g2_ref, wq_ref, sk_ref,
                    xo_ref, h2_ref, st_ref, *, d):
    y = (_dot(yna_ref[0], wo_ref[0:NA_W, :])
         + _dot(ydf_ref[0], wo_ref[NA_W:NA_W + DIFF_W, :])
         + _dot(yrt_ref[0].astype(BF16), wo_ref[NA_W + DIFF_W:MIX_W, :]))
    mod = mod_ref[0]
    x = x_ref[0] + mod[:, 2 * d:3 * d] * y
    xo_ref[0] = x
    h2 = (_rms(x) * g2_ref[...] * (1.0 + mod[:, 4 * d:5 * d]) + mod[:, 3 * d:4 * d]).astype(BF16)
    h2_ref[0] = h2
    qp = _dot(h2, wq_ref[...]).astype(BF16)
    for hp in range(2 * PEER_HEADS):
        st_ref[0, hp] = _nt(sk_ref[hp], qp[:, hp * PEER_HALF:(hp + 1) * PEER_HALF])


def _outproj(x, yna, ydf, yrt, mod3, mod_row, wo, g2, wq, sk, tm):
    b, l, d = x.shape
    nhp = 2 * PEER_HEADS

    def tok(w):
        return pl.BlockSpec((1, tm, w), lambda i, t: (i, t, 0))

    return pl.pallas_call(
        functools.partial(_outproj_kernel, d=d),
        grid=(b, l // tm),
        in_specs=[tok(d), tok(NA_W), tok(DIFF_W), tok(RET_W),
                  pl.BlockSpec((1, 1, 6 * d), lambda i, t: (mod_row(i), 0, 0)),
                  pl.BlockSpec(wo.shape, lambda i, t: (0, 0)),
                  pl.BlockSpec((1, d), lambda i, t: (0, 0)),
                  pl.BlockSpec(wq.shape, lambda i, t: (0, 0)),
                  pl.BlockSpec(sk.shape, lambda i, t: (0, 0, 0))],
        out_specs=[tok(d), tok(d),
                   pl.BlockSpec((1, nhp, PEER_NKEYS, tm), lambda i, t: (i, 0, 0, t))],
        out_shape=[jax.ShapeDtypeStruct((b, l, d), F32), jax.ShapeDtypeStruct((b, l, d), BF16),
                   jax.ShapeDtypeStruct((b, nhp, PEER_NKEYS, l), F32)],
        compiler_params=_cparams(("parallel", "parallel")),
        name="outproj",
    )(x, yna, ydf, yrt, mod3, wo, g2, wq, sk)


def _topk_kernel(st_ref, th_ref, e1_ref, e2_ref, cand_ref):
    neg = -jnp.inf
    tl = st_ref.shape[-1]
    cand_ref[len(PEER_PAIRS):, :] = jnp.full((PEER_NCAND - len(PEER_PAIRS), tl), neg, F32)

    def top_rows(work):
        rows = []
        for k in range(PEER_TOPK):
            m = jnp.max(work, axis=0, keepdims=True)
            rows.append(m)
            if k + 1 < PEER_TOPK:
                work = jnp.where(work == m, neg, work)
        return rows

    for h in range(PEER_HEADS):
        s1 = st_ref[0, 2 * h]
        s2 = st_ref[0, 2 * h + 1]
        a1 = s1 - jnp.max(s1, axis=0, keepdims=True)
        a2 = s2 - jnp.max(s2, axis=0, keepdims=True)
        t1 = top_rows(a1)
        t2 = top_rows(a2)
        for r, (a, b) in enumerate(PEER_PAIRS):
            cand_ref[r:r + 1, :] = t1[a] + t2[b]
        tau = top_rows(cand_ref[...])[-1]
        et1 = [jnp.exp(r) for r in t1]
        et2 = [jnp.exp(r) for r in t2]
        tht = [jnp.exp(tau - r) for r in t1]
        z = jnp.zeros_like(tau)
        for a, b in PEER_PAIRS:
            z = z + jnp.where(et2[b] >= tht[a], et1[a] * et2[b], 0.0)
        th_ref[0, h] = jnp.exp(tau - a1)
        e1_ref[0, h] = jnp.exp(a1) * (0.5 / z)
        e2_ref[0, h] = jnp.exp(a2)


def _topk(st, tl):
    b, nhp, nk, l = st.shape
    hspec = pl.BlockSpec((1, PEER_HEADS, nk, tl), lambda i, t: (i, 0, 0, t))
    hshape = jax.ShapeDtypeStruct((b, PEER_HEADS, nk, l), F32)
    return pl.pallas_call(
        _topk_kernel,
        grid=(b, l // tl),
        in_specs=[pl.BlockSpec((1, nhp, nk, tl), lambda i, t: (i, 0, 0, t))],
        out_specs=[hspec, hspec, hspec],
        out_shape=[hshape, hshape, hshape],
        scratch_shapes=[pltpu.VMEM((PEER_NCAND, tl), F32)],
        compiler_params=_cparams(("parallel", "parallel")),
        name="peer_topk",
    )(st)


def _peer_kernel(*refs, final, d, ich):
    if final:
        h2_ref, u_ref, vt_ref, th_ref, e1_ref, e2_ref, x_ref, mod_ref, fg_ref, o_ref, acc_ref, at_ref, gw_ref = refs
    else:
        h2_ref, u_ref, vt_ref, th_ref, e1_ref, e2_ref, x_ref, mod_ref, o_ref, acc_ref, at_ref, gw_ref = refs
    c = pl.program_id(2)
    tm = h2_ref.shape[1]
    rt = 16

    @pl.when(c == 0)
    def _():
        acc_ref[...] = jnp.zeros_like(acc_ref)

    at_ref[...] = _nt(u_ref[...], h2_ref[0])
    for ii in range(ich):
        for lg in range(tm // 128):
            ls = slice(lg * 128, (lg + 1) * 128)
            th = [jnp.broadcast_to(th_ref[0, h, ii:ii + 1, ls], (8, 128)) for h in range(PEER_HEADS)]
            e1 = [jnp.broadcast_to(e1_ref[0, h, ii:ii + 1, ls], (8, 128)) for h in range(PEER_HEADS)]
            for jg in range(PEER_NKEYS // rt):
                halves = []
                for r0 in range(jg * rt, (jg + 1) * rt, 8):
                    w = None
                    for h in range(PEER_HEADS):
                        e2 = e2_ref[0, h, r0:r0 + 8, ls]
                        term = jnp.where(e2 >= th[h], e2, 0.0) * e1[h]
                        w = term if w is None else w + term
                    a = at_ref[ii * PEER_NKEYS + r0:ii * PEER_NKEYS + r0 + 8, ls]
                    halves.append(a * (1.0 + lax.erf(a * math.sqrt(0.5))) * w)
                rs = slice(ii * PEER_NKEYS + jg * rt, ii * PEER_NKEYS + (jg + 1) * rt)
                gw_ref[rs, ls] = jnp.concatenate(halves, axis=0).astype(BF16)
    acc_ref[...] += _dot(vt_ref[...], gw_ref[...])

    @pl.when(c == pl.num_programs(2) - 1)
    def _():
        mod = mod_ref[0]
        xo = x_ref[0] + mod[:, 5 * d:6 * d] * acc_ref[...].T
        if final:
            xo = _rms(xo) * fg_ref[...]
        o_ref[0] = xo


def _peer(h2, u, vt, th, e1, e2, x, mod3, mod_row, final_g, tm, ich=8):
    b, l, d = x.shape
    ne = u.shape[0]
    ec = ich * PEER_NKEYS
    final = final_g is not None
    ispec = pl.BlockSpec((1, PEER_HEADS, ich, tm), lambda i, t, c: (i, 0, c, t))
    in_specs = [pl.BlockSpec((1, tm, d), lambda i, t, c: (i, t, 0)),
                pl.BlockSpec((ec, d), lambda i, t, c: (c, 0)),
                pl.BlockSpec((d, ec), lambda i, t, c: (0, c)),
                ispec, ispec,
                pl.BlockSpec((1, PEER_HEADS, PEER_NKEYS, tm), lambda i, t, c: (i, 0, 0, t)),
                pl.BlockSpec((1, tm, d), lambda i, t, c: (i, t, 0)),
                pl.BlockSpec((1, 1, 6 * d), lambda i, t, c: (mod_row(i), 0, 0))]
    args = [h2, u, vt, th, e1, e2, x, mod3]
    if final:
        in_specs.append(pl.BlockSpec((1, d), lambda i, t, c: (0, 0)))
        args.append(final_g)
    return pl.pallas_call(
        functools.partial(_peer_kernel, final=final, d=d, ich=ich),
        grid=(b, l // tm, ne // ec),
        in_specs=in_specs,
        out_specs=pl.BlockSpec((1, tm, d), lambda i, t, c: (i, t, 0)),
        out_shape=jax.ShapeDtypeStruct((b, l, d), F32),
        scratch_shapes=[pltpu.VMEM((d, tm), F32), pltpu.VMEM((ec, tm), F32), pltpu.VMEM((ec, tm), BF16)],
        compiler_params=_cparams(("parallel", "parallel", "arbitrary")),
        name="peer_final" if final else "peer",
    )(*args)


def _rope_tables(l, dim, width):
    t = jnp.arange(l)
    rows = (t // GRID_W).astype(F32)
    cols = (t % GRID_W).astype(F32)
    half = dim // 2
    inv = jnp.power(ROPE_BASE, -jnp.arange(0, half, 2, dtype=F32) / half)
    ang = jnp.concatenate([rows[:, None] * inv, cols[:, None] * inv], axis=-1)
    pair = (np.arange(width) % dim) // 2
    sign = np.where(np.arange(width) % 2 == 0, -1.0, 1.0).astype(np.float32)
    return jnp.cos(ang)[:, pair], jnp.sin(ang)[:, pair] * sign


def _swap_pairs(w):
    d, n = w.shape
    return w.reshape(d, n // 2, 2)[:, :, ::-1].reshape(d, n)


def _block_mixer_and_peer(x, ctx, mod3, layer, need_ctx, p, tables, final_g):
    b, l, d = x.shape
    lc = ctx.shape[1]
    row_x = lambda i: i
    row_c = lambda i: b
    lam_init = 0.8 - 0.6 * math.exp(-0.3 * layer)

    lat = _inproj(x, mod3, row_x, p["g1"], p["w_in"], tables, 256)
    cx = _inproj(ctx, mod3, row_c, p["g1"], p["w_in"], None, lc)
    naq, nak, nav, dfq, dfk, dfv, rtq, rtk, rtv, rtg = lat
    cnaq, cnak, cnav, cdfq, cdfk, cdfv, crtq, crtk, crtv, crtg = cx

    y_na = _na_latent(naq, nak, nav, cnak, cnav, p["rpb"], 4)
    y_df = _diff(dfq, cdfk, cdfv, dfk, dfv, p["lam"], p["subln"], lam_init, 256)
    y_rt, yc_rt = _retention(p["decay"], crtq, crtk, crtv, crtg, rtq, rtk, rtv, rtg)

    def channel_mix(xx, yna, ydf, yrt, row, tm, fg):
        xo, h2, st = _outproj(xx, yna, ydf, yrt, mod3, row, p["w_out"], p["g2"], p["wq"], p["sk"], tm)
        th, e1, e2 = _topk(st, min(tm, 256))
        return _peer(h2, p["u"], p["vt"], th, e1, e2, xo, mod3, row, fg, tm)

    x = channel_mix(x, y_na, y_df, y_rt, row_x, min(l, 512), final_g)
    if need_ctx:
        yc_na = _na_ctx(cnaq, cnak, cnav)
        yc_df = _diff(cdfq, cdfk, cdfv, None, None, p["lam"], p["subln"], lam_init, lc)
        ctx = channel_mix(ctx, yc_na, yc_df, yc_rt, row_c, lc, None)
    return x, ctx


def kernel(x, c, ctx, c_ctx, w_ada, b_ada, norm1_g, w_in, na_rpb, diff_lambda, diff_subln_g,
           ret_decay_logit, w_out, norm2_g, peer_wq, peer_subkeys, peer_u, peer_v, final_g):
    b, l, d = x.shape
    depth = w_ada.shape[0]
    x = x.astype(F32)
    ctx = ctx.astype(F32)
    npad = -(-(b + 1) // 8) * 8
    cpad = jnp.zeros((npad, d), F32).at[:b].set(c.astype(F32)).at[b].set(c_ctx.astype(F32))
    mod = _ada(cpad, w_ada.astype(F32), b_ada.astype(F32))
    tables = _rope_tables(l, DIFF_DIM, DIFF_QK_W) + _rope_tables(l, RET_QK, RET_QK_W)
    for layer in range(depth):
        wi = w_in[layer]
        w_ext = jnp.concatenate(
            [wi] + [_swap_pairs(wi[:, c0:c0 + 256]) for c0 in (C_DFQ, C_DFK, C_RTQ, C_RTK)], axis=1)
        p = {
            "g1": norm1_g[layer].astype(F32).reshape(1, d),
            "g2": norm2_g[layer].astype(F32).reshape(1, d),
            "w_in": w_ext.astype(BF16),
            "rpb": na_rpb[layer],
            "lam": diff_lambda[layer].astype(F32),
            "subln": jnp.tile(diff_subln_g[layer].astype(F32), DIFF_HEADS).reshape(1, DIFF_W),
            "decay": ret_decay_logit[layer],
            "w_out": w_out[layer].astype(BF16),
            "wq": peer_wq[layer].astype(BF16),
            "sk": peer_subkeys[layer].astype(BF16).reshape(2 * PEER_HEADS, PEER_NKEYS, PEER_HALF),
            "u": peer_u[layer].astype(BF16),
            "vt": peer_v[layer].astype(BF16).T,
        }
        last = layer == depth - 1
        x, ctx = _block_mixer_and_peer(x, ctx, mod[layer].reshape(npad, 1, 6 * d), layer, not last, p,
                                       tables, final_g.astype(F32).reshape(1, d) if last else None)
    return x
```

```python
import functools
import math

import numpy as np
import jax
import jax.numpy as jnp
from jax import lax
from jax.experimental import pallas as pl
from jax.experimental.pallas import tpu as pltpu

F32 = jnp.float32
BF16 = jnp.bfloat16

GRID_W = 64
NORM_EPS = 1e-6
ROPE_BASE = 10000.0
NEG_INF = -1e30

NA_HEADS = 4
NA_DIM = 64
NA_WIN_ROWS = 8
NA_WIN_COLS = 16
DIFF_HEADS = 4
DIFF_DIM = 32
DIFF_VDIM = 64
RET_HEADS = 4
RET_QK = 64
RET_V = 128
RET_CHUNK = 128

NA_W = NA_HEADS * NA_DIM
DIFF_QK_W = DIFF_HEADS * 2 * DIFF_DIM
DIFF_W = DIFF_HEADS * DIFF_VDIM
RET_QK_W = RET_HEADS * RET_QK
RET_W = RET_HEADS * RET_V
MIX_W = NA_W + DIFF_W + RET_W
IN_COLS = 3 * NA_W + 2 * DIFF_QK_W + DIFF_W + 2 * RET_QK_W + 2 * RET_W

PEER_HEADS = 8
PEER_NKEYS = 128
PEER_KDIM = 256
PEER_TOPK = 16
PEER_HALF = PEER_KDIM // 2

C_NAQ, C_NAK, C_NAV = 0, 256, 512
C_DFQ, C_DFK, C_DFV = 768, 1024, 1280
C_RTQ, C_RTK, C_RTV, C_RTG = 1536, 1792, 2048, 2560
C_DFQ_S, C_DFK_S, C_RTQ_S, C_RTK_S = 3072, 3328, 3584, 3840
IN_COLS_EXT = 4096

PEER_PAIRS = tuple((a, b) for a in range(PEER_TOPK) for b in range(PEER_TOPK // (a + 1)))
PEER_NCAND = 56

VMEM_LIMIT = 56 * 1024 * 1024


def _cparams(sem):
    return pltpu.CompilerParams(dimension_semantics=sem, vmem_limit_bytes=VMEM_LIMIT)


def _nt(a, b):
    return lax.dot_general(a, b, (((1,), (1,)), ((), ())), preferred_element_type=F32)


def _tn(a, b):
    return lax.dot_general(a, b, (((0,), (0,)), ((), ())), preferred_element_type=F32)


def _dot(a, b):
    return jnp.dot(a, b, preferred_element_type=F32)


def _rms(x):
    return x * lax.rsqrt(jnp.mean(x * x, axis=-1, keepdims=True) + NORM_EPS)


def _silu(x):
    return x * jax.nn.sigmoid(x)


def _ada_kernel(c_ref, w_ref, b_ref, o_ref):
    s = _silu(c_ref[...])
    o_ref[0] = jnp.dot(s, w_ref[0], preferred_element_type=F32,
                       precision=lax.Precision.HIGHEST) + b_ref[0]


def _ada(cpad, w_ada, b_ada):
    depth, d, n = w_ada.shape
    tn = 1536
    return pl.pallas_call(
        _ada_kernel,
        grid=(depth, n // tn),
        in_specs=[pl.BlockSpec((cpad.shape[0], d), lambda l, j: (0, 0)),
                  pl.BlockSpec((1, d, tn), lambda l, j: (l, 0, j)),
                  pl.BlockSpec((1, 1, tn), lambda l, j: (l, 0, j))],
        out_specs=pl.BlockSpec((1, cpad.shape[0], tn), lambda l, j: (l, 0, j)),
        out_shape=jax.ShapeDtypeStruct((depth, cpad.shape[0], n), F32),
        compiler_params=_cparams(("parallel", "parallel")),
        name="ada",
    )(cpad, w_ada, b_ada.reshape(depth, 1, n))


def _inproj_kernel(*refs, rope, d):
    if rope:
        (x_ref, mod_ref, g_ref, w_ref, cd_ref, sd_ref, cr_ref, sr_ref,
         naq, nak, nav, dfq, dfk, dfv, rtq, rtk, rtv, rtg) = refs
    else:
        (x_ref, mod_ref, g_ref, w_ref,
         naq, nak, nav, dfq, dfk, dfv, rtq, rtk, rtv, rtg) = refs
    mod = mod_ref[0]
    h = _rms(x_ref[0]) * g_ref[...] * (1.0 + mod[:, d:2 * d]) + mod[:, 0:d]
    hb = h.astype(BF16)

    def proj(c0, n):
        return _dot(hb, w_ref[:, c0:c0 + n])

    def roped(c0, c0s, cos_ref, sin_ref):
        if rope:
            return proj(c0, 256) * cos_ref[...] + proj(c0s, 256) * sin_ref[...]
        return proj(c0, 256)

    naq[0] = (proj(C_NAQ, 256) * NA_DIM ** -0.5).astype(BF16)
    nak[0] = proj(C_NAK, 256).astype(BF16)
    nav[0] = proj(C_NAV, 256).astype(BF16)
    dfq[0] = (roped(C_DFQ, C_DFQ_S, cd_ref if rope else None, sd_ref if rope else None)
              * DIFF_DIM ** -0.5).astype(BF16)
    dfk[0] = roped(C_DFK, C_DFK_S, cd_ref if rope else None, sd_ref if rope else None).astype(BF16)
    dfv[0] = proj(C_DFV, 256).astype(BF16)
    rtq[0] = roped(C_RTQ, C_RTQ_S, cr_ref if rope else None, sr_ref if rope else None).astype(BF16)
    rtk[0] = (roped(C_RTK, C_RTK_S, cr_ref if rope else None, sr_ref if rope else None)
              * RET_QK ** -0.5).astype(BF16)
    rtv[0] = proj(C_RTV, 512).astype(BF16)
    rtg[0] = proj(C_RTG, 512)


def _inproj(x, mod3, mod_row, g1, w_ext, tables, tm):
    b, l, d = x.shape
    rope = tables is not None
    ncols = IN_COLS_EXT if rope else IN_COLS
    in_specs = [pl.BlockSpec((1, tm, d), lambda i, t: (i, t, 0)),
                pl.BlockSpec((1, 1, 6 * d), lambda i, t: (mod_row(i), 0, 0)),
                pl.BlockSpec((1, d), lambda i, t: (0, 0)),
                pl.BlockSpec((d, ncols), lambda i, t: (0, 0))]
    args = [x, mod3, g1, w_ext]
    if rope:
        in_specs += [pl.BlockSpec((tm, 256), lambda i, t: (t, 0))] * 4
        args += list(tables)
    widths = (256, 256, 256, 256, 256, 256, 256, 256, 512, 512)
    dtypes = (BF16,) * 9 + (F32,)
    out_specs = [pl.BlockSpec((1, tm, w), lambda i, t: (i, t, 0)) for w in widths]
    out_shape = [jax.ShapeDtypeStruct((b, l, w), dt) for w, dt in zip(widths, dtypes)]
    return pl.pallas_call(
        functools.partial(_inproj_kernel, rope=rope, d=d),
        grid=(b, l // tm),
        in_specs=in_specs, out_specs=out_specs, out_shape=out_shape,
        compiler_params=_cparams(("parallel", "parallel")),
        name="inproj_rope" if rope else "inproj_ctx",
    )(*args)


def _na_plan(rows, rb):
    wr = min(NA_WIN_ROWS, rows)
    kr = min(rb + wr - 1, rows)
    nblk = rows // rb
    win0 = np.clip(np.arange(rows) - wr // 2, 0, rows - wr)
    ks = np.clip(np.arange(nblk) * rb - wr // 2, 0, rows - kr)
    pats, pat_of = [], np.zeros(nblk, np.int32)
    for g in range(nblk):
        r = g * rb + np.arange(rb)
        krow = ks[g] + np.arange(kr)
        dr = krow[None, :] - r[:, None] + (NA_WIN_ROWS - 1)
        ok = (krow[None, :] >= win0[r][:, None]) & (krow[None, :] < win0[r][:, None] + wr)
        assert ok.sum(axis=1).min() == wr
        key = (np.where(ok, dr, 0).tobytes(), ok.tobytes())
        for p, (k2, _, _) in enumerate(pats):
            if k2 == key:
                pat_of[g] = p
                break
        else:
            pat_of[g] = len(pats)
            pats.append((key, np.where(ok, dr, 0), ok))
    dr_idx = np.stack([p[1] for p in pats])
    row_ok = np.stack([p[2] for p in pats])
    qcol = np.arange(GRID_W)
    kcol = np.arange(GRID_W)
    cs = np.clip(qcol - NA_WIN_COLS // 2, 0, GRID_W - NA_WIN_COLS)
    col_ok = (kcol[None, :] >= cs[:, None]) & (kcol[None, :] < cs[:, None] + NA_WIN_COLS)
    dc_idx = np.clip(kcol[None, :] - qcol[:, None], 1 - NA_WIN_COLS, NA_WIN_COLS - 1) + NA_WIN_COLS - 1
    return kr, ks.astype(np.int32), pat_of, dr_idx, row_ok, dc_idx, col_ok


def _na_bias(rpb, plan, rb):
    kr, _, _, dr_idx, row_ok, dc_idx, col_ok = plan
    p = dr_idx.shape[0]
    nr, nc = 2 * NA_WIN_ROWS - 1, 2 * NA_WIN_COLS - 1
    row_hot = ((np.arange(nr) == dr_idx[..., None]) & row_ok[..., None]).astype(np.float32)
    col_hot = ((np.arange(nc) == dc_idx[..., None]) & col_ok[..., None]).astype(np.float32)
    bias = jnp.einsum("hrc,pqwr,xyc->hpqxwy", rpb.astype(F32), row_hot, col_hot,
                      precision=lax.Precision.HIGHEST)
    ok = row_ok[:, :, None, :, None] & col_ok[None, None, :, None, :]
    bias = jnp.where(ok[None], bias, NEG_INF)
    return bias.reshape(NA_HEADS, p, rb * GRID_W, kr * GRID_W)


def _na_kernel(ks_ref, pat_ref, q_ref, kx_ref, vx_ref, kc_ref, vc_ref, bias_ref, o_ref, *, nkeys):
    g = pl.program_id(1)
    q = q_ref[0]
    k0 = pl.multiple_of(ks_ref[g] * GRID_W, GRID_W)
    kw = kx_ref[0, pl.ds(k0, nkeys), :]
    vw = vx_ref[0, pl.ds(k0, nkeys), :]
    kc = kc_ref[0]
    vc = vc_ref[0]
    pat = pat_ref[g]
    lane = lax.broadcasted_iota(jnp.int32, (1, NA_W), 1)
    out = jnp.zeros((q.shape[0], NA_W), F32)
    for h in range(NA_HEADS):
        hm = (lane >= NA_DIM * h) & (lane < NA_DIM * (h + 1))
        qh = jnp.where(hm, q, jnp.zeros_like(q))
        sw = _nt(qh, kw) + bias_ref[h, pat]
        sc = _nt(qh, kc)
        m = jnp.maximum(jnp.max(sw, axis=-1, keepdims=True), jnp.max(sc, axis=-1, keepdims=True))
        pw = jnp.exp(sw - m)
        pc = jnp.exp(sc - m)
        den = jnp.sum(pw, axis=-1, keepdims=True) + jnp.sum(pc, axis=-1, keepdims=True)
        o = _dot(pw.astype(BF16), vw) + _dot(pc.astype(BF16), vc)
        out = out + jnp.where(hm, o / den, 0.0)
    o_ref[0] = out.astype(o_ref.dtype)


def _na_latent(q, kx, vx, kc, vc, rpb, rb):
    b, l, _ = q.shape
    lc = kc.shape[1]
    rows = l // GRID_W
    plan = _na_plan(rows, rb)
    kr, ks, pat_of = plan[0], plan[1], plan[2]
    bias = _na_bias(rpb, plan, rb)
    tq, nkeys = rb * GRID_W, kr * GRID_W
    grid_spec = pltpu.PrefetchScalarGridSpec(
        num_scalar_prefetch=2,
        grid=(b, rows // rb),
        in_specs=[pl.BlockSpec((1, tq, NA_W), lambda i, g, *_: (i, g, 0)),
                  pl.BlockSpec((1, l, NA_W), lambda i, g, *_: (i, 0, 0)),
                  pl.BlockSpec((1, l, NA_W), lambda i, g, *_: (i, 0, 0)),
                  pl.BlockSpec((1, lc, NA_W), lambda i, g, *_: (i, 0, 0)),
                  pl.BlockSpec((1, lc, NA_W), lambda i, g, *_: (i, 0, 0)),
                  pl.BlockSpec(bias.shape, lambda i, g, *_: (0, 0, 0, 0))],
        out_specs=pl.BlockSpec((1, tq, NA_W), lambda i, g, *_: (i, g, 0)),
    )
    return pl.pallas_call(
        functools.partial(_na_kernel, nkeys=nkeys),
        grid_spec=grid_spec,
        out_shape=jax.ShapeDtypeStruct((b, l, NA_W), BF16),
        compiler_params=_cparams(("parallel", "arbitrary")),
        name="na_latent",
    )(jnp.asarray(ks), jnp.asarray(pat_of), q, kx, vx, kc, vc, bias)


def _na_ctx_kernel(q_ref, kc_ref, vc_ref, o_ref):
    q = q_ref[0]
    kc = kc_ref[0]
    vc = vc_ref[0]
    lane = lax.broadcasted_iota(jnp.int32, (1, NA_W), 1)
    out = jnp.zeros((q.shape[0], NA_W), F32)
    for h in range(NA_HEADS):
        hm = (lane >= NA_DIM * h) & (lane < NA_DIM * (h + 1))
        qh = jnp.where(hm, q, jnp.zeros_like(q))
        sc = _nt(qh, kc)
        pc = jnp.exp(sc - jnp.max(sc, axis=-1, keepdims=True))
        o = _dot(pc.astype(BF16), vc)
        out = out + jnp.where(hm, o / jnp.sum(pc, axis=-1, keepdims=True), 0.0)
    o_ref[0] = out.astype(o_ref.dtype)


def _na_ctx(q, kc, vc):
    b, lc, _ = q.shape
    spec = pl.BlockSpec((1, lc, NA_W), lambda i: (i, 0, 0))
    return pl.pallas_call(
        _na_ctx_kernel, grid=(b,), in_specs=[spec, spec, spec], out_specs=spec,
        out_shape=jax.ShapeDtypeStruct((b, lc, NA_W), BF16),
        compiler_params=_cparams(("parallel",)), name="na_ctx",
    )(q, kc, vc)


def _diff_kernel(*refs, lam_init, has_x):
    if has_x:
        q_ref, kc_ref, vc_ref, kx_ref, vx_ref, lam_ref, g_ref, o_ref = refs
    else:
        q_ref, kc_ref, vc_ref, lam_ref, g_ref, o_ref = refs
    q = q_ref[0]
    kc = kc_ref[0]
    vc = vc_ref[0]
    lp = lam_ref[...]
    lam = (jnp.exp(jnp.sum(lp[0:1] * lp[1:2], axis=-1, keepdims=True))
           - jnp.exp(jnp.sum(lp[2:3] * lp[3:4], axis=-1, keepdims=True)) + lam_init)
    lane = lax.broadcasted_iota(jnp.int32, (1, DIFF_W), 1)
    out = jnp.zeros((q.shape[0], DIFF_W), F32)
    def scores(h):
        res = []
        for c in range(2):
            lo = DIFF_VDIM * h + DIFF_DIM * c
            qm = jnp.where((lane >= lo) & (lane < lo + DIFF_DIM), q, jnp.zeros_like(q))
            res.append((_nt(qm, kc), _nt(qm, kx_ref[0]) if has_x else None))
        return res

    nxt = scores(0)
    for h in range(DIFF_HEADS):
        cur = nxt
        if h + 1 < DIFF_HEADS:
            nxt = scores(h + 1)
        parts = []
        for c in range(2):
            sc, sx = cur[c]
            m = jnp.max(sc, axis=-1, keepdims=True)
            if has_x:
                m = jnp.maximum(m, jnp.max(sx, axis=-1, keepdims=True))
            ec = jnp.exp(sc - m)
            den = jnp.sum(ec, axis=-1, keepdims=True)
            ex = None
            if has_x:
                ex = jnp.exp(sx - m)
                den = den + jnp.sum(ex, axis=-1, keepdims=True)
            parts.append((ec, ex, 1.0 / den))
        (ec0, ex0, r0), (ec1, ex1, r1) = parts
        r1 = lam * r1
        o = _dot((ec0 * r0 - ec1 * r1).astype(BF16), vc)
        if has_x:
            o = o + _dot((ex0 * r0 - ex1 * r1).astype(BF16), vx_ref[0])
        hm = (lane >= DIFF_VDIM * h) & (lane < DIFF_VDIM * (h + 1))
        oh = jnp.where(hm, o, 0.0)
        ms = jnp.sum(oh * oh, axis=-1, keepdims=True) * (1.0 / DIFF_VDIM)
        out = out + oh * lax.rsqrt(ms + NORM_EPS)
    o_ref[0] = (out * g_ref[...] * (1.0 - lam_init)).astype(o_ref.dtype)


def _diff(q, kc, vc, kx, vx, lam_p, g4, lam_init, tq):
    b, l, _ = q.shape
    lc = kc.shape[1]
    has_x = kx is not None
    cspec = pl.BlockSpec((1, lc, DIFF_W), lambda i, t: (i, 0, 0))
    in_specs = [pl.BlockSpec((1, tq, DIFF_W), lambda i, t: (i, t, 0)), cspec, cspec]
    args = [q, kc, vc]
    if has_x:
        xspec = pl.BlockSpec((1, kx.shape[1], DIFF_W), lambda i, t: (i, 0, 0))
        in_specs += [xspec, xspec]
        args += [kx, vx]
    in_specs += [pl.BlockSpec(lam_p.shape, lambda i, t: (0, 0)),
                 pl.BlockSpec((1, DIFF_W), lambda i, t: (0, 0))]
    args += [lam_p, g4]
    return pl.pallas_call(
        functools.partial(_diff_kernel, lam_init=lam_init, has_x=has_x),
        grid=(b, l // tq), in_specs=in_specs,
        out_specs=pl.BlockSpec((1, tq, DIFF_W), lambda i, t: (i, t, 0)),
        out_shape=jax.ShapeDtypeStruct((b, l, DIFF_W), BF16),
        compiler_params=_cparams(("parallel", "arbitrary")),
        name="diff_latent" if has_x else "diff_ctx",
    )(*args)


def _ret_kernel(dl_ref, cq_ref, ck_ref, cv_ref, cg_ref, xq_ref, xk_ref, xv_ref, xg_ref,
                yx_ref, yc_ref, s_ref, *, n_ctx, n_lat):
    h = pl.program_id(1)
    c = RET_CHUNK
    lane = lax.broadcasted_iota(jnp.int32, (1, RET_QK_W), 1)
    hm = (lane >= RET_QK * h) & (lane < RET_QK * (h + 1))
    row = lax.broadcasted_iota(jnp.int32, (c, c), 0).astype(F32)
    col = lax.broadcasted_iota(jnp.int32, (c, c), 1).astype(F32)
    roww = lax.broadcasted_iota(jnp.int32, (c, RET_QK_W), 0).astype(F32)

    def log_sigmoid(d):
        x = jnp.full((1, 1), dl_ref[d, h], F32)
        return jnp.minimum(x, 0.0) - jnp.log1p(jnp.exp(-jnp.abs(x)))

    lgf, lgb = log_sigmoid(0), log_sigmoid(1)
    diff = row - col
    fwd = (jnp.where(diff >= 0, jnp.exp(lgf * jnp.maximum(diff, 0.0)), 0.0),
           jnp.exp(lgf * (roww + 1.0)),
           jnp.exp(lgf * (c - 1.0 - roww)),
           jnp.exp(lgf * c))
    bwd = (jnp.where(diff <= 0, jnp.exp(lgb * jnp.maximum(-diff, 0.0)), 0.0),
           jnp.exp(lgb * (c - roww)),
           jnp.exp(lgb * roww),
           jnp.exp(lgb * c))

    def step(q_ref, k_ref, v_ref, g_ref, out_ref, ci, mats, sdir, second):
        dm, qd, kd, cd = mats
        sl = pl.ds(pl.multiple_of(ci * c, c), c)
        q = q_ref[0, sl, :]
        q = jnp.where(hm, q, jnp.zeros_like(q))
        k = k_ref[0, sl, :]
        v = v_ref[0, sl, :]
        att = _nt(q, k) * dm
        s = s_ref[sdir]
        o = _dot(att.astype(BF16), v) + _dot((q.astype(F32) * qd).astype(BF16), s.astype(BF16))
        s_ref[sdir] = cd * s + _tn((k.astype(F32) * kd).astype(BF16), v)
        if second:
            tot = out_ref[0, sl, :] + o
            out_ref[0, sl, :] = _rms(tot) * _silu(g_ref[0, sl, :])
        else:
            out_ref[0, sl, :] = o

    def both(refs, n, i, second):
        step(*refs, i, fwd, 0, second)
        step(*refs, n - 1 - i, bwd, 1, second)

    s_ref[...] = jnp.zeros_like(s_ref)
    crefs = (cq_ref, ck_ref, cv_ref, cg_ref, yc_ref)
    xrefs = (xq_ref, xk_ref, xv_ref, xg_ref, yx_ref)
    for i in range(n_ctx):
        both(crefs, n_ctx, i, i >= n_ctx // 2)
    for second in (False, True):
        def body(i, carry, second=second):
            both(xrefs, n_lat, i, second)
            return carry

        lax.fori_loop(n_lat // 2 if second else 0, n_lat if second else n_lat // 2, body, 0)


def _retention(decay_logit, cq, ck, cv, cg, xq, xk, xv, xg):
    b, l, _ = xq.shape
    lc = cq.shape[1]

    def qk_spec(n):
        return pl.BlockSpec((1, n, RET_QK_W), lambda i, h: (i, 0, 0))

    def v_spec(n):
        return pl.BlockSpec((1, n, RET_V), lambda i, h: (i, 0, h))

    return pl.pallas_call(
        functools.partial(_ret_kernel, n_ctx=lc // RET_CHUNK, n_lat=l // RET_CHUNK),
        grid=(b, RET_HEADS),
        in_specs=[pl.BlockSpec(memory_space=pltpu.SMEM),
                  qk_spec(lc), qk_spec(lc), v_spec(lc), v_spec(lc),
                  qk_spec(l), qk_spec(l), v_spec(l), v_spec(l)],
        out_specs=[v_spec(l), v_spec(lc)],
        out_shape=[jax.ShapeDtypeStruct((b, l, RET_W), F32), jax.ShapeDtypeStruct((b, lc, RET_W), F32)],
        scratch_shapes=[pltpu.VMEM((2, RET_QK_W, RET_V), F32)],
        compiler_params=_cparams(("parallel", "arbitrary")),
        name="retention",
    )(decay_logit.astype(F32), cq, ck, cv, cg, xq, xk, xv, xg)


def _outproj_kernel(x_ref, yna_ref, ydf_ref, yrt_ref, mod_ref, wo_ref, g2_ref, wq_ref, sk_ref,
                    xo_ref, h2_ref, st_ref, *, d):
    y = (_dot(yna_ref[0], wo_ref[0:NA_W, :])
         + _dot(ydf_ref[0], wo_ref[NA_W:NA_W + DIFF_W, :])
         + _dot(yrt_ref[0].astype(BF16), wo_ref[NA_W + DIFF_W:MIX_W, :]))
    mod = mod_ref[0]
    x = x_ref[0] + mod[:, 2 * d:3 * d] * y
    xo_ref[0] = x
    h2 = (_rms(x) * g2_ref[...] * (1.0 + mod[:, 4 * d:5 * d]) + mod[:, 3 * d:4 * d]).astype(BF16)
    h2_ref[0] = h2
    qp = _dot(h2, wq_ref[...]).astype(BF16)
    for hp in range(2 * PEER_HEADS):
        st_ref[0, hp] = _nt(sk_ref[hp], qp[:, hp * PEER_HALF:(hp + 1) * PEER_HALF])


def _outproj(x, yna, ydf, yrt, mod3, mod_row, wo, g2, wq, sk, tm):
    b, l, d = x.shape
    nhp = 2 * PEER_HEADS

    def tok(w):
        return pl.BlockSpec((1, tm, w), lambda i, t: (i, t, 0))

    return pl.pallas_call(
        functools.partial(_outproj_kernel, d=d),
        grid=(b, l // tm),
        in_specs=[tok(d), tok(NA_W), tok(DIFF_W), tok(RET_W),
                  pl.BlockSpec((1, 1, 6 * d), lambda i, t: (mod_row(i), 0, 0)),
                  pl.BlockSpec(wo.shape, lambda i, t: (0, 0)),
                  pl.BlockSpec((1, d), lambda i, t: (0, 0)),
                  pl.BlockSpec(wq.shape, lambda i, t: (0, 0)),
                  pl.BlockSpec(sk.shape, lambda i, t: (0, 0, 0))],
        out_specs=[tok(d), tok(d),
                   pl.BlockSpec((1, nhp, PEER_NKEYS, tm), lambda i, t: (i, 0, 0, t))],
        out_shape=[jax.ShapeDtypeStruct((b, l, d), F32), jax.ShapeDtypeStruct((b, l, d), BF16),
                   jax.ShapeDtypeStruct((b, nhp, PEER_NKEYS, l), F32)],
        compiler_params=_cparams(("parallel", "parallel")),
        name="outproj",
    )(x, yna, ydf, yrt, mod3, wo, g2, wq, sk)


def _topk_kernel(st_ref, th_ref, e1_ref, e2_ref, cand_ref):
    neg = -jnp.inf
    tl = st_ref.shape[-1]
    cand_ref[len(PEER_PAIRS):, :] = jnp.full((PEER_NCAND - len(PEER_PAIRS), tl), neg, F32)

    def top_rows(work):
        rows = []
        for k in range(PEER_TOPK):
            m = jnp.max(work, axis=0, keepdims=True)
            rows.append(m)
            if k + 1 < PEER_TOPK:
                work = jnp.where(work == m, neg, work)
        return rows

    for h in range(PEER_HEADS):
        s1 = st_ref[0, 2 * h]
        s2 = st_ref[0, 2 * h + 1]
        a1 = s1 - jnp.max(s1, axis=0, keepdims=True)
        a2 = s2 - jnp.max(s2, axis=0, keepdims=True)
        t1 = top_rows(a1)
        t2 = top_rows(a2)
        for r, (a, b) in enumerate(PEER_PAIRS):
            cand_ref[r:r + 1, :] = t1[a] + t2[b]
        tau = top_rows(cand_ref[...])[-1]
        et1 = [jnp.exp(r) for r in t1]
        et2 = [jnp.exp(r) for r in t2]
        tht = [jnp.exp(tau - r) for r in t1]
        z = jnp.zeros_like(tau)
        for a, b in PEER_PAIRS:
            z = z + jnp.where(et2[b] >= tht[a], et1[a] * et2[b], 0.0)
        th_ref[0, h] = jnp.exp(tau - a1)
        e1_ref[0, h] = jnp.exp(a1) * (0.5 / z)
        e2_ref[0, h] = jnp.exp(a2)


def _topk(st, tl):
    b, nhp, nk, l = st.shape
    hspec = pl.BlockSpec((1, PEER_HEADS, nk, tl), lambda i, t: (i, 0, 0, t))
    hshape = jax.ShapeDtypeStruct((b, PEER_HEADS, nk, l), F32)
    return pl.pallas_call(
        _topk_kernel,
        grid=(b, l // tl),
        in_specs=[pl.BlockSpec((1, nhp, nk, tl), lambda i, t: (i, 0, 0, t))],
        out_specs=[hspec, hspec, hspec],
        out_shape=[hshape, hshape, hshape],
        scratch_shapes=[pltpu.VMEM((PEER_NCAND, tl), F32)],
        compiler_params=_cparams(("parallel", "parallel")),
        name="peer_topk",
    )(st)


def _peer_kernel(*refs, final, d, ich):
    if final:
        h2_ref, u_ref, vt_ref, th_ref, e1_ref, e2_ref, x_ref, mod_ref, fg_ref, o_ref = refs[:10]
    else:
        h2_ref, u_ref, vt_ref, th_ref, e1_ref, e2_ref, x_ref, mod_ref, o_ref = refs[:9]
    acc_ref, at_ref, gw0_ref, gw1_ref, h2t_ref = refs[-5:]
    c = pl.program_id(2)
    nsteps = pl.num_programs(2)
    tm = h2_ref.shape[1]
    sub = 2 * PEER_NKEYS

    @pl.when(c == 0)
    def _():
        acc_ref[...] = jnp.zeros_like(acc_ref)
        gw1_ref[...] = jnp.zeros_like(gw1_ref)
        h2t_ref[...] = h2_ref[0].T

    def gate_weights(gw_ref):
        nt8 = PEER_NKEYS // 8
        for k in range(ich // 2):
            at_ref[k] = _dot(u_ref[k * sub:(k + 1) * sub, :], h2t_ref[...])
        for k in range(ich // 2):
            for ii in range(2 * k, 2 * k + 2):
                for lg in range(tm // 128):
                    ls = slice(lg * 128, (lg + 1) * 128)
                    w = [None] * nt8
                    for h in range(PEER_HEADS):
                        th = jnp.broadcast_to(th_ref[0, h, ii:ii + 1, ls], (8, 128))
                        e1 = jnp.broadcast_to(e1_ref[0, h, ii:ii + 1, ls], (8, 128))
                        for jt in range(nt8):
                            e2 = e2_ref[0, h, jt * 8:(jt + 1) * 8, ls]
                            term = jnp.where(e2 >= th, e2, 0.0) * e1
                            w[jt] = term if w[jt] is None else w[jt] + term
                    ra = (ii - 2 * k) * PEER_NKEYS
                    for jt in range(0, nt8, 2):
                        a = at_ref[k, ra + jt * 8:ra + jt * 8 + 16, ls]
                        g = a * (1.0 + lax.erf(a * math.sqrt(0.5))) * jnp.concatenate(w[jt:jt + 2], axis=0)
                        rg = ii * PEER_NKEYS + jt * 8
                        gw_ref[rg:rg + 16, ls] = g.astype(BF16)

    def apply_v(gw_ref):
        acc_ref[...] += _dot(vt_ref[...], gw_ref[...])

    @pl.when((c < nsteps - 1) & (c % 2 == 0))
    def _():
        apply_v(gw1_ref)
        gate_weights(gw0_ref)

    @pl.when((c < nsteps - 1) & (c % 2 == 1))
    def _():
        apply_v(gw0_ref)
        gate_weights(gw1_ref)

    @pl.when(c == nsteps - 1)
    def _():
        apply_v(gw1_ref)
        mod = mod_ref[0]
        xo = x_ref[0] + mod[:, 5 * d:6 * d] * acc_ref[...].T
        if final:
            xo = _rms(xo) * fg_ref[...]
        o_ref[0] = xo


def _peer(h2, u, vt, th, e1, e2, x, mod3, mod_row, final_g, tm, ich=8):
    b, l, d = x.shape
    ne = u.shape[0]
    ec = ich * PEER_NKEYS
    nch = ne // ec
    assert nch % 2 == 0 and ich % 2 == 0
    final = final_g is not None
    ispec = pl.BlockSpec((1, PEER_HEADS, ich, tm), lambda i, t, c: (i, 0, jnp.minimum(c, nch - 1), t))
    in_specs = [pl.BlockSpec((1, tm, d), lambda i, t, c: (i, t, 0)),
                pl.BlockSpec((ec, d), lambda i, t, c: (jnp.minimum(c, nch - 1), 0)),
                pl.BlockSpec((d, ec), lambda i, t, c: (0, jnp.maximum(c - 1, 0))),
                ispec, ispec,
                pl.BlockSpec((1, PEER_HEADS, PEER_NKEYS, tm), lambda i, t, c: (i, 0, 0, t)),
                pl.BlockSpec((1, tm, d), lambda i, t, c: (i, t, 0)),
                pl.BlockSpec((1, 1, 6 * d), lambda i, t, c: (mod_row(i), 0, 0))]
    args = [h2, u, vt, th, e1, e2, x, mod3]
    if final:
        in_specs.append(pl.BlockSpec((1, d), lambda i, t, c: (0, 0)))
        args.append(final_g)
    return pl.pallas_call(
        functools.partial(_peer_kernel, final=final, d=d, ich=ich),
        grid=(b, l // tm, nch + 1),
        in_specs=in_specs,
        out_specs=pl.BlockSpec((1, tm, d), lambda i, t, c: (i, t, 0)),
        out_shape=jax.ShapeDtypeStruct((b, l, d), F32),
        scratch_shapes=[pltpu.VMEM((d, tm), F32), pltpu.VMEM((ich // 2, 2 * PEER_NKEYS, tm), F32),
                        pltpu.VMEM((ec, tm), BF16), pltpu.VMEM((ec, tm), BF16), pltpu.VMEM((d, tm), BF16)],
        compiler_params=_cparams(("parallel", "parallel", "arbitrary")),
        name="peer_final" if final else "peer",
    )(*args)


def _rope_tables(l, dim, width):
    t = jnp.arange(l)
    rows = (t // GRID_W).astype(F32)
    cols = (t % GRID_W).astype(F32)
    half = dim // 2
    inv = jnp.power(ROPE_BASE, -jnp.arange(0, half, 2, dtype=F32) / half)
    ang = jnp.concatenate([rows[:, None] * inv, cols[:, None] * inv], axis=-1)
    pair = (np.arange(width) % dim) // 2
    sign = np.where(np.arange(width) % 2 == 0, -1.0, 1.0).astype(np.float32)
    return jnp.cos(ang)[:, pair], jnp.sin(ang)[:, pair] * sign


def _swap_pairs(w):
    d, n = w.shape
    return w.reshape(d, n // 2, 2)[:, :, ::-1].reshape(d, n)


def _block_mixer_and_peer(x, ctx, mod3, layer, need_ctx, p, tables, final_g):
    b, l, d = x.shape
    lc = ctx.shape[1]
    row_x = lambda i: i
    row_c = lambda i: b
    lam_init = 0.8 - 0.6 * math.exp(-0.3 * layer)

    lat = _inproj(x, mod3, row_x, p["g1"], p["w_in"], tables, 256)
    cx = _inproj(ctx, mod3, row_c, p["g1"], p["w_in"], None, lc)
    naq, nak, nav, dfq, dfk, dfv, rtq, rtk, rtv, rtg = lat
    cnaq, cnak, cnav, cdfq, cdfk, cdfv, crtq, crtk, crtv, crtg = cx

    y_na = _na_latent(naq, nak, nav, cnak, cnav, p["rpb"], 4)
    y_df = _diff(dfq, cdfk, cdfv, dfk, dfv, p["lam"], p["subln"], lam_init, 256)
    y_rt, yc_rt = _retention(p["decay"], crtq, crtk, crtv, crtg, rtq, rtk, rtv, rtg)

    def channel_mix(xx, yna, ydf, yrt, row, tm, fg):
        xo, h2, st = _outproj(xx, yna, ydf, yrt, mod3, row, p["w_out"], p["g2"], p["wq"], p["sk"], tm)
        th, e1, e2 = _topk(st, min(tm, 256))
        return _peer(h2, p["u"], p["vt"], th, e1, e2, xo, mod3, row, fg, tm)

    x = channel_mix(x, y_na, y_df, y_rt, row_x, min(l, 512), final_g)
    if need_ctx:
        yc_na = _na_ctx(cnaq, cnak, cnav)
        yc_df = _diff(cdfq, cdfk, cdfv, None, None, p["lam"], p["subln"], lam_init, lc)
        ctx = channel_mix(ctx, yc_na, yc_df, yc_rt, row_c, lc, None)
    return x, ctx


def kernel(x, c, ctx, c_ctx, w_ada, b_ada, norm1_g, w_in, na_rpb, diff_lambda, diff_subln_g,
           ret_decay_logit, w_out, norm2_g, peer_wq, peer_subkeys, peer_u, peer_v, final_g):
    b, l, d = x.shape
    depth = w_ada.shape[0]
    x = x.astype(F32)
    ctx = ctx.astype(F32)
    npad = -(-(b + 1) // 8) * 8
    cpad = jnp.zeros((npad, d), F32).at[:b].set(c.astype(F32)).at[b].set(c_ctx.astype(F32))
    mod = _ada(cpad, w_ada.astype(F32), b_ada.astype(F32))
    tables = _rope_tables(l, DIFF_DIM, DIFF_QK_W) + _rope_tables(l, RET_QK, RET_QK_W)
    for layer in range(depth):
        wi = w_in[layer]
        w_ext = jnp.concatenate(
            [wi] + [_swap_pairs(wi[:, c0:c0 + 256]) for c0 in (C_DFQ, C_DFK, C_RTQ, C_RTK)], axis=1)
        p = {
            "g1": norm1_g[layer].astype(F32).reshape(1, d),
            "g2": norm2_g[layer].astype(F32).reshape(1, d),
            "w_in": w_ext.astype(BF16),
            "rpb": na_rpb[layer],
            "lam": diff_lambda[layer].astype(F32),
            "subln": jnp.tile(diff_subln_g[layer].astype(F32), DIFF_HEADS).reshape(1, DIFF_W),
            "decay": ret_decay_logit[layer],
            "w_out": w_out[layer].astype(BF16),
            "wq": peer_wq[layer].astype(BF16),
            "sk": peer_subkeys[layer].astype(BF16).reshape(2 * PEER_HEADS, PEER_NKEYS, PEER_HALF),
            "u": peer_u[layer].astype(BF16),
            "vt": peer_v[layer].astype(BF16).T,
        }
        last = layer == depth - 1
        x, ctx = _block_mixer_and_peer(x, ctx, mod[layer].reshape(npad, 1, 6 * d), layer, not last, p,
                                       tables, final_g.astype(F32).reshape(1, d) if last else None)
    return x
```

```python
import functools
import math

import numpy as np
import jax
import jax.numpy as jnp
from jax import lax
from jax.experimental import pallas as pl
from jax.experimental.pallas import tpu as pltpu

F32 = jnp.float32
BF16 = jnp.bfloat16

GRID_W = 64
NORM_EPS = 1e-6
ROPE_BASE = 10000.0
NEG_INF = -1e30

NA_HEADS = 4
NA_DIM = 64
NA_WIN_ROWS = 8
NA_WIN_COLS = 16
DIFF_HEADS = 4
DIFF_DIM = 32
DIFF_VDIM = 64
RET_HEADS = 4
RET_QK = 64
RET_V = 128
RET_CHUNK = 128

NA_W = NA_HEADS * NA_DIM
DIFF_QK_W = DIFF_HEADS * 2 * DIFF_DIM
DIFF_W = DIFF_HEADS * DIFF_VDIM
RET_QK_W = RET_HEADS * RET_QK
RET_W = RET_HEADS * RET_V
MIX_W = NA_W + DIFF_W + RET_W
IN_COLS = 3 * NA_W + 2 * DIFF_QK_W + DIFF_W + 2 * RET_QK_W + 2 * RET_W

PEER_HEADS = 8
PEER_NKEYS = 128
PEER_KDIM = 256
PEER_TOPK = 16
PEER_HALF = PEER_KDIM // 2

C_NAQ, C_NAK, C_NAV = 0, 256, 512
C_DFQ, C_DFK, C_DFV = 768, 1024, 1280
C_RTQ, C_RTK, C_RTV, C_RTG = 1536, 1792, 2048, 2560
C_DFQ_S, C_DFK_S, C_RTQ_S, C_RTK_S = 3072, 3328, 3584, 3840
IN_COLS_EXT = 4096

PEER_PAIRS = tuple((a, b) for a in range(PEER_TOPK) for b in range(PEER_TOPK // (a + 1)))
PEER_NCAND = 64

VMEM_LIMIT = 56 * 1024 * 1024


def _cparams(sem):
    return pltpu.CompilerParams(dimension_semantics=sem, vmem_limit_bytes=VMEM_LIMIT)


def _nt(a, b):
    return lax.dot_general(a, b, (((1,), (1,)), ((), ())), preferred_element_type=F32)


def _tn(a, b):
    return lax.dot_general(a, b, (((0,), (0,)), ((), ())), preferred_element_type=F32)


def _dot(a, b):
    return jnp.dot(a, b, preferred_element_type=F32)


def _rms(x):
    return x * lax.rsqrt(jnp.mean(x * x, axis=-1, keepdims=True) + NORM_EPS)


def _silu(x):
    return x * jax.nn.sigmoid(x)


def _ada_kernel(c_ref, w_ref, b_ref, o_ref):
    s = _silu(c_ref[...])
    o_ref[0] = jnp.dot(s, w_ref[0], preferred_element_type=F32,
                       precision=lax.Precision.HIGHEST) + b_ref[0]


def _ada(cpad, w_ada, b_ada):
    depth, d, n = w_ada.shape
    tn = 1536
    return pl.pallas_call(
        _ada_kernel,
        grid=(depth, n // tn),
        in_specs=[pl.BlockSpec((cpad.shape[0], d), lambda l, j: (0, 0)),
                  pl.BlockSpec((1, d, tn), lambda l, j: (l, 0, j)),
                  pl.BlockSpec((1, 1, tn), lambda l, j: (l, 0, j))],
        out_specs=pl.BlockSpec((1, cpad.shape[0], tn), lambda l, j: (l, 0, j)),
        out_shape=jax.ShapeDtypeStruct((depth, cpad.shape[0], n), F32),
        compiler_params=_cparams(("parallel", "parallel")),
        name="ada",
    )(cpad, w_ada, b_ada.reshape(depth, 1, n))


def _inproj_kernel(*refs, rope, d):
    if rope:
        (x_ref, mod_ref, g_ref, w_ref, cd_ref, sd_ref, cr_ref, sr_ref,
         naq, nak, nav, dfq, dfk, dfv, rtq, rtk, rtv, rtg) = refs
    else:
        (x_ref, mod_ref, g_ref, w_ref,
         naq, nak, nav, dfq, dfk, dfv, rtq, rtk, rtv, rtg) = refs
    mod = mod_ref[0]
    h = _rms(x_ref[0]) * g_ref[...] * (1.0 + mod[:, d:2 * d]) + mod[:, 0:d]
    hb = h.astype(BF16)

    def proj(c0, n):
        return _dot(hb, w_ref[:, c0:c0 + n])

    def roped(c0, c0s, cos_ref, sin_ref):
        if rope:
            return proj(c0, 256) * cos_ref[...] + proj(c0s, 256) * sin_ref[...]
        return proj(c0, 256)

    naq[0] = (proj(C_NAQ, 256) * NA_DIM ** -0.5).astype(BF16)
    nak[0] = proj(C_NAK, 256).astype(BF16)
    nav[0] = proj(C_NAV, 256).astype(BF16)
    dfq[0] = (roped(C_DFQ, C_DFQ_S, cd_ref if rope else None, sd_ref if rope else None)
              * DIFF_DIM ** -0.5).astype(BF16)
    dfk[0] = roped(C_DFK, C_DFK_S, cd_ref if rope else None, sd_ref if rope else None).astype(BF16)
    dfv[0] = proj(C_DFV, 256).astype(BF16)
    rtq[0] = roped(C_RTQ, C_RTQ_S, cr_ref if rope else None, sr_ref if rope else None).astype(BF16)
    rtk[0] = (roped(C_RTK, C_RTK_S, cr_ref if rope else None, sr_ref if rope else None)
              * RET_QK ** -0.5).astype(BF16)
    rtv[0] = proj(C_RTV, 512).astype(BF16)
    rtg[0] = proj(C_RTG, 512)


def _inproj(x, mod3, mod_row, g1, w_ext, tables, tm):
    b, l, d = x.shape
    rope = tables is not None
    ncols = IN_COLS_EXT if rope else IN_COLS
    in_specs = [pl.BlockSpec((1, tm, d), lambda i, t: (i, t, 0)),
                pl.BlockSpec((1, 1, 6 * d), lambda i, t: (mod_row(i), 0, 0)),
                pl.BlockSpec((1, d), lambda i, t: (0, 0)),
                pl.BlockSpec((d, ncols), lambda i, t: (0, 0))]
    args = [x, mod3, g1, w_ext]
    if rope:
        in_specs += [pl.BlockSpec((tm, 256), lambda i, t: (t, 0))] * 4
        args += list(tables)
    widths = (256, 256, 256, 256, 256, 256, 256, 256, 512, 512)
    dtypes = (BF16,) * 9 + (F32,)
    out_specs = [pl.BlockSpec((1, tm, w), lambda i, t: (i, t, 0)) for w in widths]
    out_shape = [jax.ShapeDtypeStruct((b, l, w), dt) for w, dt in zip(widths, dtypes)]
    return pl.pallas_call(
        functools.partial(_inproj_kernel, rope=rope, d=d),
        grid=(b, l // tm),
        in_specs=in_specs, out_specs=out_specs, out_shape=out_shape,
        compiler_params=_cparams(("parallel", "parallel")),
        name="inproj_rope" if rope else "inproj_ctx",
    )(*args)


def _na_plan(rows, rb):
    wr = min(NA_WIN_ROWS, rows)
    kr = min(rb + wr - 1, rows)
    nblk = rows // rb
    win0 = np.clip(np.arange(rows) - wr // 2, 0, rows - wr)
    ks = np.clip(np.arange(nblk) * rb - wr // 2, 0, rows - kr)
    pats, pat_of = [], np.zeros(nblk, np.int32)
    for g in range(nblk):
        r = g * rb + np.arange(rb)
        krow = ks[g] + np.arange(kr)
        dr = krow[None, :] - r[:, None] + (NA_WIN_ROWS - 1)
        ok = (krow[None, :] >= win0[r][:, None]) & (krow[None, :] < win0[r][:, None] + wr)
        assert ok.sum(axis=1).min() == wr
        key = (np.where(ok, dr, 0).tobytes(), ok.tobytes())
        for p, (k2, _, _) in enumerate(pats):
            if k2 == key:
                pat_of[g] = p
                break
        else:
            pat_of[g] = len(pats)
            pats.append((key, np.where(ok, dr, 0), ok))
    dr_idx = np.stack([p[1] for p in pats])
    row_ok = np.stack([p[2] for p in pats])
    qcol = np.arange(GRID_W)
    kcol = np.arange(GRID_W)
    cs = np.clip(qcol - NA_WIN_COLS // 2, 0, GRID_W - NA_WIN_COLS)
    col_ok = (kcol[None, :] >= cs[:, None]) & (kcol[None, :] < cs[:, None] + NA_WIN_COLS)
    dc_idx = np.clip(kcol[None, :] - qcol[:, None], 1 - NA_WIN_COLS, NA_WIN_COLS - 1) + NA_WIN_COLS - 1
    return kr, ks.astype(np.int32), pat_of, dr_idx, row_ok, dc_idx, col_ok


def _na_bias(rpb, plan, rb):
    kr, _, _, dr_idx, row_ok, dc_idx, col_ok = plan
    p = dr_idx.shape[0]
    nr, nc = 2 * NA_WIN_ROWS - 1, 2 * NA_WIN_COLS - 1
    row_hot = ((np.arange(nr) == dr_idx[..., None]) & row_ok[..., None]).astype(np.float32)
    col_hot = ((np.arange(nc) == dc_idx[..., None]) & col_ok[..., None]).astype(np.float32)
    bias = jnp.einsum("hrc,pqwr,xyc->hpqxwy", rpb.astype(F32), row_hot, col_hot,
                      precision=lax.Precision.HIGHEST)
    ok = row_ok[:, :, None, :, None] & col_ok[None, None, :, None, :]
    bias = jnp.where(ok[None], bias, NEG_INF)
    return bias.reshape(NA_HEADS, p, rb * GRID_W, kr * GRID_W)


def _na_kernel(ks_ref, pat_ref, q_ref, kx_ref, vx_ref, kc_ref, vc_ref, bias_ref, o_ref, *, nkeys):
    g = pl.program_id(1)
    q = q_ref[0]
    k0 = pl.multiple_of(ks_ref[g] * GRID_W, GRID_W)
    kw = kx_ref[0, pl.ds(k0, nkeys), :]
    vw = vx_ref[0, pl.ds(k0, nkeys), :]
    kc = kc_ref[0]
    vc = vc_ref[0]
    pat = pat_ref[g]
    lane = lax.broadcasted_iota(jnp.int32, (1, NA_W), 1)
    out = jnp.zeros((q.shape[0], NA_W), F32)
    for h in range(NA_HEADS):
        hm = (lane >= NA_DIM * h) & (lane < NA_DIM * (h + 1))
        qh = jnp.where(hm, q, jnp.zeros_like(q))
        sw = _nt(qh, kw) + bias_ref[h, pat]
        sc = _nt(qh, kc)
        m = jnp.maximum(jnp.max(sw, axis=-1, keepdims=True), jnp.max(sc, axis=-1, keepdims=True))
        pw = jnp.exp(sw - m)
        pc = jnp.exp(sc - m)
        den = jnp.sum(pw, axis=-1, keepdims=True) + jnp.sum(pc, axis=-1, keepdims=True)
        o = _dot(pw.astype(BF16), vw) + _dot(pc.astype(BF16), vc)
        out = out + jnp.where(hm, o / den, 0.0)
    o_ref[0] = out.astype(o_ref.dtype)


def _na_latent(q, kx, vx, kc, vc, rpb, rb):
    b, l, _ = q.shape
    lc = kc.shape[1]
    rows = l // GRID_W
    plan = _na_plan(rows, rb)
    kr, ks, pat_of = plan[0], plan[1], plan[2]
    bias = _na_bias(rpb, plan, rb)
    tq, nkeys = rb * GRID_W, kr * GRID_W
    grid_spec = pltpu.PrefetchScalarGridSpec(
        num_scalar_prefetch=2,
        grid=(b, rows // rb),
        in_specs=[pl.BlockSpec((1, tq, NA_W), lambda i, g, *_: (i, g, 0)),
                  pl.BlockSpec((1, l, NA_W), lambda i, g, *_: (i, 0, 0)),
                  pl.BlockSpec((1, l, NA_W), lambda i, g, *_: (i, 0, 0)),
                  pl.BlockSpec((1, lc, NA_W), lambda i, g, *_: (i, 0, 0)),
                  pl.BlockSpec((1, lc, NA_W), lambda i, g, *_: (i, 0, 0)),
                  pl.BlockSpec(bias.shape, lambda i, g, *_: (0, 0, 0, 0))],
        out_specs=pl.BlockSpec((1, tq, NA_W), lambda i, g, *_: (i, g, 0)),
    )
    return pl.pallas_call(
        functools.partial(_na_kernel, nkeys=nkeys),
        grid_spec=grid_spec,
        out_shape=jax.ShapeDtypeStruct((b, l, NA_W), BF16),
        compiler_params=_cparams(("parallel", "arbitrary")),
        name="na_latent",
    )(jnp.asarray(ks), jnp.asarray(pat_of), q, kx, vx, kc, vc, bias)


def _na_ctx_kernel(q_ref, kc_ref, vc_ref, o_ref):
    q = q_ref[0]
    kc = kc_ref[0]
    vc = vc_ref[0]
    lane = lax.broadcasted_iota(jnp.int32, (1, NA_W), 1)
    out = jnp.zeros((q.shape[0], NA_W), F32)
    for h in range(NA_HEADS):
        hm = (lane >= NA_DIM * h) & (lane < NA_DIM * (h + 1))
        qh = jnp.where(hm, q, jnp.zeros_like(q))
        sc = _nt(qh, kc)
        pc = jnp.exp(sc - jnp.max(sc, axis=-1, keepdims=True))
        o = _dot(pc.astype(BF16), vc)
        out = out + jnp.where(hm, o / jnp.sum(pc, axis=-1, keepdims=True), 0.0)
    o_ref[0] = out.astype(o_ref.dtype)


def _na_ctx(q, kc, vc):
    b, lc, _ = q.shape
    spec = pl.BlockSpec((1, lc, NA_W), lambda i: (i, 0, 0))
    return pl.pallas_call(
        _na_ctx_kernel, grid=(b,), in_specs=[spec, spec, spec], out_specs=spec,
        out_shape=jax.ShapeDtypeStruct((b, lc, NA_W), BF16),
        compiler_params=_cparams(("parallel",)), name="na_ctx",
    )(q, kc, vc)


def _diff_kernel(*refs, lam_init, has_x):
    if has_x:
        q_ref, kc_ref, vc_ref, kx_ref, vx_ref, lam_ref, g_ref, o_ref = refs
    else:
        q_ref, kc_ref, vc_ref, lam_ref, g_ref, o_ref = refs
    q = q_ref[0]
    kc = kc_ref[0]
    vc = vc_ref[0]
    lp = lam_ref[...]
    lam = (jnp.exp(jnp.sum(lp[0:1] * lp[1:2], axis=-1, keepdims=True))
           - jnp.exp(jnp.sum(lp[2:3] * lp[3:4], axis=-1, keepdims=True)) + lam_init)
    lane = lax.broadcasted_iota(jnp.int32, (1, DIFF_W), 1)
    out = jnp.zeros((q.shape[0], DIFF_W), F32)
    def scores(h):
        res = []
        for c in range(2):
            lo = DIFF_VDIM * h + DIFF_DIM * c
            qm = jnp.where((lane >= lo) & (lane < lo + DIFF_DIM), q, jnp.zeros_like(q))
            res.append((_nt(qm, kc), _nt(qm, kx_ref[0]) if has_x else None))
        return res

    nxt = scores(0)
    for h in range(DIFF_HEADS):
        cur = nxt
        if h + 1 < DIFF_HEADS:
            nxt = scores(h + 1)
        parts = []
        for c in range(2):
            sc, sx = cur[c]
            m = jnp.max(sc, axis=-1, keepdims=True)
            if has_x:
                m = jnp.maximum(m, jnp.max(sx, axis=-1, keepdims=True))
            ec = jnp.exp(sc - m)
            den = jnp.sum(ec, axis=-1, keepdims=True)
            ex = None
            if has_x:
                ex = jnp.exp(sx - m)
                den = den + jnp.sum(ex, axis=-1, keepdims=True)
            parts.append((ec, ex, 1.0 / den))
        (ec0, ex0, r0), (ec1, ex1, r1) = parts
        r1 = lam * r1
        o = _dot((ec0 * r0 - ec1 * r1).astype(BF16), vc)
        if has_x:
            o = o + _dot((ex0 * r0 - ex1 * r1).astype(BF16), vx_ref[0])
        hm = (lane >= DIFF_VDIM * h) & (lane < DIFF_VDIM * (h + 1))
        oh = jnp.where(hm, o, 0.0)
        ms = jnp.sum(oh * oh, axis=-1, keepdims=True) * (1.0 / DIFF_VDIM)
        out = out + oh * lax.rsqrt(ms + NORM_EPS)
    o_ref[0] = (out * g_ref[...] * (1.0 - lam_init)).astype(o_ref.dtype)


def _diff(q, kc, vc, kx, vx, lam_p, g4, lam_init, tq):
    b, l, _ = q.shape
    lc = kc.shape[1]
    has_x = kx is not None
    cspec = pl.BlockSpec((1, lc, DIFF_W), lambda i, t: (i, 0, 0))
    in_specs = [pl.BlockSpec((1, tq, DIFF_W), lambda i, t: (i, t, 0)), cspec, cspec]
    args = [q, kc, vc]
    if has_x:
        xspec = pl.BlockSpec((1, kx.shape[1], DIFF_W), lambda i, t: (i, 0, 0))
        in_specs += [xspec, xspec]
        args += [kx, vx]
    in_specs += [pl.BlockSpec(lam_p.shape, lambda i, t: (0, 0)),
                 pl.BlockSpec((1, DIFF_W), lambda i, t: (0, 0))]
    args += [lam_p, g4]
    return pl.pallas_call(
        functools.partial(_diff_kernel, lam_init=lam_init, has_x=has_x),
        grid=(b, l // tq), in_specs=in_specs,
        out_specs=pl.BlockSpec((1, tq, DIFF_W), lambda i, t: (i, t, 0)),
        out_shape=jax.ShapeDtypeStruct((b, l, DIFF_W), BF16),
        compiler_params=_cparams(("parallel", "arbitrary")),
        name="diff_latent" if has_x else "diff_ctx",
    )(*args)


def _ret_kernel(dl_ref, cq_ref, ck_ref, cv_ref, cg_ref, xq_ref, xk_ref, xv_ref, xg_ref,
                yx_ref, yc_ref, s_ref, *, n_ctx, n_lat):
    h = pl.program_id(1)
    c = RET_CHUNK
    lane = lax.broadcasted_iota(jnp.int32, (1, RET_QK_W), 1)
    hm = (lane >= RET_QK * h) & (lane < RET_QK * (h + 1))
    row = lax.broadcasted_iota(jnp.int32, (c, c), 0).astype(F32)
    col = lax.broadcasted_iota(jnp.int32, (c, c), 1).astype(F32)
    roww = lax.broadcasted_iota(jnp.int32, (c, RET_QK_W), 0).astype(F32)

    def log_sigmoid(d):
        x = jnp.full((1, 1), dl_ref[d, h], F32)
        return jnp.minimum(x, 0.0) - jnp.log1p(jnp.exp(-jnp.abs(x)))

    lgf, lgb = log_sigmoid(0), log_sigmoid(1)
    diff = row - col
    fwd = (jnp.where(diff >= 0, jnp.exp(lgf * jnp.maximum(diff, 0.0)), 0.0),
           jnp.exp(lgf * (roww + 1.0)),
           jnp.exp(lgf * (c - 1.0 - roww)),
           jnp.exp(lgf * c))
    bwd = (jnp.where(diff <= 0, jnp.exp(lgb * jnp.maximum(-diff, 0.0)), 0.0),
           jnp.exp(lgb * (c - roww)),
           jnp.exp(lgb * roww),
           jnp.exp(lgb * c))

    def step(q_ref, k_ref, v_ref, g_ref, out_ref, ci, mats, sdir, second):
        dm, qd, kd, cd = mats
        sl = pl.ds(pl.multiple_of(ci * c, c), c)
        q = q_ref[0, sl, :]
        q = jnp.where(hm, q, jnp.zeros_like(q))
        k = k_ref[0, sl, :]
        v = v_ref[0, sl, :]
        att = _nt(q, k) * dm
        s = s_ref[sdir]
        o = _dot(att.astype(BF16), v) + _dot((q.astype(F32) * qd).astype(BF16), s.astype(BF16))
        s_ref[sdir] = cd * s + _tn((k.astype(F32) * kd).astype(BF16), v)
        if second:
            tot = out_ref[0, sl, :] + o
            out_ref[0, sl, :] = _rms(tot) * _silu(g_ref[0, sl, :])
        else:
            out_ref[0, sl, :] = o

    def both(refs, n, i, second):
        step(*refs, i, fwd, 0, second)
        step(*refs, n - 1 - i, bwd, 1, second)

    s_ref[...] = jnp.zeros_like(s_ref)
    crefs = (cq_ref, ck_ref, cv_ref, cg_ref, yc_ref)
    xrefs = (xq_ref, xk_ref, xv_ref, xg_ref, yx_ref)
    for i in range(n_ctx):
        both(crefs, n_ctx, i, i >= n_ctx // 2)
    for second in (False, True):
        def body(i, carry, second=second):
            both(xrefs, n_lat, i, second)
            return carry

        lax.fori_loop(n_lat // 2 if second else 0, n_lat if second else n_lat // 2, body, 0)


def _retention(decay_logit, cq, ck, cv, cg, xq, xk, xv, xg):
    b, l, _ = xq.shape
    lc = cq.shape[1]

    def qk_spec(n):
        return pl.BlockSpec((1, n, RET_QK_W), lambda i, h: (i, 0, 0))

    def v_spec(n):
        return pl.BlockSpec((1, n, RET_V), lambda i, h: (i, 0, h))

    return pl.pallas_call(
        functools.partial(_ret_kernel, n_ctx=lc // RET_CHUNK, n_lat=l // RET_CHUNK),
        grid=(b, RET_HEADS),
        in_specs=[pl.BlockSpec(memory_space=pltpu.SMEM),
                  qk_spec(lc), qk_spec(lc), v_spec(lc), v_spec(lc),
                  qk_spec(l), qk_spec(l), v_spec(l), v_spec(l)],
        out_specs=[v_spec(l), v_spec(lc)],
        out_shape=[jax.ShapeDtypeStruct((b, l, RET_W), F32), jax.ShapeDtypeStruct((b, lc, RET_W), F32)],
        scratch_shapes=[pltpu.VMEM((2, RET_QK_W, RET_V), F32)],
        compiler_params=_cparams(("parallel", "arbitrary")),
        name="retention",
    )(decay_logit.astype(F32), cq, ck, cv, cg, xq, xk, xv, xg)


def _outproj_kernel(x_ref, yna_ref, ydf_ref, yrt_ref, mod_ref, wo_ref, g2_ref, wq_ref, sk_ref,
                    xo_ref, h2_ref, st_ref, *, d):
    y = (_dot(yna_ref[0], wo_ref[0:NA_W, :])
         + _dot(ydf_ref[0], wo_ref[NA_W:NA_W + DIFF_W, :])
         + _dot(yrt_ref[0].astype(BF16), wo_ref[NA_W + DIFF_W:MIX_W, :]))
    mod = mod_ref[0]
    x = x_ref[0] + mod[:, 2 * d:3 * d] * y
    xo_ref[0] = x
    h2 = (_rms(x) * g2_ref[...] * (1.0 + mod[:, 4 * d:5 * d]) + mod[:, 3 * d:4 * d]).astype(BF16)
    h2_ref[0] = h2
    qp = _dot(h2, wq_ref[...]).astype(BF16)
    for hp in range(2 * PEER_HEADS):
        st_ref[0, hp] = _nt(sk_ref[hp], qp[:, hp * PEER_HALF:(hp + 1) * PEER_HALF])


def _outproj(x, yna, ydf, yrt, mod3, mod_row, wo, g2, wq, sk, tm):
    b, l, d = x.shape
    nhp = 2 * PEER_HEADS

    def tok(w):
        return pl.BlockSpec((1, tm, w), lambda i, t: (i, t, 0))

    return pl.pallas_call(
        functools.partial(_outproj_kernel, d=d),
        grid=(b, l // tm),
        in_specs=[tok(d), tok(NA_W), tok(DIFF_W), tok(RET_W),
                  pl.BlockSpec((1, 1, 6 * d), lambda i, t: (mod_row(i), 0, 0)),
                  pl.BlockSpec(wo.shape, lambda i, t: (0, 0)),
                  pl.BlockSpec((1, d), lambda i, t: (0, 0)),
                  pl.BlockSpec(wq.shape, lambda i, t: (0, 0)),
                  pl.BlockSpec(sk.shape, lambda i, t: (0, 0, 0))],
        out_specs=[tok(d), tok(d),
                   pl.BlockSpec((1, nhp, PEER_NKEYS, tm), lambda i, t: (i, 0, 0, t))],
        out_shape=[jax.ShapeDtypeStruct((b, l, d), F32), jax.ShapeDtypeStruct((b, l, d), BF16),
                   jax.ShapeDtypeStruct((b, nhp, PEER_NKEYS, l), F32)],
        compiler_params=_cparams(("parallel", "parallel")),
        name="outproj",
    )(x, yna, ydf, yrt, mod3, wo, g2, wq, sk)


def _sort_network(n):
    pairs = []
    p = 1
    while p < n:
        k = p
        while k >= 1:
            for j in range(k % p, n - k, 2 * k):
                for i in range(min(k, n - j - k)):
                    if (i + j) // (2 * p) == (i + j + k) // (2 * p):
                        pairs.append((i + j, i + j + k))
            k //= 2
        p *= 2
    return pairs


def _topk_kernel(st_ref, th_ref, e1_ref, e2_ref, cand_ref, t2_ref):
    neg = -jnp.inf

    def blocks(x):
        return [x[r * 8:(r + 1) * 8, :] for r in range(x.shape[0] // 8)]

    def sort_columns(vs):
        vs = list(vs)
        for i, j in _sort_network(len(vs)):
            vs[i], vs[j] = jnp.maximum(vs[i], vs[j]), jnp.minimum(vs[i], vs[j])
        return vs

    def top_rows(x):
        vs = sort_columns(blocks(x))
        rows = []
        for k in range(PEER_TOPK):
            m = jnp.max(vs[0], axis=0, keepdims=True)
            rows.append(m)
            depth = min(len(vs), PEER_TOPK - 1 - k)
            if depth > 0:
                hit = vs[0] == m
                for r in range(depth):
                    below = vs[r + 1] if r + 1 < len(vs) else neg
                    vs[r] = jnp.where(hit, below, vs[r])
        return rows

    cand_ref[len(PEER_PAIRS):, :] = jnp.full((PEER_NCAND - len(PEER_PAIRS), cand_ref.shape[1]), neg, F32)
    for h in range(PEER_HEADS):
        s1 = st_ref[0, 2 * h]
        s2 = st_ref[0, 2 * h + 1]
        a1 = s1 - jnp.max(s1, axis=0, keepdims=True)
        a2 = s2 - jnp.max(s2, axis=0, keepdims=True)
        t1 = top_rows(a1)
        t2 = top_rows(a2)
        for b in range(PEER_TOPK):
            t2_ref[b:b + 1, :] = t2[b]
        t2s = t2_ref[...]
        off = 0
        for a in range(PEER_TOPK):
            n = PEER_TOPK // (a + 1)
            cand_ref[off:off + n, :] = (t1[a] + t2s)[0:n]
            off += n
        cand = cand_ref[...]
        tau = top_rows(cand)[-1]
        z = jnp.sum(jnp.where(cand >= tau, jnp.exp(cand), 0.0), axis=0, keepdims=True)
        th_ref[0, h] = jnp.exp(tau - a1)
        e1_ref[0, h] = jnp.exp(a1) * (0.5 / z)
        e2_ref[0, h] = jnp.exp(a2)


def _topk(st, tl):
    b, nhp, nk, l = st.shape
    hspec = pl.BlockSpec((1, PEER_HEADS, nk, tl), lambda i, t: (i, 0, 0, t))
    hshape = jax.ShapeDtypeStruct((b, PEER_HEADS, nk, l), F32)
    return pl.pallas_call(
        _topk_kernel,
        grid=(b, l // tl),
        in_specs=[pl.BlockSpec((1, nhp, nk, tl), lambda i, t: (i, 0, 0, t))],
        out_specs=[hspec, hspec, hspec],
        out_shape=[hshape, hshape, hshape],
        scratch_shapes=[pltpu.VMEM((PEER_NCAND, tl), F32), pltpu.VMEM((PEER_TOPK, tl), F32)],
        compiler_params=_cparams(("parallel", "parallel")),
        name="peer_topk",
    )(st)


def _peer_kernel(*refs, final, d, ich):
    if final:
        h2_ref, u_ref, vt_ref, th_ref, e1_ref, e2_ref, x_ref, mod_ref, fg_ref, o_ref = refs[:10]
    else:
        h2_ref, u_ref, vt_ref, th_ref, e1_ref, e2_ref, x_ref, mod_ref, o_ref = refs[:9]
    acc_ref, at_ref, gw_ref, h2t_ref = refs[-4:]
    c = pl.program_id(2)
    tm = h2_ref.shape[1]
    nt = PEER_NKEYS // 8

    @pl.when(c == 0)
    def _():
        acc_ref[...] = jnp.zeros_like(acc_ref)
        h2t_ref[...] = h2_ref[0].T

    at_ref[...] = _dot(u_ref[...], h2t_ref[...])
    for ii in range(ich):
        for lg in range(tm // 128):
            ls = slice(lg * 128, (lg + 1) * 128)
            w = [None] * nt
            for h in range(PEER_HEADS):
                th = jnp.broadcast_to(th_ref[0, h, ii:ii + 1, ls], (8, 128))
                e1 = jnp.broadcast_to(e1_ref[0, h, ii:ii + 1, ls], (8, 128))
                for jt in range(nt):
                    e2 = e2_ref[0, h, jt * 8:(jt + 1) * 8, ls]
                    term = jnp.where(e2 >= th, e2, 0.0) * e1
                    w[jt] = term if w[jt] is None else w[jt] + term
            for jt in range(0, nt, 2):
                r0 = ii * PEER_NKEYS + jt * 8
                a = at_ref[r0:r0 + 16, ls]
                g = a * (1.0 + lax.erf(a * math.sqrt(0.5))) * jnp.concatenate(w[jt:jt + 2], axis=0)
                gw_ref[r0:r0 + 16, ls] = g.astype(BF16)
    acc_ref[...] += _dot(vt_ref[...], gw_ref[...])

    @pl.when(c == pl.num_programs(2) - 1)
    def _():
        mod = mod_ref[0]
        xo = x_ref[0] + mod[:, 5 * d:6 * d] * acc_ref[...].T
        if final:
            xo = _rms(xo) * fg_ref[...]
        o_ref[0] = xo


def _peer(h2, u, vt, th, e1, e2, x, mod3, mod_row, final_g, tm, ich=8):
    b, l, d = x.shape
    ne = u.shape[0]
    ec = ich * PEER_NKEYS
    final = final_g is not None
    ispec = pl.BlockSpec((1, PEER_HEADS, ich, tm), lambda i, t, c: (i, 0, c, t))
    in_specs = [pl.BlockSpec((1, tm, d), lambda i, t, c: (i, t, 0)),
                pl.BlockSpec((ec, d), lambda i, t, c: (c, 0)),
                pl.BlockSpec((d, ec), lambda i, t, c: (0, c)),
                ispec, ispec,
                pl.BlockSpec((1, PEER_HEADS, PEER_NKEYS, tm), lambda i, t, c: (i, 0, 0, t)),
                pl.BlockSpec((1, tm, d), lambda i, t, c: (i, t, 0)),
                pl.BlockSpec((1, 1, 6 * d), lambda i, t, c: (mod_row(i), 0, 0))]
    args = [h2, u, vt, th, e1, e2, x, mod3]
    if final:
        in_specs.append(pl.BlockSpec((1, d), lambda i, t, c: (0, 0)))
        args.append(final_g)
    return pl.pallas_call(
        functools.partial(_peer_kernel, final=final, d=d, ich=ich),
        grid=(b, l // tm, ne // ec),
        in_specs=in_specs,
        out_specs=pl.BlockSpec((1, tm, d), lambda i, t, c: (i, t, 0)),
        out_shape=jax.ShapeDtypeStruct((b, l, d), F32),
        scratch_shapes=[pltpu.VMEM((d, tm), F32), pltpu.VMEM((ec, tm), F32),
                        pltpu.VMEM((ec, tm), BF16), pltpu.VMEM((d, tm), BF16)],
        compiler_params=_cparams(("parallel", "parallel", "arbitrary")),
        name="peer_final" if final else "peer",
    )(*args)


def _rope_tables(l, dim, width):
    t = jnp.arange(l)
    rows = (t // GRID_W).astype(F32)
    cols = (t % GRID_W).astype(F32)
    half = dim // 2
    inv = jnp.power(ROPE_BASE, -jnp.arange(0, half, 2, dtype=F32) / half)
    ang = jnp.concatenate([rows[:, None] * inv, cols[:, None] * inv], axis=-1)
    pair = (np.arange(width) % dim) // 2
    sign = np.where(np.arange(width) % 2 == 0, -1.0, 1.0).astype(np.float32)
    return jnp.cos(ang)[:, pair], jnp.sin(ang)[:, pair] * sign


def _swap_pairs(w):
    d, n = w.shape
    return w.reshape(d, n // 2, 2)[:, :, ::-1].reshape(d, n)


def _block_mixer_and_peer(x, ctx, mod3, layer, need_ctx, p, tables, final_g):
    b, l, d = x.shape
    lc = ctx.shape[1]
    row_x = lambda i: i
    row_c = lambda i: b
    lam_init = 0.8 - 0.6 * math.exp(-0.3 * layer)

    lat = _inproj(x, mod3, row_x, p["g1"], p["w_in"], tables, 256)
    cx = _inproj(ctx, mod3, row_c, p["g1"], p["w_in"], None, lc)
    naq, nak, nav, dfq, dfk, dfv, rtq, rtk, rtv, rtg = lat
    cnaq, cnak, cnav, cdfq, cdfk, cdfv, crtq, crtk, crtv, crtg = cx

    y_na = _na_latent(naq, nak, nav, cnak, cnav, p["rpb"], 4)
    y_df = _diff(dfq, cdfk, cdfv, dfk, dfv, p["lam"], p["subln"], lam_init, 256)
    y_rt, yc_rt = _retention(p["decay"], crtq, crtk, crtv, crtg, rtq, rtk, rtv, rtg)

    def channel_mix(xx, yna, ydf, yrt, row, tm, fg):
        xo, h2, st = _outproj(xx, yna, ydf, yrt, mod3, row, p["w_out"], p["g2"], p["wq"], p["sk"], tm)
        th, e1, e2 = _topk(st, min(tm, 256))
        return _peer(h2, p["u"], p["vt"], th, e1, e2, xo, mod3, row, fg, tm)

    x = channel_mix(x, y_na, y_df, y_rt, row_x, min(l, 512), final_g)
    if need_ctx:
        yc_na = _na_ctx(cnaq, cnak, cnav)
        yc_df = _diff(cdfq, cdfk, cdfv, None, None, p["lam"], p["subln"], lam_init, lc)
        ctx = channel_mix(ctx, yc_na, yc_df, yc_rt, row_c, lc, None)
    return x, ctx


def kernel(x, c, ctx, c_ctx, w_ada, b_ada, norm1_g, w_in, na_rpb, diff_lambda, diff_subln_g,
           ret_decay_logit, w_out, norm2_g, peer_wq, peer_subkeys, peer_u, peer_v, final_g):
    b, l, d = x.shape
    depth = w_ada.shape[0]
    x = x.astype(F32)
    ctx = ctx.astype(F32)
    npad = -(-(b + 1) // 8) * 8
    cpad = jnp.zeros((npad, d), F32).at[:b].set(c.astype(F32)).at[b].set(c_ctx.astype(F32))
    mod = _ada(cpad, w_ada.astype(F32), b_ada.astype(F32))
    tables = _rope_tables(l, DIFF_DIM, DIFF_QK_W) + _rope_tables(l, RET_QK, RET_QK_W)
    for layer in range(depth):
        wi = w_in[layer]
        w_ext = jnp.concatenate(
            [wi] + [_swap_pairs(wi[:, c0:c0 + 256]) for c0 in (C_DFQ, C_DFK, C_RTQ, C_RTK)], axis=1)
        p = {
            "g1": norm1_g[layer].astype(F32).reshape(1, d),
            "g2": norm2_g[layer].astype(F32).reshape(1, d),
            "w_in": w_ext.astype(BF16),
            "rpb": na_rpb[layer],
            "lam": diff_lambda[layer].astype(F32),
            "subln": jnp.tile(diff_subln_g[layer].astype(F32), DIFF_HEADS).reshape(1, DIFF_W),
            "decay": ret_decay_logit[layer],
            "w_out": w_out[layer].astype(BF16),
            "wq": peer_wq[layer].astype(BF16),
            "sk": peer_subkeys[layer].astype(BF16).reshape(2 * PEER_HEADS, PEER_NKEYS, PEER_HALF),
            "u": peer_u[layer].astype(BF16),
            "vt": peer_v[layer].astype(BF16).T,
        }
        last = layer == depth - 1
        x, ctx = _block_mixer_and_peer(x, ctx, mod[layer].reshape(npad, 1, 6 * d), layer, not last, p,
                                       tables, final_g.astype(F32).reshape(1, d) if last else None)
    return x
```

```python
import functools
import math

import numpy as np
import jax
import jax.numpy as jnp
from jax import lax
from jax.experimental import pallas as pl
from jax.experimental.pallas import tpu as pltpu

F32 = jnp.float32
BF16 = jnp.bfloat16

GRID_W = 64
NORM_EPS = 1e-6
ROPE_BASE = 10000.0
NEG_INF = -1e30
LOG2E = math.log2(math.e)

NA_HEADS = 4
NA_DIM = 64
NA_WIN_ROWS = 8
NA_WIN_COLS = 16
DIFF_HEADS = 4
DIFF_DIM = 32
DIFF_VDIM = 64
RET_HEADS = 4
RET_QK = 64
RET_V = 128
RET_CHUNK = 128
RET_UNROLL = 4

NA_W = NA_HEADS * NA_DIM
DIFF_QK_W = DIFF_HEADS * 2 * DIFF_DIM
DIFF_W = DIFF_HEADS * DIFF_VDIM
RET_QK_W = RET_HEADS * RET_QK
RET_W = RET_HEADS * RET_V
MIX_W = NA_W + DIFF_W + RET_W
IN_COLS = 3 * NA_W + 2 * DIFF_QK_W + DIFF_W + 2 * RET_QK_W + 2 * RET_W

PEER_HEADS = 8
PEER_NKEYS = 128
PEER_KDIM = 256
PEER_TOPK = 16
PEER_HALF = PEER_KDIM // 2

C_NAQ, C_NAK, C_NAV = 0, 256, 512
C_DFQ, C_DFK, C_DFV = 768, 1024, 1280
C_RTQ, C_RTK, C_RTV, C_RTG = 1536, 1792, 2048, 2560
C_DFQ_S, C_DFK_S, C_RTQ_S, C_RTK_S = 3072, 3328, 3584, 3840
IN_COLS_EXT = 4096

PEER_PAIRS = tuple((a, b) for a in range(PEER_TOPK) for b in range(PEER_TOPK // (a + 1)))
PEER_NCAND = 64

VMEM_LIMIT = 56 * 1024 * 1024


def _cparams(sem):
    return pltpu.CompilerParams(dimension_semantics=sem, vmem_limit_bytes=VMEM_LIMIT)


def _nt(a, b):
    return lax.dot_general(a, b, (((1,), (1,)), ((), ())), preferred_element_type=F32)


def _tn(a, b):
    return lax.dot_general(a, b, (((0,), (0,)), ((), ())), preferred_element_type=F32)


def _dot(a, b):
    return jnp.dot(a, b, preferred_element_type=F32)


def _rms(x):
    return x * lax.rsqrt(jnp.mean(x * x, axis=-1, keepdims=True) + NORM_EPS)


def _silu(x):
    return x * jax.nn.sigmoid(x)


def _ada_kernel(c_ref, w_ref, b_ref, o_ref):
    s = _silu(c_ref[...])
    o_ref[0] = jnp.dot(s, w_ref[0], preferred_element_type=F32,
                       precision=lax.Precision.HIGHEST) + b_ref[0]


def _ada(cpad, w_ada, b_ada):
    depth, d, n = w_ada.shape
    tn = 1536
    return pl.pallas_call(
        _ada_kernel,
        grid=(depth, n // tn),
        in_specs=[pl.BlockSpec((cpad.shape[0], d), lambda l, j: (0, 0)),
                  pl.BlockSpec((1, d, tn), lambda l, j: (l, 0, j)),
                  pl.BlockSpec((1, 1, tn), lambda l, j: (l, 0, j))],
        out_specs=pl.BlockSpec((1, cpad.shape[0], tn), lambda l, j: (l, 0, j)),
        out_shape=jax.ShapeDtypeStruct((depth, cpad.shape[0], n), F32),
        compiler_params=_cparams(("parallel", "parallel")),
        name="ada",
    )(cpad, w_ada, b_ada.reshape(depth, 1, n))


def _inproj_kernel(*refs, rope, d):
    if rope:
        (x_ref, mod_ref, g_ref, w_ref, cd_ref, sd_ref, cr_ref, sr_ref,
         naq, nak, nav, dfq, dfk, dfv, rtq, rtk, rtv, rtg) = refs
    else:
        (x_ref, mod_ref, g_ref, w_ref,
         naq, nak, nav, dfq, dfk, dfv, rtq, rtk, rtv, rtg) = refs
    mod = mod_ref[0]
    h = _rms(x_ref[0]) * g_ref[...] * (1.0 + mod[:, d:2 * d]) + mod[:, 0:d]
    hb = h.astype(BF16)

    def proj(c0, n):
        return _dot(hb, w_ref[:, c0:c0 + n])

    def roped(c0, c0s, cos_ref, sin_ref):
        if rope:
            return proj(c0, 256) * cos_ref[...] + proj(c0s, 256) * sin_ref[...]
        return proj(c0, 256)

    naq[0] = (proj(C_NAQ, 256) * NA_DIM ** -0.5).astype(BF16)
    nak[0] = proj(C_NAK, 256).astype(BF16)
    nav[0] = proj(C_NAV, 256).astype(BF16)
    dfq[0] = (roped(C_DFQ, C_DFQ_S, cd_ref if rope else None, sd_ref if rope else None)
              * (DIFF_DIM ** -0.5 * LOG2E)).astype(BF16)
    dfk[0] = roped(C_DFK, C_DFK_S, cd_ref if rope else None, sd_ref if rope else None).astype(BF16)
    dfv[0] = proj(C_DFV, 256).astype(BF16)
    rtq[0] = roped(C_RTQ, C_RTQ_S, cr_ref if rope else None, sr_ref if rope else None).astype(BF16)
    rtk[0] = (roped(C_RTK, C_RTK_S, cr_ref if rope else None, sr_ref if rope else None)
              * RET_QK ** -0.5).astype(BF16)
    rtv[0] = proj(C_RTV, 512).astype(BF16)
    rtg[0] = proj(C_RTG, 512)


def _inproj(x, mod3, mod_row, g1, w_ext, tables, tm):
    b, l, d = x.shape
    rope = tables is not None
    ncols = IN_COLS_EXT if rope else IN_COLS
    in_specs = [pl.BlockSpec((1, tm, d), lambda i, t: (i, t, 0)),
                pl.BlockSpec((1, 1, 6 * d), lambda i, t: (mod_row(i), 0, 0)),
                pl.BlockSpec((1, d), lambda i, t: (0, 0)),
                pl.BlockSpec((d, ncols), lambda i, t: (0, 0))]
    args = [x, mod3, g1, w_ext]
    if rope:
        in_specs += [pl.BlockSpec((tm, 256), lambda i, t: (t, 0))] * 4
        args += list(tables)
    widths = (256, 256, 256, 256, 256, 256, 256, 256, 512, 512)
    dtypes = (BF16,) * 9 + (F32,)
    out_specs = [pl.BlockSpec((1, tm, w), lambda i, t: (i, t, 0)) for w in widths]
    out_shape = [jax.ShapeDtypeStruct((b, l, w), dt) for w, dt in zip(widths, dtypes)]
    return pl.pallas_call(
        functools.partial(_inproj_kernel, rope=rope, d=d),
        grid=(b, l // tm),
        in_specs=in_specs, out_specs=out_specs, out_shape=out_shape,
        compiler_params=_cparams(("parallel", "parallel")),
        name="inproj_rope" if rope else "inproj_ctx",
    )(*args)


def _na_plan(rows, rb):
    wr = min(NA_WIN_ROWS, rows)
    kr = min(rb + wr - 1, rows)
    nblk = rows // rb
    win0 = np.clip(np.arange(rows) - wr // 2, 0, rows - wr)
    ks = np.clip(np.arange(nblk) * rb - wr // 2, 0, rows - kr)
    pats, pat_of = [], np.zeros(nblk, np.int32)
    for g in range(nblk):
        r = g * rb + np.arange(rb)
        krow = ks[g] + np.arange(kr)
        dr = krow[None, :] - r[:, None] + (NA_WIN_ROWS - 1)
        ok = (krow[None, :] >= win0[r][:, None]) & (krow[None, :] < win0[r][:, None] + wr)
        assert ok.sum(axis=1).min() == wr
        key = (np.where(ok, dr, 0).tobytes(), ok.tobytes())
        for p, (k2, _, _) in enumerate(pats):
            if k2 == key:
                pat_of[g] = p
                break
        else:
            pat_of[g] = len(pats)
            pats.append((key, np.where(ok, dr, 0), ok))
    dr_idx = np.stack([p[1] for p in pats])
    row_ok = np.stack([p[2] for p in pats])
    qcol = np.arange(GRID_W)
    kcol = np.arange(GRID_W)
    cs = np.clip(qcol - NA_WIN_COLS // 2, 0, GRID_W - NA_WIN_COLS)
    col_ok = (kcol[None, :] >= cs[:, None]) & (kcol[None, :] < cs[:, None] + NA_WIN_COLS)
    dc_idx = np.clip(kcol[None, :] - qcol[:, None], 1 - NA_WIN_COLS, NA_WIN_COLS - 1) + NA_WIN_COLS - 1
    return kr, ks.astype(np.int32), pat_of, dr_idx, row_ok, dc_idx, col_ok


def _na_bias(rpb, plan, rb):
    kr, _, _, dr_idx, row_ok, dc_idx, col_ok = plan
    p = dr_idx.shape[0]
    nr, nc = 2 * NA_WIN_ROWS - 1, 2 * NA_WIN_COLS - 1
    row_hot = ((np.arange(nr) == dr_idx[..., None]) & row_ok[..., None]).astype(np.float32)
    col_hot = ((np.arange(nc) == dc_idx[..., None]) & col_ok[..., None]).astype(np.float32)
    bias = jnp.einsum("hrc,pqwr,xyc->hpqxwy", rpb.astype(F32), row_hot, col_hot,
                      precision=lax.Precision.HIGHEST)
    ok = row_ok[:, :, None, :, None] & col_ok[None, None, :, None, :]
    bias = jnp.where(ok[None], bias, NEG_INF)
    return bias.reshape(NA_HEADS, p, rb * GRID_W, kr * GRID_W)


def _na_kernel(ks_ref, pat_ref, q_ref, kx_ref, vx_ref, kc_ref, vc_ref, bias_ref, o_ref, *, nkeys):
    g = pl.program_id(1)
    q = q_ref[0]
    k0 = pl.multiple_of(ks_ref[g] * GRID_W, GRID_W)
    kw = kx_ref[0, pl.ds(k0, nkeys), :]
    vw = vx_ref[0, pl.ds(k0, nkeys), :]
    kc = kc_ref[0]
    vc = vc_ref[0]
    pat = pat_ref[g]
    lane = lax.broadcasted_iota(jnp.int32, (1, NA_W), 1)
    out = jnp.zeros((q.shape[0], NA_W), F32)
    for h in range(NA_HEADS):
        hm = (lane >= NA_DIM * h) & (lane < NA_DIM * (h + 1))
        qh = jnp.where(hm, q, jnp.zeros_like(q))
        sw = _nt(qh, kw) + bias_ref[h, pat]
        sc = _nt(qh, kc)
        m = jnp.maximum(jnp.max(sw, axis=-1, keepdims=True), jnp.max(sc, axis=-1, keepdims=True))
        pw = jnp.exp(sw - m)
        pc = jnp.exp(sc - m)
        den = jnp.sum(pw, axis=-1, keepdims=True) + jnp.sum(pc, axis=-1, keepdims=True)
        o = _dot(pw.astype(BF16), vw) + _dot(pc.astype(BF16), vc)
        out = out + jnp.where(hm, o / den, 0.0)
    o_ref[0] = out.astype(o_ref.dtype)


def _na_latent(q, kx, vx, kc, vc, rpb, rb):
    b, l, _ = q.shape
    lc = kc.shape[1]
    rows = l // GRID_W
    plan = _na_plan(rows, rb)
    kr, ks, pat_of = plan[0], plan[1], plan[2]
    bias = _na_bias(rpb, plan, rb)
    tq, nkeys = rb * GRID_W, kr * GRID_W
    grid_spec = pltpu.PrefetchScalarGridSpec(
        num_scalar_prefetch=2,
        grid=(b, rows // rb),
        in_specs=[pl.BlockSpec((1, tq, NA_W), lambda i, g, *_: (i, g, 0)),
                  pl.BlockSpec((1, l, NA_W), lambda i, g, *_: (i, 0, 0)),
                  pl.BlockSpec((1, l, NA_W), lambda i, g, *_: (i, 0, 0)),
                  pl.BlockSpec((1, lc, NA_W), lambda i, g, *_: (i, 0, 0)),
                  pl.BlockSpec((1, lc, NA_W), lambda i, g, *_: (i, 0, 0)),
                  pl.BlockSpec(bias.shape, lambda i, g, *_: (0, 0, 0, 0))],
        out_specs=pl.BlockSpec((1, tq, NA_W), lambda i, g, *_: (i, g, 0)),
    )
    return pl.pallas_call(
        functools.partial(_na_kernel, nkeys=nkeys),
        grid_spec=grid_spec,
        out_shape=jax.ShapeDtypeStruct((b, l, NA_W), BF16),
        compiler_params=_cparams(("parallel", "arbitrary")),
        name="na_latent",
    )(jnp.asarray(ks), jnp.asarray(pat_of), q, kx, vx, kc, vc, bias)


def _na_ctx_kernel(q_ref, kc_ref, vc_ref, o_ref):
    q = q_ref[0]
    kc = kc_ref[0]
    vc = vc_ref[0]
    lane = lax.broadcasted_iota(jnp.int32, (1, NA_W), 1)
    out = jnp.zeros((q.shape[0], NA_W), F32)
    for h in range(NA_HEADS):
        hm = (lane >= NA_DIM * h) & (lane < NA_DIM * (h + 1))
        qh = jnp.where(hm, q, jnp.zeros_like(q))
        sc = _nt(qh, kc)
        pc = jnp.exp(sc - jnp.max(sc, axis=-1, keepdims=True))
        o = _dot(pc.astype(BF16), vc)
        out = out + jnp.where(hm, o / jnp.sum(pc, axis=-1, keepdims=True), 0.0)
    o_ref[0] = out.astype(o_ref.dtype)


def _na_ctx(q, kc, vc):
    b, lc, _ = q.shape
    spec = pl.BlockSpec((1, lc, NA_W), lambda i: (i, 0, 0))
    return pl.pallas_call(
        _na_ctx_kernel, grid=(b,), in_specs=[spec, spec, spec], out_specs=spec,
        out_shape=jax.ShapeDtypeStruct((b, lc, NA_W), BF16),
        compiler_params=_cparams(("parallel",)), name="na_ctx",
    )(q, kc, vc)


def _diff_kernel(*refs, lam_init, has_x):
    if has_x:
        q_ref, kc_ref, vc_ref, kx_ref, vx_ref, lam_ref, g_ref, o_ref = refs
    else:
        q_ref, kc_ref, vc_ref, lam_ref, g_ref, o_ref = refs
    q = q_ref[0]
    kc = kc_ref[0]
    vc = vc_ref[0]
    lp = lam_ref[...]
    lam = (jnp.exp(jnp.sum(lp[0:1] * lp[1:2], axis=-1, keepdims=True))
           - jnp.exp(jnp.sum(lp[2:3] * lp[3:4], axis=-1, keepdims=True)) + lam_init)
    lane = lax.broadcasted_iota(jnp.int32, (1, DIFF_W), 1)
    out = jnp.zeros((q.shape[0], DIFF_W), F32)
    def scores(h):
        res = []
        for c in range(2):
            lo = DIFF_VDIM * h + DIFF_DIM * c
            qm = jnp.where((lane >= lo) & (lane < lo + DIFF_DIM), q, jnp.zeros_like(q))
            res.append((_nt(qm, kc), _nt(qm, kx_ref[0]) if has_x else None))
        return res

    nxt = scores(0)
    for h in range(DIFF_HEADS):
        cur = nxt
        if h + 1 < DIFF_HEADS:
            nxt = scores(h + 1)
        parts = []
        for c in range(2):
            sc, sx = cur[c]
            m = jnp.max(sc, axis=-1, keepdims=True)
            if has_x:
                m = jnp.maximum(m, jnp.max(sx, axis=-1, keepdims=True))
            ec = jnp.exp2(sc - m)
            den = jnp.sum(ec, axis=-1, keepdims=True)
            ex = None
            if has_x:
                ex = jnp.exp2(sx - m)
                den = den + jnp.sum(ex, axis=-1, keepdims=True)
            parts.append((ec, ex, den))
        (ec0, ex0, den0), (ec1, ex1, den1) = parts
        rho = lam * den0 / den1
        o = _dot((ec0 - ec1 * rho).astype(BF16), vc)
        if has_x:
            o = o + _dot((ex0 - ex1 * rho).astype(BF16), vx_ref[0])
        hm = (lane >= DIFF_VDIM * h) & (lane < DIFF_VDIM * (h + 1))
        oh = jnp.where(hm, o / den0, 0.0)
        ms = jnp.sum(oh * oh, axis=-1, keepdims=True) * (1.0 / DIFF_VDIM)
        out = out + oh * lax.rsqrt(ms + NORM_EPS)
    o_ref[0] = (out * g_ref[...] * (1.0 - lam_init)).astype(o_ref.dtype)


def _diff(q, kc, vc, kx, vx, lam_p, g4, lam_init, tq):
    b, l, _ = q.shape
    lc = kc.shape[1]
    has_x = kx is not None
    cspec = pl.BlockSpec((1, lc, DIFF_W), lambda i, t: (i, 0, 0))
    in_specs = [pl.BlockSpec((1, tq, DIFF_W), lambda i, t: (i, t, 0)), cspec, cspec]
    args = [q, kc, vc]
    if has_x:
        xspec = pl.BlockSpec((1, kx.shape[1], DIFF_W), lambda i, t: (i, 0, 0))
        in_specs += [xspec, xspec]
        args += [kx, vx]
    in_specs += [pl.BlockSpec(lam_p.shape, lambda i, t: (0, 0)),
                 pl.BlockSpec((1, DIFF_W), lambda i, t: (0, 0))]
    args += [lam_p, g4]
    return pl.pallas_call(
        functools.partial(_diff_kernel, lam_init=lam_init, has_x=has_x),
        grid=(b, l // tq), in_specs=in_specs,
        out_specs=pl.BlockSpec((1, tq, DIFF_W), lambda i, t: (i, t, 0)),
        out_shape=jax.ShapeDtypeStruct((b, l, DIFF_W), BF16),
        compiler_params=_cparams(("parallel", "arbitrary")),
        name="diff_latent" if has_x else "diff_ctx",
    )(*args)


def _ret_kernel(dl_ref, cq_ref, ck_ref, cv_ref, cg_ref, xq_ref, xk_ref, xv_ref, xg_ref,
                yx_ref, yc_ref, s_ref, *, n_ctx, n_lat):
    h = pl.program_id(1)
    c = RET_CHUNK
    lane = lax.broadcasted_iota(jnp.int32, (1, RET_QK_W), 1)
    hm = (lane >= RET_QK * h) & (lane < RET_QK * (h + 1))
    row = lax.broadcasted_iota(jnp.int32, (c, c), 0).astype(F32)
    col = lax.broadcasted_iota(jnp.int32, (c, c), 1).astype(F32)
    roww = lax.broadcasted_iota(jnp.int32, (c, RET_QK_W), 0).astype(F32)

    def log_sigmoid(d):
        x = jnp.full((1, 1), dl_ref[d, h], F32)
        return jnp.minimum(x, 0.0) - jnp.log1p(jnp.exp(-jnp.abs(x)))

    lgf, lgb = log_sigmoid(0), log_sigmoid(1)
    diff = row - col
    fwd = (jnp.where(diff >= 0, jnp.exp(lgf * jnp.maximum(diff, 0.0)), 0.0),
           jnp.exp(lgf * (roww + 1.0)),
           jnp.exp(lgf * (c - 1.0 - roww)),
           jnp.exp(lgf * c))
    bwd = (jnp.where(diff <= 0, jnp.exp(lgb * jnp.maximum(-diff, 0.0)), 0.0),
           jnp.exp(lgb * (c - roww)),
           jnp.exp(lgb * roww),
           jnp.exp(lgb * c))

    def step(q_ref, k_ref, v_ref, g_ref, out_ref, ci, mats, sdir, second):
        dm, qd, kd, cd = mats
        sl = pl.ds(pl.multiple_of(ci * c, c), c)
        q = q_ref[0, sl, :]
        q = jnp.where(hm, q, jnp.zeros_like(q))
        k = k_ref[0, sl, :]
        v = v_ref[0, sl, :]
        att = _nt(q, k) * dm
        s = s_ref[sdir]
        o = _dot(att.astype(BF16), v) + _dot((q.astype(F32) * qd).astype(BF16), s.astype(BF16))
        s_ref[sdir] = cd * s + _tn((k.astype(F32) * kd).astype(BF16), v)
        if second:
            tot = out_ref[0, sl, :] + o
            out_ref[0, sl, :] = _rms(tot) * _silu(g_ref[0, sl, :])
        else:
            out_ref[0, sl, :] = o

    def both(refs, n, i, second):
        step(*refs, i, fwd, 0, second)
        step(*refs, n - 1 - i, bwd, 1, second)

    s_ref[...] = jnp.zeros_like(s_ref)
    crefs = (cq_ref, ck_ref, cv_ref, cg_ref, yc_ref)
    xrefs = (xq_ref, xk_ref, xv_ref, xg_ref, yx_ref)
    for i in range(n_ctx):
        both(crefs, n_ctx, i, i >= n_ctx // 2)
    for second in (False, True):
        def body(i, carry, second=second):
            both(xrefs, n_lat, i, second)
            return carry

        lax.fori_loop(n_lat // 2 if second else 0, n_lat if second else n_lat // 2, body, 0,
                      unroll=RET_UNROLL if (n_lat // 2) % RET_UNROLL == 0 else 1)


def _retention(decay_logit, cq, ck, cv, cg, xq, xk, xv, xg):
    b, l, _ = xq.shape
    lc = cq.shape[1]

    def qk_spec(n):
        return pl.BlockSpec((1, n, RET_QK_W), lambda i, h: (i, 0, 0))

    def v_spec(n):
        return pl.BlockSpec((1, n, RET_V), lambda i, h: (i, 0, h))

    return pl.pallas_call(
        functools.partial(_ret_kernel, n_ctx=lc // RET_CHUNK, n_lat=l // RET_CHUNK),
        grid=(b, RET_HEADS),
        in_specs=[pl.BlockSpec(memory_space=pltpu.SMEM),
                  qk_spec(lc), qk_spec(lc), v_spec(lc), v_spec(lc),
                  qk_spec(l), qk_spec(l), v_spec(l), v_spec(l)],
        out_specs=[v_spec(l), v_spec(lc)],
        out_shape=[jax.ShapeDtypeStruct((b, l, RET_W), F32), jax.ShapeDtypeStruct((b, lc, RET_W), F32)],
        scratch_shapes=[pltpu.VMEM((2, RET_QK_W, RET_V), F32)],
        compiler_params=_cparams(("parallel", "arbitrary")),
        name="retention",
    )(decay_logit.astype(F32), cq, ck, cv, cg, xq, xk, xv, xg)


def _outproj_kernel(x_ref, yna_ref, ydf_ref, yrt_ref, mod_ref, wo_ref, g2_ref, wq_ref, sk_ref,
                    xo_ref, h2_ref, st_ref, *, d):
    y = (_dot(yna_ref[0], wo_ref[0:NA_W, :])
         + _dot(ydf_ref[0], wo_ref[NA_W:NA_W + DIFF_W, :])
         + _dot(yrt_ref[0].astype(BF16), wo_ref[NA_W + DIFF_W:MIX_W, :]))
    mod = mod_ref[0]
    x = x_ref[0] + mod[:, 2 * d:3 * d] * y
    xo_ref[0] = x
    h2 = (_rms(x) * g2_ref[...] * (1.0 + mod[:, 4 * d:5 * d]) + mod[:, 3 * d:4 * d]).astype(BF16)
    h2_ref[0] = h2
    qp = _dot(h2, wq_ref[...]).astype(BF16)
    for hp in range(2 * PEER_HEADS):
        st_ref[0, hp] = _nt(sk_ref[hp], qp[:, hp * PEER_HALF:(hp + 1) * PEER_HALF])


def _outproj(x, yna, ydf, yrt, mod3, mod_row, wo, g2, wq, sk, tm):
    b, l, d = x.shape
    nhp = 2 * PEER_HEADS

    def tok(w):
        return pl.BlockSpec((1, tm, w), lambda i, t: (i, t, 0))

    return pl.pallas_call(
        functools.partial(_outproj_kernel, d=d),
        grid=(b, l // tm),
        in_specs=[tok(d), tok(NA_W), tok(DIFF_W), tok(RET_W),
                  pl.BlockSpec((1, 1, 6 * d), lambda i, t: (mod_row(i), 0, 0)),
                  pl.BlockSpec(wo.shape, lambda i, t: (0, 0)),
                  pl.BlockSpec((1, d), lambda i, t: (0, 0)),
                  pl.BlockSpec(wq.shape, lambda i, t: (0, 0)),
                  pl.BlockSpec(sk.shape, lambda i, t: (0, 0, 0))],
        out_specs=[tok(d), tok(d),
                   pl.BlockSpec((1, nhp, PEER_NKEYS, tm), lambda i, t: (i, 0, 0, t))],
        out_shape=[jax.ShapeDtypeStruct((b, l, d), F32), jax.ShapeDtypeStruct((b, l, d), BF16),
                   jax.ShapeDtypeStruct((b, nhp, PEER_NKEYS, l), F32)],
        compiler_params=_cparams(("parallel", "parallel")),
        name="outproj",
    )(x, yna, ydf, yrt, mod3, wo, g2, wq, sk)


def _sort_network(n):
    pairs = []
    p = 1
    while p < n:
        k = p
        while k >= 1:
            for j in range(k % p, n - k, 2 * k):
                for i in range(min(k, n - j - k)):
                    if (i + j) // (2 * p) == (i + j + k) // (2 * p):
                        pairs.append((i + j, i + j + k))
            k //= 2
        p *= 2
    return pairs


def _topk_kernel(st_ref, th_ref, e1_ref, e2_ref, cand_ref, t2_ref):
    neg = -jnp.inf

    def blocks(x):
        return [x[r * 8:(r + 1) * 8, :] for r in range(x.shape[0] // 8)]

    def sort_columns(vs):
        vs = list(vs)
        for i, j in _sort_network(len(vs)):
            vs[i], vs[j] = jnp.maximum(vs[i], vs[j]), jnp.minimum(vs[i], vs[j])
        return vs

    def top_rows(x):
        vs = sort_columns(blocks(x))
        rows = []
        for k in range(PEER_TOPK):
            m = jnp.max(vs[0], axis=0, keepdims=True)
            rows.append(m)
            depth = min(len(vs), PEER_TOPK - 1 - k)
            if depth > 0:
                hit = vs[0] == m
                for r in range(depth):
                    below = vs[r + 1] if r + 1 < len(vs) else neg
                    vs[r] = jnp.where(hit, below, vs[r])
        return rows

    cand_ref[len(PEER_PAIRS):, :] = jnp.full((PEER_NCAND - len(PEER_PAIRS), cand_ref.shape[1]), neg, F32)
    for h in range(PEER_HEADS):
        s1 = st_ref[0, 2 * h]
        s2 = st_ref[0, 2 * h + 1]
        a1 = s1 - jnp.max(s1, axis=0, keepdims=True)
        a2 = s2 - jnp.max(s2, axis=0, keepdims=True)
        t1 = top_rows(a1)
        t2 = top_rows(a2)
        for b in range(PEER_TOPK):
            t2_ref[b:b + 1, :] = t2[b]
        t2s = t2_ref[...]
        off = 0
        for a in range(PEER_TOPK):
            n = PEER_TOPK // (a + 1)
            cand_ref[off:off + n, :] = (t1[a] + t2s)[0:n]
            off += n
        cand = cand_ref[...]
        tau = top_rows(cand)[-1]
        z = jnp.sum(jnp.where(cand >= tau, jnp.exp(cand), 0.0), axis=0, keepdims=True)
        th_ref[0, h] = jnp.exp(tau - a1)
        e1_ref[0, h] = jnp.exp(a1) * (0.5 / z)
        e2_ref[0, h] = jnp.exp(a2)


def _topk(st, tl):
    b, nhp, nk, l = st.shape
    hspec = pl.BlockSpec((1, PEER_HEADS, nk, tl), lambda i, t: (i, 0, 0, t))
    hshape = jax.ShapeDtypeStruct((b, PEER_HEADS, nk, l), F32)
    return pl.pallas_call(
        _topk_kernel,
        grid=(b, l // tl),
        in_specs=[pl.BlockSpec((1, nhp, nk, tl), lambda i, t: (i, 0, 0, t))],
        out_specs=[hspec, hspec, hspec],
        out_shape=[hshape, hshape, hshape],
        scratch_shapes=[pltpu.VMEM((PEER_NCAND, tl), F32), pltpu.VMEM((PEER_TOPK, tl), F32)],
        compiler_params=_cparams(("parallel", "parallel")),
        name="peer_topk",
    )(st)


def _peer_kernel(*refs, final, d, ich):
    if final:
        h2_ref, u_ref, vt_ref, th_ref, e1_ref, e2_ref, x_ref, mod_ref, fg_ref, o_ref = refs[:10]
    else:
        h2_ref, u_ref, vt_ref, th_ref, e1_ref, e2_ref, x_ref, mod_ref, o_ref = refs[:9]
    acc_ref, at_ref, gw_ref, h2t_ref = refs[-4:]
    c = pl.program_id(2)
    tm = h2_ref.shape[1]
    nt = PEER_NKEYS // 8

    @pl.when(c == 0)
    def _():
        acc_ref[...] = jnp.zeros_like(acc_ref)
        h2t_ref[...] = h2_ref[0].T

    at_ref[...] = _dot(u_ref[...], h2t_ref[...])
    for ii in range(ich):
        for lg in range(tm // 128):
            ls = slice(lg * 128, (lg + 1) * 128)
            w = [None] * nt
            for h in range(PEER_HEADS):
                th = jnp.broadcast_to(th_ref[0, h, ii:ii + 1, ls], (8, 128))
                e1 = jnp.broadcast_to(e1_ref[0, h, ii:ii + 1, ls], (8, 128))
                for jt in range(nt):
                    e2 = e2_ref[0, h, jt * 8:(jt + 1) * 8, ls]
                    term = jnp.where(e2 >= th, e2, 0.0) * e1
                    w[jt] = term if w[jt] is None else w[jt] + term
            for jt in range(0, nt, 2):
                r0 = ii * PEER_NKEYS + jt * 8
                a = at_ref[r0:r0 + 16, ls]
                g = a * (1.0 + lax.erf(a * math.sqrt(0.5))) * jnp.concatenate(w[jt:jt + 2], axis=0)
                gw_ref[r0:r0 + 16, ls] = g.astype(BF16)
    acc_ref[...] += _dot(vt_ref[...], gw_ref[...])

    @pl.when(c == pl.num_programs(2) - 1)
    def _():
        mod = mod_ref[0]
        xo = x_ref[0] + mod[:, 5 * d:6 * d] * acc_ref[...].T
        if final:
            xo = _rms(xo) * fg_ref[...]
        o_ref[0] = xo


def _peer(h2, u, vt, th, e1, e2, x, mod3, mod_row, final_g, tm, ich=8):
    b, l, d = x.shape
    ne = u.shape[0]
    ec = ich * PEER_NKEYS
    final = final_g is not None
    ispec = pl.BlockSpec((1, PEER_HEADS, ich, tm), lambda i, t, c: (i, 0, c, t))
    in_specs = [pl.BlockSpec((1, tm, d), lambda i, t, c: (i, t, 0)),
                pl.BlockSpec((ec, d), lambda i, t, c: (c, 0)),
                pl.BlockSpec((d, ec), lambda i, t, c: (0, c)),
                ispec, ispec,
                pl.BlockSpec((1, PEER_HEADS, PEER_NKEYS, tm), lambda i, t, c: (i, 0, 0, t)),
                pl.BlockSpec((1, tm, d), lambda i, t, c: (i, t, 0)),
                pl.BlockSpec((1, 1, 6 * d), lambda i, t, c: (mod_row(i), 0, 0))]
    args = [h2, u, vt, th, e1, e2, x, mod3]
    if final:
        in_specs.append(pl.BlockSpec((1, d), lambda i, t, c: (0, 0)))
        args.append(final_g)
    return pl.pallas_call(
        functools.partial(_peer_kernel, final=final, d=d, ich=ich),
        grid=(b, l // tm, ne // ec),
        in_specs=in_specs,
        out_specs=pl.BlockSpec((1, tm, d), lambda i, t, c: (i, t, 0)),
        out_shape=jax.ShapeDtypeStruct((b, l, d), F32),
        scratch_shapes=[pltpu.VMEM((d, tm), F32), pltpu.VMEM((ec, tm), F32),
                        pltpu.VMEM((ec, tm), BF16), pltpu.VMEM((d, tm), BF16)],
        compiler_params=_cparams(("parallel", "parallel", "arbitrary")),
        name="peer_final" if final else "peer",
    )(*args)


def _rope_tables(l, dim, width):
    t = jnp.arange(l)
    rows = (t // GRID_W).astype(F32)
    cols = (t % GRID_W).astype(F32)
    half = dim // 2
    inv = jnp.power(ROPE_BASE, -jnp.arange(0, half, 2, dtype=F32) / half)
    ang = jnp.concatenate([rows[:, None] * inv, cols[:, None] * inv], axis=-1)
    pair = (np.arange(width) % dim) // 2
    sign = np.where(np.arange(width) % 2 == 0, -1.0, 1.0).astype(np.float32)
    return jnp.cos(ang)[:, pair], jnp.sin(ang)[:, pair] * sign


def _swap_pairs(w):
    d, n = w.shape
    return w.reshape(d, n // 2, 2)[:, :, ::-1].reshape(d, n)


def _block_mixer_and_peer(x, ctx, mod3, layer, need_ctx, p, tables, final_g):
    b, l, d = x.shape
    lc = ctx.shape[1]
    row_x = lambda i: i
    row_c = lambda i: b
    lam_init = 0.8 - 0.6 * math.exp(-0.3 * layer)

    lat = _inproj(x, mod3, row_x, p["g1"], p["w_in"], tables, 256)
    cx = _inproj(ctx, mod3, row_c, p["g1"], p["w_in"], None, lc)
    naq, nak, nav, dfq, dfk, dfv, rtq, rtk, rtv, rtg = lat
    cnaq, cnak, cnav, cdfq, cdfk, cdfv, crtq, crtk, crtv, crtg = cx

    y_na = _na_latent(naq, nak, nav, cnak, cnav, p["rpb"], 4)
    y_df = _diff(dfq, cdfk, cdfv, dfk, dfv, p["lam"], p["subln"], lam_init, 256)
    y_rt, yc_rt = _retention(p["decay"], crtq, crtk, crtv, crtg, rtq, rtk, rtv, rtg)

    def channel_mix(xx, yna, ydf, yrt, row, tm, fg):
        xo, h2, st = _outproj(xx, yna, ydf, yrt, mod3, row, p["w_out"], p["g2"], p["wq"], p["sk"], tm)
        th, e1, e2 = _topk(st, min(tm, 256))
        return _peer(h2, p["u"], p["vt"], th, e1, e2, xo, mod3, row, fg, tm)

    x = channel_mix(x, y_na, y_df, y_rt, row_x, min(l, 512), final_g)
    if need_ctx:
        yc_na = _na_ctx(cnaq, cnak, cnav)
        yc_df = _diff(cdfq, cdfk, cdfv, None, None, p["lam"], p["subln"], lam_init, lc)
        ctx = channel_mix(ctx, yc_na, yc_df, yc_rt, row_c, lc, None)
    return x, ctx


def kernel(x, c, ctx, c_ctx, w_ada, b_ada, norm1_g, w_in, na_rpb, diff_lambda, diff_subln_g,
           ret_decay_logit, w_out, norm2_g, peer_wq, peer_subkeys, peer_u, peer_v, final_g):
    b, l, d = x.shape
    depth = w_ada.shape[0]
    x = x.astype(F32)
    ctx = ctx.astype(F32)
    npad = -(-(b + 1) // 8) * 8
    cpad = jnp.zeros((npad, d), F32).at[:b].set(c.astype(F32)).at[b].set(c_ctx.astype(F32))
    mod = _ada(cpad, w_ada.astype(F32), b_ada.astype(F32))
    tables = _rope_tables(l, DIFF_DIM, DIFF_QK_W) + _rope_tables(l, RET_QK, RET_QK_W)
    for layer in range(depth):
        wi = w_in[layer]
        w_ext = jnp.concatenate(
            [wi] + [_swap_pairs(wi[:, c0:c0 + 256]) for c0 in (C_DFQ, C_DFK, C_RTQ, C_RTK)], axis=1)
        p = {
            "g1": norm1_g[layer].astype(F32).reshape(1, d),
            "g2": norm2_g[layer].astype(F32).reshape(1, d),
            "w_in": w_ext.astype(BF16),
            "rpb": na_rpb[layer],
            "lam": diff_lambda[layer].astype(F32),
            "subln": jnp.tile(diff_subln_g[layer].astype(F32), DIFF_HEADS).reshape(1, DIFF_W),
            "decay": ret_decay_logit[layer],
            "w_out": w_out[layer].astype(BF16),
            "wq": peer_wq[layer].astype(BF16),
            "sk": peer_subkeys[layer].astype(BF16).reshape(2 * PEER_HEADS, PEER_NKEYS, PEER_HALF),
            "u": peer_u[layer].astype(BF16),
            "vt": peer_v[layer].astype(BF16).T,
        }
        last = layer == depth - 1
        x, ctx = _block_mixer_and_peer(x, ctx, mod[layer].reshape(npad, 1, 6 * d), layer, not last, p,
                                       tables, final_g.astype(F32).reshape(1, d) if last else None)
    return x
```

```python
import functools
import math

import numpy as np
import jax
import jax.numpy as jnp
from jax import lax
from jax.experimental import pallas as pl
from jax.experimental.pallas import tpu as pltpu

F32 = jnp.float32
BF16 = jnp.bfloat16

GRID_W = 64
NORM_EPS = 1e-6
ROPE_BASE = 10000.0
NEG_INF = -1e30
LOG2E = math.log2(math.e)

NA_HEADS = 4
NA_DIM = 64
NA_WIN_ROWS = 8
NA_WIN_COLS = 16
DIFF_HEADS = 4
DIFF_DIM = 32
DIFF_VDIM = 64
RET_HEADS = 4
RET_QK = 64
RET_V = 128
RET_CHUNK = 128
RET_UNROLL = 8

NA_W = NA_HEADS * NA_DIM
DIFF_QK_W = DIFF_HEADS * 2 * DIFF_DIM
DIFF_W = DIFF_HEADS * DIFF_VDIM
RET_QK_W = RET_HEADS * RET_QK
RET_W = RET_HEADS * RET_V
MIX_W = NA_W + DIFF_W + RET_W
IN_COLS = 3 * NA_W + 2 * DIFF_QK_W + DIFF_W + 2 * RET_QK_W + 2 * RET_W

PEER_HEADS = 8
PEER_NKEYS = 128
PEER_KDIM = 256
PEER_TOPK = 16
PEER_HALF = PEER_KDIM // 2

C_NAQ, C_NAK, C_NAV = 0, 256, 512
C_DFQ, C_DFK, C_DFV = 768, 1024, 1280
C_RTQ, C_RTK, C_RTV, C_RTG = 1536, 1792, 2048, 2560
C_DFQ_S, C_DFK_S, C_RTQ_S, C_RTK_S = 3072, 3328, 3584, 3840
IN_COLS_EXT = 4096

PEER_PAIRS = tuple((a, b) for a in range(PEER_TOPK) for b in range(PEER_TOPK // (a + 1)))
PEER_NCAND = 64

VMEM_LIMIT = 56 * 1024 * 1024


def _cparams(sem):
    return pltpu.CompilerParams(dimension_semantics=sem, vmem_limit_bytes=VMEM_LIMIT)


def _nt(a, b):
    return lax.dot_general(a, b, (((1,), (1,)), ((), ())), preferred_element_type=F32)


def _tn(a, b):
    return lax.dot_general(a, b, (((0,), (0,)), ((), ())), preferred_element_type=F32)


def _dot(a, b):
    return jnp.dot(a, b, preferred_element_type=F32)


def _rms(x):
    return x * lax.rsqrt(jnp.mean(x * x, axis=-1, keepdims=True) + NORM_EPS)


def _silu(x):
    return x * jax.nn.sigmoid(x)


def _ada_kernel(c_ref, w_ref, b_ref, o_ref):
    s = _silu(c_ref[...])
    o_ref[0] = jnp.dot(s, w_ref[0], preferred_element_type=F32,
                       precision=lax.Precision.HIGHEST) + b_ref[0]


def _ada(cpad, w_ada, b_ada):
    depth, d, n = w_ada.shape
    tn = 1536
    return pl.pallas_call(
        _ada_kernel,
        grid=(depth, n // tn),
        in_specs=[pl.BlockSpec((cpad.shape[0], d), lambda l, j: (0, 0)),
                  pl.BlockSpec((1, d, tn), lambda l, j: (l, 0, j)),
                  pl.BlockSpec((1, 1, tn), lambda l, j: (l, 0, j))],
        out_specs=pl.BlockSpec((1, cpad.shape[0], tn), lambda l, j: (l, 0, j)),
        out_shape=jax.ShapeDtypeStruct((depth, cpad.shape[0], n), F32),
        compiler_params=_cparams(("parallel", "parallel")),
        name="ada",
    )(cpad, w_ada, b_ada.reshape(depth, 1, n))


def _inproj_kernel(*refs, rope, d):
    if rope:
        (x_ref, mod_ref, g_ref, w_ref, cd_ref, sd_ref, cr_ref, sr_ref,
         naq, nak, nav, dfq, dfk, dfv, rtq, rtk, rtv, rtg) = refs
    else:
        (x_ref, mod_ref, g_ref, w_ref,
         naq, nak, nav, dfq, dfk, dfv, rtq, rtk, rtv, rtg) = refs
    mod = mod_ref[0]
    h = _rms(x_ref[0]) * g_ref[...] * (1.0 + mod[:, d:2 * d]) + mod[:, 0:d]
    hb = h.astype(BF16)

    def proj(c0, n):
        return _dot(hb, w_ref[:, c0:c0 + n])

    def roped(c0, c0s, cos_ref, sin_ref):
        if rope:
            return proj(c0, 256) * cos_ref[...] + proj(c0s, 256) * sin_ref[...]
        return proj(c0, 256)

    naq[0] = (proj(C_NAQ, 256) * (NA_DIM ** -0.5 * LOG2E)).astype(BF16)
    nak[0] = proj(C_NAK, 256).astype(BF16)
    nav[0] = proj(C_NAV, 256).astype(BF16)
    dfq[0] = (roped(C_DFQ, C_DFQ_S, cd_ref if rope else None, sd_ref if rope else None)
              * (DIFF_DIM ** -0.5 * LOG2E)).astype(BF16)
    dfk[0] = roped(C_DFK, C_DFK_S, cd_ref if rope else None, sd_ref if rope else None).astype(BF16)
    dfv[0] = proj(C_DFV, 256).astype(BF16)
    rtq[0] = roped(C_RTQ, C_RTQ_S, cr_ref if rope else None, sr_ref if rope else None).astype(BF16)
    rtk[0] = (roped(C_RTK, C_RTK_S, cr_ref if rope else None, sr_ref if rope else None)
              * RET_QK ** -0.5).astype(BF16)
    rtv[0] = proj(C_RTV, 512).astype(BF16)
    rtg[0] = proj(C_RTG, 512)


def _inproj(x, mod3, mod_row, g1, w_ext, tables, tm):
    b, l, d = x.shape
    rope = tables is not None
    ncols = IN_COLS_EXT if rope else IN_COLS
    in_specs = [pl.BlockSpec((1, tm, d), lambda i, t: (i, t, 0)),
                pl.BlockSpec((1, 1, 6 * d), lambda i, t: (mod_row(i), 0, 0)),
                pl.BlockSpec((1, d), lambda i, t: (0, 0)),
                pl.BlockSpec((d, ncols), lambda i, t: (0, 0))]
    args = [x, mod3, g1, w_ext]
    if rope:
        in_specs += [pl.BlockSpec((tm, 256), lambda i, t: (t, 0))] * 4
        args += list(tables)
    widths = (256, 256, 256, 256, 256, 256, 256, 256, 512, 512)
    dtypes = (BF16,) * 9 + (F32,)
    out_specs = [pl.BlockSpec((1, tm, w), lambda i, t: (i, t, 0)) for w in widths]
    out_shape = [jax.ShapeDtypeStruct((b, l, w), dt) for w, dt in zip(widths, dtypes)]
    return pl.pallas_call(
        functools.partial(_inproj_kernel, rope=rope, d=d),
        grid=(b, l // tm),
        in_specs=in_specs, out_specs=out_specs, out_shape=out_shape,
        compiler_params=_cparams(("parallel", "parallel")),
        name="inproj_rope" if rope else "inproj_ctx",
    )(*args)


def _na_plan(rows, rb):
    wr = min(NA_WIN_ROWS, rows)
    kr = min(rb + wr - 1, rows)
    nblk = rows // rb
    win0 = np.clip(np.arange(rows) - wr // 2, 0, rows - wr)
    ks = np.clip(np.arange(nblk) * rb - wr // 2, 0, rows - kr)
    pats, pat_of = [], np.zeros(nblk, np.int32)
    for g in range(nblk):
        r = g * rb + np.arange(rb)
        krow = ks[g] + np.arange(kr)
        dr = krow[None, :] - r[:, None] + (NA_WIN_ROWS - 1)
        ok = (krow[None, :] >= win0[r][:, None]) & (krow[None, :] < win0[r][:, None] + wr)
        assert ok.sum(axis=1).min() == wr
        key = (np.where(ok, dr, 0).tobytes(), ok.tobytes())
        for p, (k2, _, _) in enumerate(pats):
            if k2 == key:
                pat_of[g] = p
                break
        else:
            pat_of[g] = len(pats)
            pats.append((key, np.where(ok, dr, 0), ok))
    dr_idx = np.stack([p[1] for p in pats])
    row_ok = np.stack([p[2] for p in pats])
    qcol = np.arange(GRID_W)
    kcol = np.arange(GRID_W)
    cs = np.clip(qcol - NA_WIN_COLS // 2, 0, GRID_W - NA_WIN_COLS)
    col_ok = (kcol[None, :] >= cs[:, None]) & (kcol[None, :] < cs[:, None] + NA_WIN_COLS)
    dc_idx = np.clip(kcol[None, :] - qcol[:, None], 1 - NA_WIN_COLS, NA_WIN_COLS - 1) + NA_WIN_COLS - 1
    return kr, ks.astype(np.int32), pat_of, dr_idx, row_ok, dc_idx, col_ok


def _na_bias(rpb, plan, rb):
    kr, _, _, dr_idx, row_ok, dc_idx, col_ok = plan
    p = dr_idx.shape[0]
    nr, nc = 2 * NA_WIN_ROWS - 1, 2 * NA_WIN_COLS - 1
    row_hot = ((np.arange(nr) == dr_idx[..., None]) & row_ok[..., None]).astype(np.float32)
    col_hot = ((np.arange(nc) == dc_idx[..., None]) & col_ok[..., None]).astype(np.float32)
    bias = jnp.einsum("hrc,pqwr,xyc->hpqxwy", rpb.astype(F32), row_hot, col_hot,
                      precision=lax.Precision.HIGHEST)
    ok = row_ok[:, :, None, :, None] & col_ok[None, None, :, None, :]
    bias = jnp.where(ok[None], bias * LOG2E, NEG_INF)
    return bias.reshape(NA_HEADS, p, rb * GRID_W, kr * GRID_W)


def _na_kernel(ks_ref, pat_ref, q_ref, kx_ref, vx_ref, kc_ref, vc_ref, bias_ref, o_ref, *, nkeys):
    g = pl.program_id(1)
    q = q_ref[0]
    k0 = pl.multiple_of(ks_ref[g] * GRID_W, GRID_W)
    kw = kx_ref[0, pl.ds(k0, nkeys), :]
    vw = vx_ref[0, pl.ds(k0, nkeys), :]
    kc = kc_ref[0]
    vc = vc_ref[0]
    pat = pat_ref[g]
    lane = lax.broadcasted_iota(jnp.int32, (1, NA_W), 1)
    out = jnp.zeros((q.shape[0], NA_W), F32)
    for h in range(NA_HEADS):
        hm = (lane >= NA_DIM * h) & (lane < NA_DIM * (h + 1))
        qh = jnp.where(hm, q, jnp.zeros_like(q))
        sw = _nt(qh, kw) + bias_ref[h, pat]
        sc = _nt(qh, kc)
        m = jnp.maximum(jnp.max(sw, axis=-1, keepdims=True), jnp.max(sc, axis=-1, keepdims=True))
        pw = jnp.exp2(sw - m)
        pc = jnp.exp2(sc - m)
        den = jnp.sum(pw, axis=-1, keepdims=True) + jnp.sum(pc, axis=-1, keepdims=True)
        o = _dot(pw.astype(BF16), vw) + _dot(pc.astype(BF16), vc)
        out = out + jnp.where(hm, o / den, 0.0)
    o_ref[0] = out.astype(o_ref.dtype)


def _na_latent(q, kx, vx, kc, vc, rpb, rb):
    b, l, _ = q.shape
    lc = kc.shape[1]
    rows = l // GRID_W
    plan = _na_plan(rows, rb)
    kr, ks, pat_of = plan[0], plan[1], plan[2]
    bias = _na_bias(rpb, plan, rb)
    tq, nkeys = rb * GRID_W, kr * GRID_W
    grid_spec = pltpu.PrefetchScalarGridSpec(
        num_scalar_prefetch=2,
        grid=(b, rows // rb),
        in_specs=[pl.BlockSpec((1, tq, NA_W), lambda i, g, *_: (i, g, 0)),
                  pl.BlockSpec((1, l, NA_W), lambda i, g, *_: (i, 0, 0)),
                  pl.BlockSpec((1, l, NA_W), lambda i, g, *_: (i, 0, 0)),
                  pl.BlockSpec((1, lc, NA_W), lambda i, g, *_: (i, 0, 0)),
                  pl.BlockSpec((1, lc, NA_W), lambda i, g, *_: (i, 0, 0)),
                  pl.BlockSpec(bias.shape, lambda i, g, *_: (0, 0, 0, 0))],
        out_specs=pl.BlockSpec((1, tq, NA_W), lambda i, g, *_: (i, g, 0)),
    )
    return pl.pallas_call(
        functools.partial(_na_kernel, nkeys=nkeys),
        grid_spec=grid_spec,
        out_shape=jax.ShapeDtypeStruct((b, l, NA_W), BF16),
        compiler_params=_cparams(("parallel", "arbitrary")),
        name="na_latent",
    )(jnp.asarray(ks), jnp.asarray(pat_of), q, kx, vx, kc, vc, bias)


def _na_ctx_kernel(q_ref, kc_ref, vc_ref, o_ref):
    q = q_ref[0]
    kc = kc_ref[0]
    vc = vc_ref[0]
    lane = lax.broadcasted_iota(jnp.int32, (1, NA_W), 1)
    out = jnp.zeros((q.shape[0], NA_W), F32)
    for h in range(NA_HEADS):
        hm = (lane >= NA_DIM * h) & (lane < NA_DIM * (h + 1))
        qh = jnp.where(hm, q, jnp.zeros_like(q))
        sc = _nt(qh, kc)
        pc = jnp.exp2(sc - jnp.max(sc, axis=-1, keepdims=True))
        o = _dot(pc.astype(BF16), vc)
        out = out + jnp.where(hm, o / jnp.sum(pc, axis=-1, keepdims=True), 0.0)
    o_ref[0] = out.astype(o_ref.dtype)


def _na_ctx(q, kc, vc):
    b, lc, _ = q.shape
    spec = pl.BlockSpec((1, lc, NA_W), lambda i: (i, 0, 0))
    return pl.pallas_call(
        _na_ctx_kernel, grid=(b,), in_specs=[spec, spec, spec], out_specs=spec,
        out_shape=jax.ShapeDtypeStruct((b, lc, NA_W), BF16),
        compiler_params=_cparams(("parallel",)), name="na_ctx",
    )(q, kc, vc)


def _diff_kernel(*refs, lam_init, has_x):
    if has_x:
        q_ref, kc_ref, vc_ref, kx_ref, vx_ref, lam_ref, g_ref, o_ref = refs
    else:
        q_ref, kc_ref, vc_ref, lam_ref, g_ref, o_ref = refs
    q = q_ref[0]
    kc = kc_ref[0]
    vc = vc_ref[0]
    lp = lam_ref[...]
    lam = (jnp.exp(jnp.sum(lp[0:1] * lp[1:2], axis=-1, keepdims=True))
           - jnp.exp(jnp.sum(lp[2:3] * lp[3:4], axis=-1, keepdims=True)) + lam_init)
    lane = lax.broadcasted_iota(jnp.int32, (1, DIFF_W), 1)
    out = jnp.zeros((q.shape[0], DIFF_W), F32)
    def scores(h):
        res = []
        for c in range(2):
            lo = DIFF_VDIM * h + DIFF_DIM * c
            qm = jnp.where((lane >= lo) & (lane < lo + DIFF_DIM), q, jnp.zeros_like(q))
            res.append((_nt(qm, kc), _nt(qm, kx_ref[0]) if has_x else None))
        return res

    nxt = scores(0)
    for h in range(DIFF_HEADS):
        cur = nxt
        if h + 1 < DIFF_HEADS:
            nxt = scores(h + 1)
        parts = []
        for c in range(2):
            sc, sx = cur[c]
            m = jnp.max(sc, axis=-1, keepdims=True)
            if has_x:
                m = jnp.maximum(m, jnp.max(sx, axis=-1, keepdims=True))
            ec = jnp.exp2(sc - m)
            den = jnp.sum(ec, axis=-1, keepdims=True)
            ex = None
            if has_x:
                ex = jnp.exp2(sx - m)
                den = den + jnp.sum(ex, axis=-1, keepdims=True)
            parts.append((ec, ex, den))
        (ec0, ex0, den0), (ec1, ex1, den1) = parts
        rho = lam * den0 / den1
        o = _dot((ec0 - ec1 * rho).astype(BF16), vc)
        if has_x:
            o = o + _dot((ex0 - ex1 * rho).astype(BF16), vx_ref[0])
        hm = (lane >= DIFF_VDIM * h) & (lane < DIFF_VDIM * (h + 1))
        oh = jnp.where(hm, o / den0, 0.0)
        ms = jnp.sum(oh * oh, axis=-1, keepdims=True) * (1.0 / DIFF_VDIM)
        out = out + oh * lax.rsqrt(ms + NORM_EPS)
    o_ref[0] = (out * g_ref[...] * (1.0 - lam_init)).astype(o_ref.dtype)


def _diff(q, kc, vc, kx, vx, lam_p, g4, lam_init, tq):
    b, l, _ = q.shape
    lc = kc.shape[1]
    has_x = kx is not None
    cspec = pl.BlockSpec((1, lc, DIFF_W), lambda i, t: (i, 0, 0))
    in_specs = [pl.BlockSpec((1, tq, DIFF_W), lambda i, t: (i, t, 0)), cspec, cspec]
    args = [q, kc, vc]
    if has_x:
        xspec = pl.BlockSpec((1, kx.shape[1], DIFF_W), lambda i, t: (i, 0, 0))
        in_specs += [xspec, xspec]
        args += [kx, vx]
    in_specs += [pl.BlockSpec(lam_p.shape, lambda i, t: (0, 0)),
                 pl.BlockSpec((1, DIFF_W), lambda i, t: (0, 0))]
    args += [lam_p, g4]
    return pl.pallas_call(
        functools.partial(_diff_kernel, lam_init=lam_init, has_x=has_x),
        grid=(b, l // tq), in_specs=in_specs,
        out_specs=pl.BlockSpec((1, tq, DIFF_W), lambda i, t: (i, t, 0)),
        out_shape=jax.ShapeDtypeStruct((b, l, DIFF_W), BF16),
        compiler_params=_cparams(("parallel", "arbitrary")),
        name="diff_latent" if has_x else "diff_ctx",
    )(*args)


def _ret_kernel(dl_ref, cq_ref, ck_ref, cv_ref, cg_ref, xq_ref, xk_ref, xv_ref, xg_ref,
                yx_ref, yc_ref, s_ref, *, n_ctx, n_lat):
    h = pl.program_id(1)
    c = RET_CHUNK
    lane = lax.broadcasted_iota(jnp.int32, (1, RET_QK_W), 1)
    hm = (lane >= RET_QK * h) & (lane < RET_QK * (h + 1))
    row = lax.broadcasted_iota(jnp.int32, (c, c), 0).astype(F32)
    col = lax.broadcasted_iota(jnp.int32, (c, c), 1).astype(F32)
    roww = lax.broadcasted_iota(jnp.int32, (c, RET_QK_W), 0).astype(F32)

    def log_sigmoid(d):
        x = jnp.full((1, 1), dl_ref[d, h], F32)
        return jnp.minimum(x, 0.0) - jnp.log1p(jnp.exp(-jnp.abs(x)))

    lgf, lgb = log_sigmoid(0), log_sigmoid(1)
    diff = row - col
    fwd = (jnp.where(diff >= 0, jnp.exp(lgf * jnp.maximum(diff, 0.0)), 0.0),
           jnp.exp(lgf * (roww + 1.0)),
           jnp.exp(lgf * (c - 1.0 - roww)),
           jnp.exp(lgf * c))
    bwd = (jnp.where(diff <= 0, jnp.exp(lgb * jnp.maximum(-diff, 0.0)), 0.0),
           jnp.exp(lgb * (c - roww)),
           jnp.exp(lgb * roww),
           jnp.exp(lgb * c))

    def step(q_ref, k_ref, v_ref, g_ref, out_ref, ci, mats, sdir, second):
        dm, qd, kd, cd = mats
        sl = pl.ds(pl.multiple_of(ci * c, c), c)
        q = q_ref[0, sl, :]
        q = jnp.where(hm, q, jnp.zeros_like(q))
        k = k_ref[0, sl, :]
        v = v_ref[0, sl, :]
        att = _nt(q, k) * dm
        s = s_ref[sdir]
        o = _dot(att.astype(BF16), v) + _dot((q.astype(F32) * qd).astype(BF16), s.astype(BF16))
        s_ref[sdir] = cd * s + _tn((k.astype(F32) * kd).astype(BF16), v)
        if second:
            tot = out_ref[0, sl, :] + o
            out_ref[0, sl, :] = _rms(tot) * _silu(g_ref[0, sl, :])
        else:
            out_ref[0, sl, :] = o

    def both(refs, n, i, second):
        step(*refs, i, fwd, 0, second)
        step(*refs, n - 1 - i, bwd, 1, second)

    s_ref[...] = jnp.zeros_like(s_ref)
    crefs = (cq_ref, ck_ref, cv_ref, cg_ref, yc_ref)
    xrefs = (xq_ref, xk_ref, xv_ref, xg_ref, yx_ref)
    for i in range(n_ctx):
        both(crefs, n_ctx, i, i >= n_ctx // 2)
    for second in (False, True):
        def body(i, carry, second=second):
            both(xrefs, n_lat, i, second)
            return carry

        lax.fori_loop(n_lat // 2 if second else 0, n_lat if second else n_lat // 2, body, 0,
                      unroll=RET_UNROLL if (n_lat // 2) % RET_UNROLL == 0 else 1)


def _retention(decay_logit, cq, ck, cv, cg, xq, xk, xv, xg):
    b, l, _ = xq.shape
    lc = cq.shape[1]

    def qk_spec(n):
        return pl.BlockSpec((1, n, RET_QK_W), lambda i, h: (i, 0, 0))

    def v_spec(n):
        return pl.BlockSpec((1, n, RET_V), lambda i, h: (i, 0, h))

    return pl.pallas_call(
        functools.partial(_ret_kernel, n_ctx=lc // RET_CHUNK, n_lat=l // RET_CHUNK),
        grid=(b, RET_HEADS),
        in_specs=[pl.BlockSpec(memory_space=pltpu.SMEM),
                  qk_spec(lc), qk_spec(lc), v_spec(lc), v_spec(lc),
                  qk_spec(l), qk_spec(l), v_spec(l), v_spec(l)],
        out_specs=[v_spec(l), v_spec(lc)],
        out_shape=[jax.ShapeDtypeStruct((b, l, RET_W), F32), jax.ShapeDtypeStruct((b, lc, RET_W), F32)],
        scratch_shapes=[pltpu.VMEM((2, RET_QK_W, RET_V), F32)],
        compiler_params=_cparams(("parallel", "arbitrary")),
        name="retention",
    )(decay_logit.astype(F32), cq, ck, cv, cg, xq, xk, xv, xg)


def _outproj_kernel(x_ref, yna_ref, ydf_ref, yrt_ref, mod_ref, wo_ref, g2_ref, wq_ref, sk_ref,
                    xo_ref, h2_ref, st_ref, *, d):
    y = (_dot(yna_ref[0], wo_ref[0:NA_W, :])
         + _dot(ydf_ref[0], wo_ref[NA_W:NA_W + DIFF_W, :])
         + _dot(yrt_ref[0].astype(BF16), wo_ref[NA_W + DIFF_W:MIX_W, :]))
    mod = mod_ref[0]
    x = x_ref[0] + mod[:, 2 * d:3 * d] * y
    xo_ref[0] = x
    h2 = (_rms(x) * g2_ref[...] * (1.0 + mod[:, 4 * d:5 * d]) + mod[:, 3 * d:4 * d]).astype(BF16)
    h2_ref[0] = h2
    qp = _dot(h2, wq_ref[...]).astype(BF16)
    for hp in range(2 * PEER_HEADS):
        st_ref[0, hp] = _nt(sk_ref[hp], qp[:, hp * PEER_HALF:(hp + 1) * PEER_HALF])


def _outproj(x, yna, ydf, yrt, mod3, mod_row, wo, g2, wq, sk, tm):
    b, l, d = x.shape
    nhp = 2 * PEER_HEADS

    def tok(w):
        return pl.BlockSpec((1, tm, w), lambda i, t: (i, t, 0))

    return pl.pallas_call(
        functools.partial(_outproj_kernel, d=d),
        grid=(b, l // tm),
        in_specs=[tok(d), tok(NA_W), tok(DIFF_W), tok(RET_W),
                  pl.BlockSpec((1, 1, 6 * d), lambda i, t: (mod_row(i), 0, 0)),
                  pl.BlockSpec(wo.shape, lambda i, t: (0, 0)),
                  pl.BlockSpec((1, d), lambda i, t: (0, 0)),
                  pl.BlockSpec(wq.shape, lambda i, t: (0, 0)),
                  pl.BlockSpec(sk.shape, lambda i, t: (0, 0, 0))],
        out_specs=[tok(d), tok(d),
                   pl.BlockSpec((1, nhp, PEER_NKEYS, tm), lambda i, t: (i, 0, 0, t))],
        out_shape=[jax.ShapeDtypeStruct((b, l, d), F32), jax.ShapeDtypeStruct((b, l, d), BF16),
                   jax.ShapeDtypeStruct((b, nhp, PEER_NKEYS, l), F32)],
        compiler_params=_cparams(("parallel", "parallel")),
        name="outproj",
    )(x, yna, ydf, yrt, mod3, wo, g2, wq, sk)


def _sort_network(n):
    pairs = []
    p = 1
    while p < n:
        k = p
        while k >= 1:
            for j in range(k % p, n - k, 2 * k):
                for i in range(min(k, n - j - k)):
                    if (i + j) // (2 * p) == (i + j + k) // (2 * p):
                        pairs.append((i + j, i + j + k))
            k //= 2
        p *= 2
    return pairs


def _topk_kernel(st_ref, th_ref, e1_ref, e2_ref, cand_ref, t2_ref):
    neg = -jnp.inf

    def blocks(x):
        return [x[r * 8:(r + 1) * 8, :] for r in range(x.shape[0] // 8)]

    def sort_columns(vs):
        vs = list(vs)
        for i, j in _sort_network(len(vs)):
            vs[i], vs[j] = jnp.maximum(vs[i], vs[j]), jnp.minimum(vs[i], vs[j])
        return vs

    def top_rows(x):
        vs = sort_columns(blocks(x))
        rows = []
        for k in range(PEER_TOPK):
            m = jnp.max(vs[0], axis=0, keepdims=True)
            rows.append(m)
            depth = min(len(vs), PEER_TOPK - 1 - k)
            if depth > 0:
                hit = vs[0] == m
                for r in range(depth):
                    below = vs[r + 1] if r + 1 < len(vs) else neg
                    vs[r] = jnp.where(hit, below, vs[r])
        return rows

    cand_ref[len(PEER_PAIRS):, :] = jnp.full((PEER_NCAND - len(PEER_PAIRS), cand_ref.shape[1]), neg, F32)
    for h in range(PEER_HEADS):
        s1 = st_ref[0, 2 * h]
        s2 = st_ref[0, 2 * h + 1]
        a1 = s1 - jnp.max(s1, axis=0, keepdims=True)
        a2 = s2 - jnp.max(s2, axis=0, keepdims=True)
        t1 = top_rows(a1)
        t2 = top_rows(a2)
        for b in range(PEER_TOPK):
            t2_ref[b:b + 1, :] = t2[b]
        t2s = t2_ref[...]
        off = 0
        for a in range(PEER_TOPK):
            n = PEER_TOPK // (a + 1)
            cand_ref[off:off + n, :] = (t1[a] + t2s)[0:n]
            off += n
        cand = cand_ref[...]
        tau = top_rows(cand)[-1]
        z = jnp.sum(jnp.where(cand >= tau, jnp.exp(cand), 0.0), axis=0, keepdims=True)
        th_ref[0, h] = jnp.exp(tau - a1)
        e1_ref[0, h] = jnp.exp(a1) * (0.5 / z)
        e2_ref[0, h] = jnp.exp(a2)


def _topk(st, tl):
    b, nhp, nk, l = st.shape
    hspec = pl.BlockSpec((1, PEER_HEADS, nk, tl), lambda i, t: (i, 0, 0, t))
    hshape = jax.ShapeDtypeStruct((b, PEER_HEADS, nk, l), F32)
    return pl.pallas_call(
        _topk_kernel,
        grid=(b, l // tl),
        in_specs=[pl.BlockSpec((1, nhp, nk, tl), lambda i, t: (i, 0, 0, t))],
        out_specs=[hspec, hspec, hspec],
        out_shape=[hshape, hshape, hshape],
        scratch_shapes=[pltpu.VMEM((PEER_NCAND, tl), F32), pltpu.VMEM((PEER_TOPK, tl), F32)],
        compiler_params=_cparams(("parallel", "parallel")),
        name="peer_topk",
    )(st)


def _peer_kernel(*refs, final, d, ich):
    if final:
        h2_ref, u_ref, vt_ref, th_ref, e1_ref, e2_ref, x_ref, mod_ref, fg_ref, o_ref = refs[:10]
    else:
        h2_ref, u_ref, vt_ref, th_ref, e1_ref, e2_ref, x_ref, mod_ref, o_ref = refs[:9]
    acc_ref, at_ref, gw_ref, h2t_ref = refs[-4:]
    c = pl.program_id(2)
    tm = h2_ref.shape[1]
    nt = PEER_NKEYS // 8

    @pl.when(c == 0)
    def _():
        acc_ref[...] = jnp.zeros_like(acc_ref)
        h2t_ref[...] = h2_ref[0].T

    at_ref[...] = _dot(u_ref[...], h2t_ref[...])
    for ii in range(ich):
        for lg in range(tm // 128):
            ls = slice(lg * 128, (lg + 1) * 128)
            w = [None] * nt
            for h in range(PEER_HEADS):
                th = jnp.broadcast_to(th_ref[0, h, ii:ii + 1, ls], (8, 128))
                e1 = jnp.broadcast_to(e1_ref[0, h, ii:ii + 1, ls], (8, 128))
                for jt in range(nt):
                    e2 = e2_ref[0, h, jt * 8:(jt + 1) * 8, ls]
                    term = jnp.where(e2 >= th, e2, 0.0) * e1
                    w[jt] = term if w[jt] is None else w[jt] + term
            for jt in range(0, nt, 2):
                r0 = ii * PEER_NKEYS + jt * 8
                a = at_ref[r0:r0 + 16, ls]
                g = a * (1.0 + lax.erf(a * math.sqrt(0.5))) * jnp.concatenate(w[jt:jt + 2], axis=0)
                gw_ref[r0:r0 + 16, ls] = g.astype(BF16)
    acc_ref[...] += _dot(vt_ref[...], gw_ref[...])

    @pl.when(c == pl.num_programs(2) - 1)
    def _():
        mod = mod_ref[0]
        xo = x_ref[0] + mod[:, 5 * d:6 * d] * acc_ref[...].T
        if final:
            xo = _rms(xo) * fg_ref[...]
        o_ref[0] = xo


def _peer(h2, u, vt, th, e1, e2, x, mod3, mod_row, final_g, tm, ich=8):
    b, l, d = x.shape
    ne = u.shape[0]
    ec = ich * PEER_NKEYS
    final = final_g is not None
    ispec = pl.BlockSpec((1, PEER_HEADS, ich, tm), lambda i, t, c: (i, 0, c, t))
    in_specs = [pl.BlockSpec((1, tm, d), lambda i, t, c: (i, t, 0)),
                pl.BlockSpec((ec, d), lambda i, t, c: (c, 0)),
                pl.BlockSpec((d, ec), lambda i, t, c: (0, c)),
                ispec, ispec,
                pl.BlockSpec((1, PEER_HEADS, PEER_NKEYS, tm), lambda i, t, c: (i, 0, 0, t)),
                pl.BlockSpec((1, tm, d), lambda i, t, c: (i, t, 0)),
                pl.BlockSpec((1, 1, 6 * d), lambda i, t, c: (mod_row(i), 0, 0))]
    args = [h2, u, vt, th, e1, e2, x, mod3]
    if final:
        in_specs.append(pl.BlockSpec((1, d), lambda i, t, c: (0, 0)))
        args.append(final_g)
    return pl.pallas_call(
        functools.partial(_peer_kernel, final=final, d=d, ich=ich),
        grid=(b, l // tm, ne // ec),
        in_specs=in_specs,
        out_specs=pl.BlockSpec((1, tm, d), lambda i, t, c: (i, t, 0)),
        out_shape=jax.ShapeDtypeStruct((b, l, d), F32),
        scratch_shapes=[pltpu.VMEM((d, tm), F32), pltpu.VMEM((ec, tm), F32),
                        pltpu.VMEM((ec, tm), BF16), pltpu.VMEM((d, tm), BF16)],
        compiler_params=_cparams(("parallel", "parallel", "arbitrary")),
        name="peer_final" if final else "peer",
    )(*args)


def _rope_tables(l, dim, width):
    t = jnp.arange(l)
    rows = (t // GRID_W).astype(F32)
    cols = (t % GRID_W).astype(F32)
    half = dim // 2
    inv = jnp.power(ROPE_BASE, -jnp.arange(0, half, 2, dtype=F32) / half)
    ang = jnp.concatenate([rows[:, None] * inv, cols[:, None] * inv], axis=-1)
    pair = (np.arange(width) % dim) // 2
    sign = np.where(np.arange(width) % 2 == 0, -1.0, 1.0).astype(np.float32)
    return jnp.cos(ang)[:, pair], jnp.sin(ang)[:, pair] * sign


def _swap_pairs(w):
    d, n = w.shape
    return w.reshape(d, n // 2, 2)[:, :, ::-1].reshape(d, n)


def _block_mixer_and_peer(x, ctx, mod3, layer, need_ctx, p, tables, final_g):
    b, l, d = x.shape
    lc = ctx.shape[1]
    row_x = lambda i: i
    row_c = lambda i: b
    lam_init = 0.8 - 0.6 * math.exp(-0.3 * layer)

    lat = _inproj(x, mod3, row_x, p["g1"], p["w_in"], tables, min(l, 512))
    cx = _inproj(ctx, mod3, row_c, p["g1"], p["w_in"], None, lc)
    naq, nak, nav, dfq, dfk, dfv, rtq, rtk, rtv, rtg = lat
    cnaq, cnak, cnav, cdfq, cdfk, cdfv, crtq, crtk, crtv, crtg = cx

    y_na = _na_latent(naq, nak, nav, cnak, cnav, p["rpb"], 4)
    y_df = _diff(dfq, cdfk, cdfv, dfk, dfv, p["lam"], p["subln"], lam_init, 256)
    y_rt, yc_rt = _retention(p["decay"], crtq, crtk, crtv, crtg, rtq, rtk, rtv, rtg)

    def channel_mix(xx, yna, ydf, yrt, row, tm, fg):
        xo, h2, st = _outproj(xx, yna, ydf, yrt, mod3, row, p["w_out"], p["g2"], p["wq"], p["sk"], tm)
        th, e1, e2 = _topk(st, min(tm, 256))
        return _peer(h2, p["u"], p["vt"], th, e1, e2, xo, mod3, row, fg, tm)

    x = channel_mix(x, y_na, y_df, y_rt, row_x, min(l, 512), final_g)
    if need_ctx:
        yc_na = _na_ctx(cnaq, cnak, cnav)
        yc_df = _diff(cdfq, cdfk, cdfv, None, None, p["lam"], p["subln"], lam_init, lc)
        flat = lambda t: t.reshape(1, b * lc, t.shape[-1])
        ctx = channel_mix(flat(ctx), flat(yc_na), flat(yc_df), flat(yc_rt), lambda i: b,
                          min(b * lc, 512), None).reshape(b, lc, d)
    return x, ctx


def kernel(x, c, ctx, c_ctx, w_ada, b_ada, norm1_g, w_in, na_rpb, diff_lambda, diff_subln_g,
           ret_decay_logit, w_out, norm2_g, peer_wq, peer_subkeys, peer_u, peer_v, final_g):
    b, l, d = x.shape
    depth = w_ada.shape[0]
    x = x.astype(F32)
    ctx = ctx.astype(F32)
    npad = -(-(b + 1) // 8) * 8
    cpad = jnp.zeros((npad, d), F32).at[:b].set(c.astype(F32)).at[b].set(c_ctx.astype(F32))
    mod = _ada(cpad, w_ada.astype(F32), b_ada.astype(F32))
    tables = _rope_tables(l, DIFF_DIM, DIFF_QK_W) + _rope_tables(l, RET_QK, RET_QK_W)
    for layer in range(depth):
        wi = w_in[layer]
        w_ext = jnp.concatenate(
            [wi] + [_swap_pairs(wi[:, c0:c0 + 256]) for c0 in (C_DFQ, C_DFK, C_RTQ, C_RTK)], axis=1)
        p = {
            "g1": norm1_g[layer].astype(F32).reshape(1, d),
            "g2": norm2_g[layer].astype(F32).reshape(1, d),
            "w_in": w_ext.astype(BF16),
            "rpb": na_rpb[layer],
            "lam": diff_lambda[layer].astype(F32),
            "subln": jnp.tile(diff_subln_g[layer].astype(F32), DIFF_HEADS).reshape(1, DIFF_W),
            "decay": ret_decay_logit[layer],
            "w_out": w_out[layer].astype(BF16),
            "wq": peer_wq[layer].astype(BF16),
            "sk": peer_subkeys[layer].astype(BF16).reshape(2 * PEER_HEADS, PEER_NKEYS, PEER_HALF),
            "u": peer_u[layer].astype(BF16),
            "vt": peer_v[layer].astype(BF16).T,
        }
        last = layer == depth - 1
        x, ctx = _block_mixer_and_peer(x, ctx, mod[layer].reshape(npad, 1, 6 * d), layer, not last, p,
                                       tables, final_g.astype(F32).reshape(1, d) if last else None)
    return x
```

```python
import functools
import math

import numpy as np
import jax
import jax.numpy as jnp
from jax import lax
from jax.experimental import pallas as pl
from jax.experimental.pallas import tpu as pltpu

F32 = jnp.float32
BF16 = jnp.bfloat16

GRID_W = 64
NORM_EPS = 1e-6
ROPE_BASE = 10000.0
NEG_INF = -1e30
LOG2E = math.log2(math.e)

NA_HEADS = 4
NA_DIM = 64
NA_WIN_ROWS = 8
NA_WIN_COLS = 16
DIFF_HEADS = 4
DIFF_DIM = 32
DIFF_VDIM = 64
RET_HEADS = 4
RET_QK = 64
RET_V = 128
RET_CHUNK = 128
RET_UNROLL = 8

NA_W = NA_HEADS * NA_DIM
DIFF_QK_W = DIFF_HEADS * 2 * DIFF_DIM
DIFF_W = DIFF_HEADS * DIFF_VDIM
RET_QK_W = RET_HEADS * RET_QK
RET_W = RET_HEADS * RET_V
MIX_W = NA_W + DIFF_W + RET_W
IN_COLS = 3 * NA_W + 2 * DIFF_QK_W + DIFF_W + 2 * RET_QK_W + 2 * RET_W

PEER_HEADS = 8
PEER_NKEYS = 128
PEER_KDIM = 256
PEER_TOPK = 16
PEER_HALF = PEER_KDIM // 2

C_NAQ, C_NAK, C_NAV = 0, 256, 512
C_DFQ, C_DFK, C_DFV = 768, 1024, 1280
C_RTQ, C_RTK, C_RTV, C_RTG = 1536, 1792, 2048, 2560
C_DFQ_S, C_DFK_S, C_RTQ_S, C_RTK_S = 3072, 3328, 3584, 3840
IN_COLS_EXT = 4096

PEER_PAIRS = tuple((a, b) for a in range(PEER_TOPK) for b in range(PEER_TOPK // (a + 1)))
PEER_NCAND = 64

VMEM_LIMIT = 56 * 1024 * 1024


def _cparams(sem):
    return pltpu.CompilerParams(dimension_semantics=sem, vmem_limit_bytes=VMEM_LIMIT)


def _nt(a, b):
    return lax.dot_general(a, b, (((1,), (1,)), ((), ())), preferred_element_type=F32)


def _tn(a, b):
    return lax.dot_general(a, b, (((0,), (0,)), ((), ())), preferred_element_type=F32)


def _dot(a, b):
    return jnp.dot(a, b, preferred_element_type=F32)


def _rms(x):
    return x * lax.rsqrt(jnp.mean(x * x, axis=-1, keepdims=True) + NORM_EPS)


def _silu(x):
    return x * jax.nn.sigmoid(x)


def _ada_kernel(c_ref, w_ref, b_ref, o_ref):
    s = _silu(c_ref[...])
    o_ref[0] = jnp.dot(s, w_ref[0], preferred_element_type=F32,
                       precision=lax.Precision.HIGHEST) + b_ref[0]


def _ada(cpad, w_ada, b_ada):
    depth, d, n = w_ada.shape
    tn = 1536
    return pl.pallas_call(
        _ada_kernel,
        grid=(depth, n // tn),
        in_specs=[pl.BlockSpec((cpad.shape[0], d), lambda l, j: (0, 0)),
                  pl.BlockSpec((1, d, tn), lambda l, j: (l, 0, j)),
                  pl.BlockSpec((1, 1, tn), lambda l, j: (l, 0, j))],
        out_specs=pl.BlockSpec((1, cpad.shape[0], tn), lambda l, j: (l, 0, j)),
        out_shape=jax.ShapeDtypeStruct((depth, cpad.shape[0], n), F32),
        compiler_params=_cparams(("parallel", "parallel")),
        name="ada",
    )(cpad, w_ada, b_ada.reshape(depth, 1, n))


def _inproj_kernel(*refs, rope, d):
    if rope:
        (x_ref, mod_ref, g_ref, w_ref, cd_ref, sd_ref, cr_ref, sr_ref,
         naq, nak, nav, dfq, dfk, dfv, rtq, rtk, rtv, rtg) = refs
    else:
        (x_ref, mod_ref, g_ref, w_ref,
         naq, nak, nav, dfq, dfk, dfv, rtq, rtk, rtv, rtg) = refs
    mod = mod_ref[0]
    h = _rms(x_ref[0]) * g_ref[...] * (1.0 + mod[:, d:2 * d]) + mod[:, 0:d]
    hb = h.astype(BF16)

    def proj(c0, n):
        return _dot(hb, w_ref[:, c0:c0 + n])

    def roped(c0, c0s, cos_ref, sin_ref):
        if rope:
            return proj(c0, 256) * cos_ref[...] + proj(c0s, 256) * sin_ref[...]
        return proj(c0, 256)

    naq[0] = (proj(C_NAQ, 256) * (NA_DIM ** -0.5 * LOG2E)).astype(BF16)
    nak[0] = proj(C_NAK, 256).astype(BF16)
    nav[0] = proj(C_NAV, 256).astype(BF16)
    dfq[0] = (roped(C_DFQ, C_DFQ_S, cd_ref if rope else None, sd_ref if rope else None)
              * (DIFF_DIM ** -0.5 * LOG2E)).astype(BF16)
    dfk[0] = roped(C_DFK, C_DFK_S, cd_ref if rope else None, sd_ref if rope else None).astype(BF16)
    dfv[0] = proj(C_DFV, 256).astype(BF16)
    rtq[0] = roped(C_RTQ, C_RTQ_S, cr_ref if rope else None, sr_ref if rope else None).astype(BF16)
    rtk[0] = (roped(C_RTK, C_RTK_S, cr_ref if rope else None, sr_ref if rope else None)
              * RET_QK ** -0.5).astype(BF16)
    rtv[0] = proj(C_RTV, 512).astype(BF16)
    rtg[0] = proj(C_RTG, 512)


def _inproj(x, mod3, mod_row, g1, w_ext, tables, tm):
    b, l, d = x.shape
    rope = tables is not None
    ncols = IN_COLS_EXT if rope else IN_COLS
    in_specs = [pl.BlockSpec((1, tm, d), lambda i, t: (i, t, 0)),
                pl.BlockSpec((1, 1, 6 * d), lambda i, t: (mod_row(i), 0, 0)),
                pl.BlockSpec((1, d), lambda i, t: (0, 0)),
                pl.BlockSpec((d, ncols), lambda i, t: (0, 0))]
    args = [x, mod3, g1, w_ext]
    if rope:
        in_specs += [pl.BlockSpec((tm, 256), lambda i, t: (t, 0))] * 4
        args += list(tables)
    widths = (256, 256, 256, 256, 256, 256, 256, 256, 512, 512)
    dtypes = (BF16,) * 9 + (F32,)
    out_specs = [pl.BlockSpec((1, tm, w), lambda i, t: (i, t, 0)) for w in widths]
    out_shape = [jax.ShapeDtypeStruct((b, l, w), dt) for w, dt in zip(widths, dtypes)]
    return pl.pallas_call(
        functools.partial(_inproj_kernel, rope=rope, d=d),
        grid=(b, l // tm),
        in_specs=in_specs, out_specs=out_specs, out_shape=out_shape,
        compiler_params=_cparams(("parallel", "parallel")),
        name="inproj_rope" if rope else "inproj_ctx",
    )(*args)


def _na_plan(rows, rb):
    wr = min(NA_WIN_ROWS, rows)
    kr = min(rb + wr - 1, rows)
    nblk = rows // rb
    win0 = np.clip(np.arange(rows) - wr // 2, 0, rows - wr)
    ks = np.clip(np.arange(nblk) * rb - wr // 2, 0, rows - kr)
    pats, pat_of = [], np.zeros(nblk, np.int32)
    for g in range(nblk):
        r = g * rb + np.arange(rb)
        krow = ks[g] + np.arange(kr)
        dr = krow[None, :] - r[:, None] + (NA_WIN_ROWS - 1)
        ok = (krow[None, :] >= win0[r][:, None]) & (krow[None, :] < win0[r][:, None] + wr)
        assert ok.sum(axis=1).min() == wr
        key = (np.where(ok, dr, 0).tobytes(), ok.tobytes())
        for p, (k2, _, _) in enumerate(pats):
            if k2 == key:
                pat_of[g] = p
                break
        else:
            pat_of[g] = len(pats)
            pats.append((key, np.where(ok, dr, 0), ok))
    dr_idx = np.stack([p[1] for p in pats])
    row_ok = np.stack([p[2] for p in pats])
    qcol = np.arange(GRID_W)
    kcol = np.arange(GRID_W)
    cs = np.clip(qcol - NA_WIN_COLS // 2, 0, GRID_W - NA_WIN_COLS)
    col_ok = (kcol[None, :] >= cs[:, None]) & (kcol[None, :] < cs[:, None] + NA_WIN_COLS)
    dc_idx = np.clip(kcol[None, :] - qcol[:, None], 1 - NA_WIN_COLS, NA_WIN_COLS - 1) + NA_WIN_COLS - 1
    return kr, ks.astype(np.int32), pat_of, dr_idx, row_ok, dc_idx, col_ok


def _na_bias(rpb, plan, rb):
    kr, _, _, dr_idx, row_ok, dc_idx, col_ok = plan
    p = dr_idx.shape[0]
    nr, nc = 2 * NA_WIN_ROWS - 1, 2 * NA_WIN_COLS - 1
    row_hot = ((np.arange(nr) == dr_idx[..., None]) & row_ok[..., None]).astype(np.float32)
    col_hot = ((np.arange(nc) == dc_idx[..., None]) & col_ok[..., None]).astype(np.float32)
    bias = jnp.einsum("hrc,pqwr,xyc->hpqxwy", rpb.astype(F32), row_hot, col_hot,
                      precision=lax.Precision.HIGHEST)
    ok = row_ok[:, :, None, :, None] & col_ok[None, None, :, None, :]
    bias = jnp.where(ok[None], bias * LOG2E, NEG_INF)
    return bias.reshape(NA_HEADS, p, rb * GRID_W, kr * GRID_W)


def _na_kernel(ks_ref, pat_ref, q_ref, kx_ref, vx_ref, kc_ref, vc_ref, bias_ref, o_ref, *, nkeys):
    g = pl.program_id(1)
    q = q_ref[0]
    k0 = pl.multiple_of(ks_ref[g] * GRID_W, GRID_W)
    kw = kx_ref[0, pl.ds(k0, nkeys), :]
    vw = vx_ref[0, pl.ds(k0, nkeys), :]
    kc = kc_ref[0]
    vc = vc_ref[0]
    pat = pat_ref[g]
    lane = lax.broadcasted_iota(jnp.int32, (1, NA_W), 1)
    out = jnp.zeros((q.shape[0], NA_W), F32)
    for h in range(NA_HEADS):
        hm = (lane >= NA_DIM * h) & (lane < NA_DIM * (h + 1))
        qh = jnp.where(hm, q, jnp.zeros_like(q))
        sw = _nt(qh, kw) + bias_ref[h, pat]
        sc = _nt(qh, kc)
        m = jnp.maximum(jnp.max(sw, axis=-1, keepdims=True), jnp.max(sc, axis=-1, keepdims=True))
        pw = jnp.exp2(sw - m)
        pc = jnp.exp2(sc - m)
        den = jnp.sum(pw, axis=-1, keepdims=True) + jnp.sum(pc, axis=-1, keepdims=True)
        o = _dot(pw.astype(BF16), vw) + _dot(pc.astype(BF16), vc)
        out = out + jnp.where(hm, o / den, 0.0)
    o_ref[0] = out.astype(o_ref.dtype)


def _na_latent(q, kx, vx, kc, vc, rpb, rb):
    b, l, _ = q.shape
    lc = kc.shape[1]
    rows = l // GRID_W
    plan = _na_plan(rows, rb)
    kr, ks, pat_of = plan[0], plan[1], plan[2]
    bias = _na_bias(rpb, plan, rb)
    tq, nkeys = rb * GRID_W, kr * GRID_W
    grid_spec = pltpu.PrefetchScalarGridSpec(
        num_scalar_prefetch=2,
        grid=(b, rows // rb),
        in_specs=[pl.BlockSpec((1, tq, NA_W), lambda i, g, *_: (i, g, 0)),
                  pl.BlockSpec((1, l, NA_W), lambda i, g, *_: (i, 0, 0)),
                  pl.BlockSpec((1, l, NA_W), lambda i, g, *_: (i, 0, 0)),
                  pl.BlockSpec((1, lc, NA_W), lambda i, g, *_: (i, 0, 0)),
                  pl.BlockSpec((1, lc, NA_W), lambda i, g, *_: (i, 0, 0)),
                  pl.BlockSpec(bias.shape, lambda i, g, *_: (0, 0, 0, 0))],
        out_specs=pl.BlockSpec((1, tq, NA_W), lambda i, g, *_: (i, g, 0)),
    )
    return pl.pallas_call(
        functools.partial(_na_kernel, nkeys=nkeys),
        grid_spec=grid_spec,
        out_shape=jax.ShapeDtypeStruct((b, l, NA_W), BF16),
        compiler_params=_cparams(("parallel", "arbitrary")),
        name="na_latent",
    )(jnp.asarray(ks), jnp.asarray(pat_of), q, kx, vx, kc, vc, bias)


def _na_ctx_kernel(q_ref, kc_ref, vc_ref, o_ref):
    q = q_ref[0]
    kc = kc_ref[0]
    vc = vc_ref[0]
    lane = lax.broadcasted_iota(jnp.int32, (1, NA_W), 1)
    out = jnp.zeros((q.shape[0], NA_W), F32)
    for h in range(NA_HEADS):
        hm = (lane >= NA_DIM * h) & (lane < NA_DIM * (h + 1))
        qh = jnp.where(hm, q, jnp.zeros_like(q))
        sc = _nt(qh, kc)
        pc = jnp.exp2(sc - jnp.max(sc, axis=-1, keepdims=True))
        o = _dot(pc.astype(BF16), vc)
        out = out + jnp.where(hm, o / jnp.sum(pc, axis=-1, keepdims=True), 0.0)
    o_ref[0] = out.astype(o_ref.dtype)


def _na_ctx(q, kc, vc):
    b, lc, _ = q.shape
    spec = pl.BlockSpec((1, lc, NA_W), lambda i: (i, 0, 0))
    return pl.pallas_call(
        _na_ctx_kernel, grid=(b,), in_specs=[spec, spec, spec], out_specs=spec,
        out_shape=jax.ShapeDtypeStruct((b, lc, NA_W), BF16),
        compiler_params=_cparams(("parallel",)), name="na_ctx",
    )(q, kc, vc)


def _diff_kernel(*refs, lam_init, has_x):
    if has_x:
        q_ref, kc_ref, vc_ref, kx_ref, vx_ref, lam_ref, g_ref, o_ref = refs
    else:
        q_ref, kc_ref, vc_ref, lam_ref, g_ref, o_ref = refs
    q = q_ref[0]
    kc = kc_ref[0]
    vc = vc_ref[0]
    lp = lam_ref[...]
    lam = (jnp.exp(jnp.sum(lp[0:1] * lp[1:2], axis=-1, keepdims=True))
           - jnp.exp(jnp.sum(lp[2:3] * lp[3:4], axis=-1, keepdims=True)) + lam_init)
    lane = lax.broadcasted_iota(jnp.int32, (1, DIFF_W), 1)
    out = jnp.zeros((q.shape[0], DIFF_W), F32)
    def scores(h):
        res = []
        for c in range(2):
            lo = DIFF_VDIM * h + DIFF_DIM * c
            qm = jnp.where((lane >= lo) & (lane < lo + DIFF_DIM), q, jnp.zeros_like(q))
            res.append((_nt(qm, kc), _nt(qm, kx_ref[0]) if has_x else None))
        return res

    nxt = scores(0)
    for h in range(DIFF_HEADS):
        cur = nxt
        if h + 1 < DIFF_HEADS:
            nxt = scores(h + 1)
        parts = []
        for c in range(2):
            sc, sx = cur[c]
            m = jnp.max(sc, axis=-1, keepdims=True)
            if has_x:
                m = jnp.maximum(m, jnp.max(sx, axis=-1, keepdims=True))
            ec = jnp.exp2(sc - m)
            den = jnp.sum(ec, axis=-1, keepdims=True)
            ex = None
            if has_x:
                ex = jnp.exp2(sx - m)
                den = den + jnp.sum(ex, axis=-1, keepdims=True)
            parts.append((ec, ex, den))
        (ec0, ex0, den0), (ec1, ex1, den1) = parts
        rho = lam * den0 / den1
        o = _dot((ec0 - ec1 * rho).astype(BF16), vc)
        if has_x:
            o = o + _dot((ex0 - ex1 * rho).astype(BF16), vx_ref[0])
        hm = (lane >= DIFF_VDIM * h) & (lane < DIFF_VDIM * (h + 1))
        oh = jnp.where(hm, o / den0, 0.0)
        ms = jnp.sum(oh * oh, axis=-1, keepdims=True) * (1.0 / DIFF_VDIM)
        out = out + oh * lax.rsqrt(ms + NORM_EPS)
    o_ref[0] = (out * g_ref[...] * (1.0 - lam_init)).astype(o_ref.dtype)


def _diff(q, kc, vc, kx, vx, lam_p, g4, lam_init, tq):
    b, l, _ = q.shape
    lc = kc.shape[1]
    has_x = kx is not None
    cspec = pl.BlockSpec((1, lc, DIFF_W), lambda i, t: (i, 0, 0))
    in_specs = [pl.BlockSpec((1, tq, DIFF_W), lambda i, t: (i, t, 0)), cspec, cspec]
    args = [q, kc, vc]
    if has_x:
        xspec = pl.BlockSpec((1, kx.shape[1], DIFF_W), lambda i, t: (i, 0, 0))
        in_specs += [xspec, xspec]
        args += [kx, vx]
    in_specs += [pl.BlockSpec(lam_p.shape, lambda i, t: (0, 0)),
                 pl.BlockSpec((1, DIFF_W), lambda i, t: (0, 0))]
    args += [lam_p, g4]
    return pl.pallas_call(
        functools.partial(_diff_kernel, lam_init=lam_init, has_x=has_x),
        grid=(b, l // tq), in_specs=in_specs,
        out_specs=pl.BlockSpec((1, tq, DIFF_W), lambda i, t: (i, t, 0)),
        out_shape=jax.ShapeDtypeStruct((b, l, DIFF_W), BF16),
        compiler_params=_cparams(("parallel", "arbitrary")),
        name="diff_latent" if has_x else "diff_ctx",
    )(*args)


def _ret_kernel(dl_ref, cq_ref, ck_ref, cv_ref, cg_ref, xq_ref, xk_ref, xv_ref, xg_ref,
                yx_ref, yc_ref, s_ref, *, n_ctx, n_lat):
    h = pl.program_id(1)
    c = RET_CHUNK
    lane = lax.broadcasted_iota(jnp.int32, (1, RET_QK_W), 1)
    hm = (lane >= RET_QK * h) & (lane < RET_QK * (h + 1))
    row = lax.broadcasted_iota(jnp.int32, (c, c), 0).astype(F32)
    col = lax.broadcasted_iota(jnp.int32, (c, c), 1).astype(F32)
    roww = lax.broadcasted_iota(jnp.int32, (c, RET_QK_W), 0).astype(F32)

    def log_sigmoid(d):
        x = jnp.full((1, 1), dl_ref[d, h], F32)
        return jnp.minimum(x, 0.0) - jnp.log1p(jnp.exp(-jnp.abs(x)))

    lgf, lgb = log_sigmoid(0), log_sigmoid(1)
    diff = row - col
    fwd = (jnp.where(diff >= 0, jnp.exp(lgf * jnp.maximum(diff, 0.0)), 0.0),
           jnp.exp(lgf * (roww + 1.0)),
           jnp.exp(lgf * (c - 1.0 - roww)),
           jnp.exp(lgf * c))
    bwd = (jnp.where(diff <= 0, jnp.exp(lgb * jnp.maximum(-diff, 0.0)), 0.0),
           jnp.exp(lgb * (c - roww)),
           jnp.exp(lgb * roww),
           jnp.exp(lgb * c))

    def step(q_ref, k_ref, v_ref, g_ref, out_ref, ci, mats, sdir, second):
        dm, qd, kd, cd = mats
        sl = pl.ds(pl.multiple_of(ci * c, c), c)
        q = q_ref[0, sl, :]
        q = jnp.where(hm, q, jnp.zeros_like(q))
        k = k_ref[0, sl, :]
        v = v_ref[0, sl, :]
        att = _nt(q, k) * dm
        s = s_ref[sdir]
        o = _dot(att.astype(BF16), v) + _dot((q.astype(F32) * qd).astype(BF16), s.astype(BF16))
        s_ref[sdir] = cd * s + _tn((k.astype(F32) * kd).astype(BF16), v)
        if second:
            tot = out_ref[0, sl, :] + o
            out_ref[0, sl, :] = _rms(tot) * _silu(g_ref[0, sl, :])
        else:
            out_ref[0, sl, :] = o

    def both(refs, n, i, second):
        step(*refs, i, fwd, 0, second)
        step(*refs, n - 1 - i, bwd, 1, second)

    s_ref[...] = jnp.zeros_like(s_ref)
    crefs = (cq_ref, ck_ref, cv_ref, cg_ref, yc_ref)
    xrefs = (xq_ref, xk_ref, xv_ref, xg_ref, yx_ref)
    for i in range(n_ctx):
        both(crefs, n_ctx, i, i >= n_ctx // 2)
    for second in (False, True):
        def body(i, carry, second=second):
            both(xrefs, n_lat, i, second)
            return carry

        lax.fori_loop(n_lat // 2 if second else 0, n_lat if second else n_lat // 2, body, 0,
                      unroll=RET_UNROLL if (n_lat // 2) % RET_UNROLL == 0 else 1)


def _retention(decay_logit, cq, ck, cv, cg, xq, xk, xv, xg):
    b, l, _ = xq.shape
    lc = cq.shape[1]

    def qk_spec(n):
        return pl.BlockSpec((1, n, RET_QK_W), lambda i, h: (i, 0, 0))

    def v_spec(n):
        return pl.BlockSpec((1, n, RET_V), lambda i, h: (i, 0, h))

    return pl.pallas_call(
        functools.partial(_ret_kernel, n_ctx=lc // RET_CHUNK, n_lat=l // RET_CHUNK),
        grid=(b, RET_HEADS),
        in_specs=[pl.BlockSpec(memory_space=pltpu.SMEM),
                  qk_spec(lc), qk_spec(lc), v_spec(lc), v_spec(lc),
                  qk_spec(l), qk_spec(l), v_spec(l), v_spec(l)],
        out_specs=[v_spec(l), v_spec(lc)],
        out_shape=[jax.ShapeDtypeStruct((b, l, RET_W), F32), jax.ShapeDtypeStruct((b, lc, RET_W), F32)],
        scratch_shapes=[pltpu.VMEM((2, RET_QK_W, RET_V), F32)],
        compiler_params=_cparams(("parallel", "arbitrary")),
        name="retention",
    )(decay_logit.astype(F32), cq, ck, cv, cg, xq, xk, xv, xg)


def _outproj_kernel(x_ref, yna_ref, ydf_ref, yrt_ref, mod_ref, wo_ref, g2_ref, wq_ref, sk_ref,
                    xo_ref, h2_ref, st_ref, *, d):
    y = (_dot(yna_ref[0], wo_ref[0:NA_W, :])
         + _dot(ydf_ref[0], wo_ref[NA_W:NA_W + DIFF_W, :])
         + _dot(yrt_ref[0].astype(BF16), wo_ref[NA_W + DIFF_W:MIX_W, :]))
    mod = mod_ref[0]
    x = x_ref[0] + mod[:, 2 * d:3 * d] * y
    xo_ref[0] = x
    h2 = (_rms(x) * g2_ref[...] * (1.0 + mod[:, 4 * d:5 * d]) + mod[:, 3 * d:4 * d]).astype(BF16)
    h2_ref[0] = h2
    qp = _dot(h2, wq_ref[...]).astype(BF16)
    for hp in range(2 * PEER_HEADS):
        st_ref[0, hp] = _nt(sk_ref[hp], qp[:, hp * PEER_HALF:(hp + 1) * PEER_HALF])


def _outproj(x, yna, ydf, yrt, mod3, mod_row, wo, g2, wq, sk, tm):
    b, l, d = x.shape
    nhp = 2 * PEER_HEADS

    def tok(w):
        return pl.BlockSpec((1, tm, w), lambda i, t: (i, t, 0))

    return pl.pallas_call(
        functools.partial(_outproj_kernel, d=d),
        grid=(b, l // tm),
        in_specs=[tok(d), tok(NA_W), tok(DIFF_W), tok(RET_W),
                  pl.BlockSpec((1, 1, 6 * d), lambda i, t: (mod_row(i), 0, 0)),
                  pl.BlockSpec(wo.shape, lambda i, t: (0, 0)),
                  pl.BlockSpec((1, d), lambda i, t: (0, 0)),
                  pl.BlockSpec(wq.shape, lambda i, t: (0, 0)),
                  pl.BlockSpec(sk.shape, lambda i, t: (0, 0, 0))],
        out_specs=[tok(d), tok(d),
                   pl.BlockSpec((1, nhp, PEER_NKEYS, tm), lambda i, t: (i, 0, 0, t))],
        out_shape=[jax.ShapeDtypeStruct((b, l, d), F32), jax.ShapeDtypeStruct((b, l, d), BF16),
                   jax.ShapeDtypeStruct((b, nhp, PEER_NKEYS, l), F32)],
        compiler_params=_cparams(("parallel", "parallel")),
        name="outproj",
    )(x, yna, ydf, yrt, mod3, wo, g2, wq, sk)


def _sort_network(n):
    pairs = []
    p = 1
    while p < n:
        k = p
        while k >= 1:
            for j in range(k % p, n - k, 2 * k):
                for i in range(min(k, n - j - k)):
                    if (i + j) // (2 * p) == (i + j + k) // (2 * p):
                        pairs.append((i + j, i + j + k))
            k //= 2
        p *= 2
    return pairs


def _topk_kernel(st_ref, th_ref, e1_ref, e2_ref, cand_ref, t2_ref):
    neg = -jnp.inf

    def blocks(x):
        return [x[r * 8:(r + 1) * 8, :] for r in range(x.shape[0] // 8)]

    def sort_columns(vs):
        vs = list(vs)
        for i, j in _sort_network(len(vs)):
            vs[i], vs[j] = jnp.maximum(vs[i], vs[j]), jnp.minimum(vs[i], vs[j])
        return vs

    def top_rows(x):
        vs = sort_columns(blocks(x))
        rows = []
        for k in range(PEER_TOPK):
            m = jnp.max(vs[0], axis=0, keepdims=True)
            rows.append(m)
            depth = min(len(vs), PEER_TOPK - 1 - k)
            if depth > 0:
                hit = vs[0] == m
                for r in range(depth):
                    below = vs[r + 1] if r + 1 < len(vs) else neg
                    vs[r] = jnp.where(hit, below, vs[r])
        return rows

    cand_ref[len(PEER_PAIRS):, :] = jnp.full((PEER_NCAND - len(PEER_PAIRS), cand_ref.shape[1]), neg, F32)
    for h in range(PEER_HEADS):
        s1 = st_ref[0, 2 * h]
        s2 = st_ref[0, 2 * h + 1]
        a1 = s1 - jnp.max(s1, axis=0, keepdims=True)
        a2 = s2 - jnp.max(s2, axis=0, keepdims=True)
        t1 = top_rows(a1)
        t2 = top_rows(a2)
        for b in range(PEER_TOPK):
            t2_ref[b:b + 1, :] = t2[b]
        t2s = t2_ref[...]
        off = 0
        for a in range(PEER_TOPK):
            n = PEER_TOPK // (a + 1)
            cand_ref[off:off + n, :] = (t1[a] + t2s)[0:n]
            off += n
        cand = cand_ref[...]
        tau = top_rows(cand)[-1]
        z = jnp.sum(jnp.where(cand >= tau, jnp.exp(cand), 0.0), axis=0, keepdims=True)
        th_ref[0, h] = jnp.exp(tau - a1)
        e1_ref[0, h] = jnp.exp(a1) * (0.5 / z)
        e2_ref[0, h] = jnp.exp(a2)


def _topk(st, tl):
    b, nhp, nk, l = st.shape
    hspec = pl.BlockSpec((1, PEER_HEADS, nk, tl), lambda i, t: (i, 0, 0, t))
    hshape = jax.ShapeDtypeStruct((b, PEER_HEADS, nk, l), F32)
    return pl.pallas_call(
        _topk_kernel,
        grid=(b, l // tl),
        in_specs=[pl.BlockSpec((1, nhp, nk, tl), lambda i, t: (i, 0, 0, t))],
        out_specs=[hspec, hspec, hspec],
        out_shape=[hshape, hshape, hshape],
        scratch_shapes=[pltpu.VMEM((PEER_NCAND, tl), F32), pltpu.VMEM((PEER_TOPK, tl), F32)],
        compiler_params=_cparams(("parallel", "parallel")),
        name="peer_topk",
    )(st)


def _peer_kernel(*refs, final, d, ich):
    if final:
        h2_ref, u_ref, vt_ref, th_ref, e1_ref, e2_ref, x_ref, mod_ref, fg_ref, o_ref = refs[:10]
    else:
        h2_ref, u_ref, vt_ref, th_ref, e1_ref, e2_ref, x_ref, mod_ref, o_ref = refs[:9]
    acc_ref, at_ref, gw_ref, h2t_ref = refs[-4:]
    c = pl.program_id(2)
    tm = h2_ref.shape[1]
    nt = PEER_NKEYS // 8

    @pl.when(c == 0)
    def _():
        acc_ref[...] = jnp.zeros_like(acc_ref)
        h2t_ref[...] = h2_ref[0].T

    at_ref[...] = _dot(u_ref[...], h2t_ref[...])
    for ii in range(ich):
        for lg in range(tm // 128):
            ls = slice(lg * 128, (lg + 1) * 128)
            w = [None] * nt
            for h in range(PEER_HEADS):
                th = jnp.broadcast_to(th_ref[0, h, ii:ii + 1, ls], (8, 128))
                e1 = jnp.broadcast_to(e1_ref[0, h, ii:ii + 1, ls], (8, 128))
                for jt in range(nt):
                    e2 = e2_ref[0, h, jt * 8:(jt + 1) * 8, ls]
                    term = jnp.where(e2 >= th, e2, 0.0) * e1
                    w[jt] = term if w[jt] is None else w[jt] + term
            for jt in range(0, nt, 2):
                r0 = ii * PEER_NKEYS + jt * 8
                a = at_ref[r0:r0 + 16, ls]
                g = a * (1.0 + lax.erf(a * math.sqrt(0.5))) * jnp.concatenate(w[jt:jt + 2], axis=0)
                gw_ref[r0:r0 + 16, ls] = g.astype(BF16)
    acc_ref[...] += _dot(vt_ref[...], gw_ref[...])

    @pl.when(c == pl.num_programs(2) - 1)
    def _():
        mod = mod_ref[0]
        xo = x_ref[0] + mod[:, 5 * d:6 * d] * acc_ref[...].T
        if final:
            xo = _rms(xo) * fg_ref[...]
        o_ref[0] = xo


def _peer(h2, u, vt, th, e1, e2, x, mod3, mod_row, final_g, tm, ich=16):
    b, l, d = x.shape
    ne = u.shape[0]
    ec = ich * PEER_NKEYS
    final = final_g is not None
    ispec = pl.BlockSpec((1, PEER_HEADS, ich, tm), lambda i, t, c: (i, 0, c, t))
    in_specs = [pl.BlockSpec((1, tm, d), lambda i, t, c: (i, t, 0)),
                pl.BlockSpec((ec, d), lambda i, t, c: (c, 0)),
                pl.BlockSpec((d, ec), lambda i, t, c: (0, c)),
                ispec, ispec,
                pl.BlockSpec((1, PEER_HEADS, PEER_NKEYS, tm), lambda i, t, c: (i, 0, 0, t)),
                pl.BlockSpec((1, tm, d), lambda i, t, c: (i, t, 0)),
                pl.BlockSpec((1, 1, 6 * d), lambda i, t, c: (mod_row(i), 0, 0))]
    args = [h2, u, vt, th, e1, e2, x, mod3]
    if final:
        in_specs.append(pl.BlockSpec((1, d), lambda i, t, c: (0, 0)))
        args.append(final_g)
    return pl.pallas_call(
        functools.partial(_peer_kernel, final=final, d=d, ich=ich),
        grid=(b, l // tm, ne // ec),
        in_specs=in_specs,
        out_specs=pl.BlockSpec((1, tm, d), lambda i, t, c: (i, t, 0)),
        out_shape=jax.ShapeDtypeStruct((b, l, d), F32),
        scratch_shapes=[pltpu.VMEM((d, tm), F32), pltpu.VMEM((ec, tm), F32),
                        pltpu.VMEM((ec, tm), BF16), pltpu.VMEM((d, tm), BF16)],
        compiler_params=_cparams(("parallel", "parallel", "arbitrary")),
        name="peer_final" if final else "peer",
    )(*args)


def _rope_tables(l, dim, width):
    t = jnp.arange(l)
    rows = (t // GRID_W).astype(F32)
    cols = (t % GRID_W).astype(F32)
    half = dim // 2
    inv = jnp.power(ROPE_BASE, -jnp.arange(0, half, 2, dtype=F32) / half)
    ang = jnp.concatenate([rows[:, None] * inv, cols[:, None] * inv], axis=-1)
    pair = (np.arange(width) % dim) // 2
    sign = np.where(np.arange(width) % 2 == 0, -1.0, 1.0).astype(np.float32)
    return jnp.cos(ang)[:, pair], jnp.sin(ang)[:, pair] * sign


def _swap_pairs(w):
    d, n = w.shape
    return w.reshape(d, n // 2, 2)[:, :, ::-1].reshape(d, n)


def _block_mixer_and_peer(x, ctx, mod3, layer, need_ctx, p, tables, final_g):
    b, l, d = x.shape
    lc = ctx.shape[1]
    row_x = lambda i: i
    row_c = lambda i: b
    lam_init = 0.8 - 0.6 * math.exp(-0.3 * layer)

    lat = _inproj(x, mod3, row_x, p["g1"], p["w_in"], tables, min(l, 512))
    cx = _inproj(ctx, mod3, row_c, p["g1"], p["w_in"], None, lc)
    naq, nak, nav, dfq, dfk, dfv, rtq, rtk, rtv, rtg = lat
    cnaq, cnak, cnav, cdfq, cdfk, cdfv, crtq, crtk, crtv, crtg = cx

    y_na = _na_latent(naq, nak, nav, cnak, cnav, p["rpb"], 4)
    y_df = _diff(dfq, cdfk, cdfv, dfk, dfv, p["lam"], p["subln"], lam_init, 256)
    y_rt, yc_rt = _retention(p["decay"], crtq, crtk, crtv, crtg, rtq, rtk, rtv, rtg)

    def channel_mix(xx, yna, ydf, yrt, row, tm, fg):
        xo, h2, st = _outproj(xx, yna, ydf, yrt, mod3, row, p["w_out"], p["g2"], p["wq"], p["sk"], tm)
        th, e1, e2 = _topk(st, min(tm, 256))
        return _peer(h2, p["u"], p["vt"], th, e1, e2, xo, mod3, row, fg, tm)

    x = channel_mix(x, y_na, y_df, y_rt, row_x, min(l, 512), final_g)
    if need_ctx:
        yc_na = _na_ctx(cnaq, cnak, cnav)
        yc_df = _diff(cdfq, cdfk, cdfv, None, None, p["lam"], p["subln"], lam_init, lc)
        flat = lambda t: t.reshape(1, b * lc, t.shape[-1])
        ctx = channel_mix(flat(ctx), flat(yc_na), flat(yc_df), flat(yc_rt), lambda i: b,
                          min(b * lc, 512), None).reshape(b, lc, d)
    return x, ctx


def kernel(x, c, ctx, c_ctx, w_ada, b_ada, norm1_g, w_in, na_rpb, diff_lambda, diff_subln_g,
           ret_decay_logit, w_out, norm2_g, peer_wq, peer_subkeys, peer_u, peer_v, final_g):
    b, l, d = x.shape
    depth = w_ada.shape[0]
    x = x.astype(F32)
    ctx = ctx.astype(F32)
    npad = -(-(b + 1) // 8) * 8
    cpad = jnp.zeros((npad, d), F32).at[:b].set(c.astype(F32)).at[b].set(c_ctx.astype(F32))
    mod = _ada(cpad, w_ada.astype(F32), b_ada.astype(F32))
    tables = _rope_tables(l, DIFF_DIM, DIFF_QK_W) + _rope_tables(l, RET_QK, RET_QK_W)
    for layer in range(depth):
        wi = w_in[layer]
        w_ext = jnp.concatenate(
            [wi] + [_swap_pairs(wi[:, c0:c0 + 256]) for c0 in (C_DFQ, C_DFK, C_RTQ, C_RTK)], axis=1)
        p = {
            "g1": norm1_g[layer].astype(F32).reshape(1, d),
            "g2": norm2_g[layer].astype(F32).reshape(1, d),
            "w_in": w_ext.astype(BF16),
            "rpb": na_rpb[layer],
            "lam": diff_lambda[layer].astype(F32),
            "subln": jnp.tile(diff_subln_g[layer].astype(F32), DIFF_HEADS).reshape(1, DIFF_W),
            "decay": ret_decay_logit[layer],
            "w_out": w_out[layer].astype(BF16),
            "wq": peer_wq[layer].astype(BF16),
            "sk": peer_subkeys[layer].astype(BF16).reshape(2 * PEER_HEADS, PEER_NKEYS, PEER_HALF),
            "u": peer_u[layer].astype(BF16),
            "vt": peer_v[layer].astype(BF16).T,
        }
        last = layer == depth - 1
        x, ctx = _block_mixer_and_peer(x, ctx, mod[layer].reshape(npad, 1, 6 * d), layer, not last, p,
                                       tables, final_g.astype(F32).reshape(1, d) if last else None)
    return x
```

```python
import functools
import math

import numpy as np
import jax
import jax.numpy as jnp
from jax import lax
from jax.experimental import pallas as pl
from jax.experimental.pallas import tpu as pltpu

F32 = jnp.float32
BF16 = jnp.bfloat16

GRID_W = 64
NORM_EPS = 1e-6
ROPE_BASE = 10000.0
NEG_INF = -1e30
LOG2E = math.log2(math.e)

NA_HEADS = 4
NA_DIM = 64
NA_WIN_ROWS = 8
NA_WIN_COLS = 16
DIFF_HEADS = 4
DIFF_DIM = 32
DIFF_VDIM = 64
RET_HEADS = 4
RET_QK = 64
RET_V = 128
RET_CHUNK = 128
RET_UNROLL = 8

NA_W = NA_HEADS * NA_DIM
DIFF_QK_W = DIFF_HEADS * 2 * DIFF_DIM
DIFF_W = DIFF_HEADS * DIFF_VDIM
RET_QK_W = RET_HEADS * RET_QK
RET_W = RET_HEADS * RET_V
MIX_W = NA_W + DIFF_W + RET_W
IN_COLS = 3 * NA_W + 2 * DIFF_QK_W + DIFF_W + 2 * RET_QK_W + 2 * RET_W

PEER_HEADS = 8
PEER_NKEYS = 128
PEER_KDIM = 256
PEER_TOPK = 16
PEER_HALF = PEER_KDIM // 2

C_NAQ, C_NAK, C_NAV = 0, 256, 512
C_DFQ, C_DFK, C_DFV = 768, 1024, 1280
C_RTQ, C_RTK, C_RTV, C_RTG = 1536, 1792, 2048, 2560

PEER_PAIRS = tuple((a, b) for a in range(PEER_TOPK) for b in range(PEER_TOPK // (a + 1)))
PEER_NCAND = 64

VMEM_LIMIT = 56 * 1024 * 1024


def _cparams(sem):
    return pltpu.CompilerParams(dimension_semantics=sem, vmem_limit_bytes=VMEM_LIMIT)


def _nt(a, b):
    return lax.dot_general(a, b, (((1,), (1,)), ((), ())), preferred_element_type=F32)


def _tn(a, b):
    return lax.dot_general(a, b, (((0,), (0,)), ((), ())), preferred_element_type=F32)


def _dot(a, b):
    return jnp.dot(a, b, preferred_element_type=F32)


def _rms(x):
    return x * lax.rsqrt(jnp.mean(x * x, axis=-1, keepdims=True) + NORM_EPS)


def _silu(x):
    return x * jax.nn.sigmoid(x)


def _ada_kernel(c_ref, w_ref, b_ref, o_ref):
    s = _silu(c_ref[...])
    o_ref[0] = jnp.dot(s, w_ref[0], preferred_element_type=F32,
                       precision=lax.Precision.HIGHEST) + b_ref[0]


def _ada(cpad, w_ada, b_ada):
    depth, d, n = w_ada.shape
    tn = 1536
    return pl.pallas_call(
        _ada_kernel,
        grid=(depth, n // tn),
        in_specs=[pl.BlockSpec((cpad.shape[0], d), lambda l, j: (0, 0)),
                  pl.BlockSpec((1, d, tn), lambda l, j: (l, 0, j)),
                  pl.BlockSpec((1, 1, tn), lambda l, j: (l, 0, j))],
        out_specs=pl.BlockSpec((1, cpad.shape[0], tn), lambda l, j: (l, 0, j)),
        out_shape=jax.ShapeDtypeStruct((depth, cpad.shape[0], n), F32),
        compiler_params=_cparams(("parallel", "parallel")),
        name="ada",
    )(cpad, w_ada, b_ada.reshape(depth, 1, n))


def _inproj_kernel(*refs, rope, d):
    if rope:
        (x_ref, mod_ref, g_ref, w_ref, cd_ref, sd_ref, cr_ref, sr_ref,
         naq, nak, nav, dfq, dfk, dfv, rtq, rtk, rtv, rtg) = refs
    else:
        (x_ref, mod_ref, g_ref, w_ref,
         naq, nak, nav, dfq, dfk, dfv, rtq, rtk, rtv, rtg) = refs
    mod = mod_ref[0]
    h = _rms(x_ref[0]) * g_ref[...] * (1.0 + mod[:, d:2 * d]) + mod[:, 0:d]
    hb = h.astype(BF16)

    def proj(c0, n):
        return _dot(hb, w_ref[:, c0:c0 + n])

    even_lane = (lax.broadcasted_iota(jnp.int32, (1, 256), 1) & 1) == 0

    def roped(c0, cos_ref, sin_ref):
        p = proj(c0, 256)
        if not rope:
            return p
        partner = jnp.where(even_lane, pltpu.roll(p, 255, 1), pltpu.roll(p, 1, 1))
        return p * cos_ref[...] + partner * sin_ref[...]

    naq[0] = (proj(C_NAQ, 256) * (NA_DIM ** -0.5 * LOG2E)).astype(BF16)
    nak[0] = proj(C_NAK, 256).astype(BF16)
    nav[0] = proj(C_NAV, 256).astype(BF16)
    dfq[0] = (roped(C_DFQ, cd_ref if rope else None, sd_ref if rope else None)
              * (DIFF_DIM ** -0.5 * LOG2E)).astype(BF16)
    dfk[0] = roped(C_DFK, cd_ref if rope else None, sd_ref if rope else None).astype(BF16)
    dfv[0] = proj(C_DFV, 256).astype(BF16)
    rtq[0] = roped(C_RTQ, cr_ref if rope else None, sr_ref if rope else None).astype(BF16)
    rtk[0] = (roped(C_RTK, cr_ref if rope else None, sr_ref if rope else None)
              * RET_QK ** -0.5).astype(BF16)
    rtv[0] = proj(C_RTV, 512).astype(BF16)
    rtg[0] = proj(C_RTG, 512)


def _inproj(x, mod3, mod_row, g1, w_in, tables, tm):
    b, l, d = x.shape
    rope = tables is not None
    in_specs = [pl.BlockSpec((1, tm, d), lambda i, t: (i, t, 0)),
                pl.BlockSpec((1, 1, 6 * d), lambda i, t: (mod_row(i), 0, 0)),
                pl.BlockSpec((1, d), lambda i, t: (0, 0)),
                pl.BlockSpec((d, IN_COLS), lambda i, t: (0, 0))]
    args = [x, mod3, g1, w_in]
    if rope:
        in_specs += [pl.BlockSpec((tm, 256), lambda i, t: (t, 0))] * 4
        args += list(tables)
    widths = (256, 256, 256, 256, 256, 256, 256, 256, 512, 512)
    dtypes = (BF16,) * 9 + (F32,)
    out_specs = [pl.BlockSpec((1, tm, w), lambda i, t: (i, t, 0)) for w in widths]
    out_shape = [jax.ShapeDtypeStruct((b, l, w), dt) for w, dt in zip(widths, dtypes)]
    return pl.pallas_call(
        functools.partial(_inproj_kernel, rope=rope, d=d),
        grid=(b, l // tm),
        in_specs=in_specs, out_specs=out_specs, out_shape=out_shape,
        compiler_params=_cparams(("parallel", "parallel")),
        name="inproj_rope" if rope else "inproj_ctx",
    )(*args)


def _na_plan(rows, rb):
    wr = min(NA_WIN_ROWS, rows)
    kr = min(rb + wr - 1, rows)
    nblk = rows // rb
    win0 = np.clip(np.arange(rows) - wr // 2, 0, rows - wr)
    ks = np.clip(np.arange(nblk) * rb - wr // 2, 0, rows - kr)
    pats, pat_of = [], np.zeros(nblk, np.int32)
    for g in range(nblk):
        r = g * rb + np.arange(rb)
        krow = ks[g] + np.arange(kr)
        dr = krow[None, :] - r[:, None] + (NA_WIN_ROWS - 1)
        ok = (krow[None, :] >= win0[r][:, None]) & (krow[None, :] < win0[r][:, None] + wr)
        assert ok.sum(axis=1).min() == wr
        key = (np.where(ok, dr, 0).tobytes(), ok.tobytes())
        for p, (k2, _, _) in enumerate(pats):
            if k2 == key:
                pat_of[g] = p
                break
        else:
            pat_of[g] = len(pats)
            pats.append((key, np.where(ok, dr, 0), ok))
    dr_idx = np.stack([p[1] for p in pats])
    row_ok = np.stack([p[2] for p in pats])
    qcol = np.arange(GRID_W)
    kcol = np.arange(GRID_W)
    cs = np.clip(qcol - NA_WIN_COLS // 2, 0, GRID_W - NA_WIN_COLS)
    col_ok = (kcol[None, :] >= cs[:, None]) & (kcol[None, :] < cs[:, None] + NA_WIN_COLS)
    dc_idx = np.clip(kcol[None, :] - qcol[:, None], 1 - NA_WIN_COLS, NA_WIN_COLS - 1) + NA_WIN_COLS - 1
    return kr, ks.astype(np.int32), pat_of, dr_idx, row_ok, dc_idx, col_ok


def _na_bias(rpb, plan, rb):
    kr, _, _, dr_idx, row_ok, dc_idx, col_ok = plan
    p = dr_idx.shape[0]
    nr, nc = 2 * NA_WIN_ROWS - 1, 2 * NA_WIN_COLS - 1
    row_hot = ((np.arange(nr) == dr_idx[..., None]) & row_ok[..., None]).astype(np.float32)
    col_hot = ((np.arange(nc) == dc_idx[..., None]) & col_ok[..., None]).astype(np.float32)
    bias = jnp.einsum("hrc,pqwr,xyc->hpqxwy", rpb.astype(F32), row_hot, col_hot,
                      precision=lax.Precision.HIGHEST)
    ok = row_ok[:, :, None, :, None] & col_ok[None, None, :, None, :]
    bias = jnp.where(ok[None], bias * LOG2E, NEG_INF)
    return bias.reshape(NA_HEADS, p, rb * GRID_W, kr * GRID_W)


def _na_kernel(ks_ref, pat_ref, q_ref, kx_ref, vx_ref, kc_ref, vc_ref, bias_ref, o_ref, *, nkeys):
    g = pl.program_id(1)
    q = q_ref[0]
    k0 = pl.multiple_of(ks_ref[g] * GRID_W, GRID_W)
    kw = kx_ref[0, pl.ds(k0, nkeys), :]
    vw = vx_ref[0, pl.ds(k0, nkeys), :]
    kc = kc_ref[0]
    vc = vc_ref[0]
    pat = pat_ref[g]
    lane = lax.broadcasted_iota(jnp.int32, (1, NA_W), 1)
    out = jnp.zeros((q.shape[0], NA_W), F32)
    for h in range(NA_HEADS):
        hm = (lane >= NA_DIM * h) & (lane < NA_DIM * (h + 1))
        qh = jnp.where(hm, q, jnp.zeros_like(q))
        sw = _nt(qh, kw) + bias_ref[h, pat]
        sc = _nt(qh, kc)
        m = jnp.maximum(jnp.max(sw, axis=-1, keepdims=True), jnp.max(sc, axis=-1, keepdims=True))
        pw = jnp.exp2(sw - m)
        pc = jnp.exp2(sc - m)
        den = jnp.sum(pw, axis=-1, keepdims=True) + jnp.sum(pc, axis=-1, keepdims=True)
        o = _dot(pw.astype(BF16), vw) + _dot(pc.astype(BF16), vc)
        out = out + jnp.where(hm, o / den, 0.0)
    o_ref[0] = out.astype(o_ref.dtype)


def _na_latent(q, kx, vx, kc, vc, rpb, rb):
    b, l, _ = q.shape
    lc = kc.shape[1]
    rows = l // GRID_W
    plan = _na_plan(rows, rb)
    kr, ks, pat_of = plan[0], plan[1], plan[2]
    bias = _na_bias(rpb, plan, rb)
    tq, nkeys = rb * GRID_W, kr * GRID_W
    grid_spec = pltpu.PrefetchScalarGridSpec(
        num_scalar_prefetch=2,
        grid=(b, rows // rb),
        in_specs=[pl.BlockSpec((1, tq, NA_W), lambda i, g, *_: (i, g, 0)),
                  pl.BlockSpec((1, l, NA_W), lambda i, g, *_: (i, 0, 0)),
                  pl.BlockSpec((1, l, NA_W), lambda i, g, *_: (i, 0, 0)),
                  pl.BlockSpec((1, lc, NA_W), lambda i, g, *_: (i, 0, 0)),
                  pl.BlockSpec((1, lc, NA_W), lambda i, g, *_: (i, 0, 0)),
                  pl.BlockSpec(bias.shape, lambda i, g, *_: (0, 0, 0, 0))],
        out_specs=pl.BlockSpec((1, tq, NA_W), lambda i, g, *_: (i, g, 0)),
    )
    return pl.pallas_call(
        functools.partial(_na_kernel, nkeys=nkeys),
        grid_spec=grid_spec,
        out_shape=jax.ShapeDtypeStruct((b, l, NA_W), BF16),
        compiler_params=_cparams(("parallel", "arbitrary")),
        name="na_latent",
    )(jnp.asarray(ks), jnp.asarray(pat_of), q, kx, vx, kc, vc, bias)


def _na_ctx_kernel(q_ref, kc_ref, vc_ref, o_ref):
    q = q_ref[0]
    kc = kc_ref[0]
    vc = vc_ref[0]
    lane = lax.broadcasted_iota(jnp.int32, (1, NA_W), 1)
    out = jnp.zeros((q.shape[0], NA_W), F32)
    for h in range(NA_HEADS):
        hm = (lane >= NA_DIM * h) & (lane < NA_DIM * (h + 1))
        qh = jnp.where(hm, q, jnp.zeros_like(q))
        sc = _nt(qh, kc)
        pc = jnp.exp2(sc - jnp.max(sc, axis=-1, keepdims=True))
        o = _dot(pc.astype(BF16), vc)
        out = out + jnp.where(hm, o / jnp.sum(pc, axis=-1, keepdims=True), 0.0)
    o_ref[0] = out.astype(o_ref.dtype)


def _na_ctx(q, kc, vc):
    b, lc, _ = q.shape
    spec = pl.BlockSpec((1, lc, NA_W), lambda i: (i, 0, 0))
    return pl.pallas_call(
        _na_ctx_kernel, grid=(b,), in_specs=[spec, spec, spec], out_specs=spec,
        out_shape=jax.ShapeDtypeStruct((b, lc, NA_W), BF16),
        compiler_params=_cparams(("parallel",)), name="na_ctx",
    )(q, kc, vc)


def _diff_kernel(*refs, lam_init, has_x):
    if has_x:
        q_ref, kc_ref, vc_ref, kx_ref, vx_ref, lam_ref, g_ref, o_ref = refs
    else:
        q_ref, kc_ref, vc_ref, lam_ref, g_ref, o_ref = refs
    q = q_ref[0]
    kc = kc_ref[0]
    vc = vc_ref[0]
    lp = lam_ref[...]
    lam = (jnp.exp(jnp.sum(lp[0:1] * lp[1:2], axis=-1, keepdims=True))
           - jnp.exp(jnp.sum(lp[2:3] * lp[3:4], axis=-1, keepdims=True)) + lam_init)
    lane = lax.broadcasted_iota(jnp.int32, (1, DIFF_W), 1)
    out = jnp.zeros((q.shape[0], DIFF_W), F32)
    def scores(h):
        res = []
        for c in range(2):
            lo = DIFF_VDIM * h + DIFF_DIM * c
            qm = jnp.where((lane >= lo) & (lane < lo + DIFF_DIM), q, jnp.zeros_like(q))
            res.append((_nt(qm, kc), _nt(qm, kx_ref[0]) if has_x else None))
        return res

    nxt = scores(0)
    for h in range(DIFF_HEADS):
        cur = nxt
        if h + 1 < DIFF_HEADS:
            nxt = scores(h + 1)
        parts = []
        for c in range(2):
            sc, sx = cur[c]
            m = jnp.max(sc, axis=-1, keepdims=True)
            if has_x:
                m = jnp.maximum(m, jnp.max(sx, axis=-1, keepdims=True))
            ec = jnp.exp2(sc - m)
            den = jnp.sum(ec, axis=-1, keepdims=True)
            ex = None
            if has_x:
                ex = jnp.exp2(sx - m)
                den = den + jnp.sum(ex, axis=-1, keepdims=True)
            parts.append((ec, ex, den))
        (ec0, ex0, den0), (ec1, ex1, den1) = parts
        rho = lam * den0 / den1
        o = _dot((ec0 - ec1 * rho).astype(BF16), vc)
        if has_x:
            o = o + _dot((ex0 - ex1 * rho).astype(BF16), vx_ref[0])
        hm = (lane >= DIFF_VDIM * h) & (lane < DIFF_VDIM * (h + 1))
        oh = jnp.where(hm, o / den0, 0.0)
        ms = jnp.sum(oh * oh, axis=-1, keepdims=True) * (1.0 / DIFF_VDIM)
        out = out + oh * lax.rsqrt(ms + NORM_EPS)
    o_ref[0] = (out * g_ref[...] * (1.0 - lam_init)).astype(o_ref.dtype)


def _diff(q, kc, vc, kx, vx, lam_p, g4, lam_init, tq):
    b, l, _ = q.shape
    lc = kc.shape[1]
    has_x = kx is not None
    cspec = pl.BlockSpec((1, lc, DIFF_W), lambda i, t: (i, 0, 0))
    in_specs = [pl.BlockSpec((1, tq, DIFF_W), lambda i, t: (i, t, 0)), cspec, cspec]
    args = [q, kc, vc]
    if has_x:
        xspec = pl.BlockSpec((1, kx.shape[1], DIFF_W), lambda i, t: (i, 0, 0))
        in_specs += [xspec, xspec]
        args += [kx, vx]
    in_specs += [pl.BlockSpec(lam_p.shape, lambda i, t: (0, 0)),
                 pl.BlockSpec((1, DIFF_W), lambda i, t: (0, 0))]
    args += [lam_p, g4]
    return pl.pallas_call(
        functools.partial(_diff_kernel, lam_init=lam_init, has_x=has_x),
        grid=(b, l // tq), in_specs=in_specs,
        out_specs=pl.BlockSpec((1, tq, DIFF_W), lambda i, t: (i, t, 0)),
        out_shape=jax.ShapeDtypeStruct((b, l, DIFF_W), BF16),
        compiler_params=_cparams(("parallel", "arbitrary")),
        name="diff_latent" if has_x else "diff_ctx",
    )(*args)


def _ret_kernel(dl_ref, cq_ref, ck_ref, cv_ref, cg_ref, xq_ref, xk_ref, xv_ref, xg_ref,
                yx_ref, yc_ref, s_ref, *, n_ctx, n_lat):
    h = pl.program_id(1)
    c = RET_CHUNK
    lane = lax.broadcasted_iota(jnp.int32, (1, RET_QK_W), 1)
    hm = (lane >= RET_QK * h) & (lane < RET_QK * (h + 1))
    row = lax.broadcasted_iota(jnp.int32, (c, c), 0).astype(F32)
    col = lax.broadcasted_iota(jnp.int32, (c, c), 1).astype(F32)
    roww = lax.broadcasted_iota(jnp.int32, (c, RET_QK_W), 0).astype(F32)

    def log_sigmoid(d):
        x = jnp.full((1, 1), dl_ref[d, h], F32)
        return jnp.minimum(x, 0.0) - jnp.log1p(jnp.exp(-jnp.abs(x)))

    lgf, lgb = log_sigmoid(0), log_sigmoid(1)
    diff = row - col
    fwd = (jnp.where(diff >= 0, jnp.exp(lgf * jnp.maximum(diff, 0.0)), 0.0),
           jnp.exp(lgf * (roww + 1.0)),
           jnp.exp(lgf * (c - 1.0 - roww)),
           jnp.exp(lgf * c))
    bwd = (jnp.where(diff <= 0, jnp.exp(lgb * jnp.maximum(-diff, 0.0)), 0.0),
           jnp.exp(lgb * (c - roww)),
           jnp.exp(lgb * roww),
           jnp.exp(lgb * c))

    def step(q_ref, k_ref, v_ref, g_ref, out_ref, ci, mats, sdir, second):
        dm, qd, kd, cd = mats
        sl = pl.ds(pl.multiple_of(ci * c, c), c)
        q = q_ref[0, sl, :]
        q = jnp.where(hm, q, jnp.zeros_like(q))
        k = k_ref[0, sl, :]
        v = v_ref[0, sl, :]
        att = _nt(q, k) * dm
        s = s_ref[sdir]
        o = _dot(att.astype(BF16), v) + _dot((q.astype(F32) * qd).astype(BF16), s.astype(BF16))
        s_ref[sdir] = cd * s + _tn((k.astype(F32) * kd).astype(BF16), v)
        if second:
            tot = out_ref[0, sl, :] + o
            out_ref[0, sl, :] = _rms(tot) * _silu(g_ref[0, sl, :])
        else:
            out_ref[0, sl, :] = o

    def both(refs, n, i, second):
        step(*refs, i, fwd, 0, second)
        step(*refs, n - 1 - i, bwd, 1, second)

    s_ref[...] = jnp.zeros_like(s_ref)
    crefs = (cq_ref, ck_ref, cv_ref, cg_ref, yc_ref)
    xrefs = (xq_ref, xk_ref, xv_ref, xg_ref, yx_ref)
    for i in range(n_ctx):
        both(crefs, n_ctx, i, i >= n_ctx // 2)
    for second in (False, True):
        def body(i, carry, second=second):
            both(xrefs, n_lat, i, second)
            return carry

        lax.fori_loop(n_lat // 2 if second else 0, n_lat if second else n_lat // 2, body, 0,
                      unroll=RET_UNROLL if (n_lat // 2) % RET_UNROLL == 0 else 1)


def _retention(decay_logit, cq, ck, cv, cg, xq, xk, xv, xg):
    b, l, _ = xq.shape
    lc = cq.shape[1]

    def qk_spec(n):
        return pl.BlockSpec((1, n, RET_QK_W), lambda i, h: (i, 0, 0))

    def v_spec(n):
        return pl.BlockSpec((1, n, RET_V), lambda i, h: (i, 0, h))

    return pl.pallas_call(
        functools.partial(_ret_kernel, n_ctx=lc // RET_CHUNK, n_lat=l // RET_CHUNK),
        grid=(b, RET_HEADS),
        in_specs=[pl.BlockSpec(memory_space=pltpu.SMEM),
                  qk_spec(lc), qk_spec(lc), v_spec(lc), v_spec(lc),
                  qk_spec(l), qk_spec(l), v_spec(l), v_spec(l)],
        out_specs=[v_spec(l), v_spec(lc)],
        out_shape=[jax.ShapeDtypeStruct((b, l, RET_W), F32), jax.ShapeDtypeStruct((b, lc, RET_W), F32)],
        scratch_shapes=[pltpu.VMEM((2, RET_QK_W, RET_V), F32)],
        compiler_params=_cparams(("parallel", "arbitrary")),
        name="retention",
    )(decay_logit.astype(F32), cq, ck, cv, cg, xq, xk, xv, xg)


def _outproj_kernel(x_ref, yna_ref, ydf_ref, yrt_ref, mod_ref, wo_ref, g2_ref, wq_ref, sk_ref,
                    xo_ref, h2_ref, st_ref, *, d):
    y = (_dot(yna_ref[0], wo_ref[0:NA_W, :])
         + _dot(ydf_ref[0], wo_ref[NA_W:NA_W + DIFF_W, :])
         + _dot(yrt_ref[0].astype(BF16), wo_ref[NA_W + DIFF_W:MIX_W, :]))
    mod = mod_ref[0]
    x = x_ref[0] + mod[:, 2 * d:3 * d] * y
    xo_ref[0] = x
    h2 = (_rms(x) * g2_ref[...] * (1.0 + mod[:, 4 * d:5 * d]) + mod[:, 3 * d:4 * d]).astype(BF16)
    h2_ref[0] = h2
    qp = _dot(h2, wq_ref[...]).astype(BF16)
    for hp in range(2 * PEER_HEADS):
        st_ref[0, hp] = _nt(sk_ref[hp], qp[:, hp * PEER_HALF:(hp + 1) * PEER_HALF])


def _outproj(x, yna, ydf, yrt, mod3, mod_row, wo, g2, wq, sk, tm):
    b, l, d = x.shape
    nhp = 2 * PEER_HEADS

    def tok(w):
        return pl.BlockSpec((1, tm, w), lambda i, t: (i, t, 0))

    return pl.pallas_call(
        functools.partial(_outproj_kernel, d=d),
        grid=(b, l // tm),
        in_specs=[tok(d), tok(NA_W), tok(DIFF_W), tok(RET_W),
                  pl.BlockSpec((1, 1, 6 * d), lambda i, t: (mod_row(i), 0, 0)),
                  pl.BlockSpec(wo.shape, lambda i, t: (0, 0)),
                  pl.BlockSpec((1, d), lambda i, t: (0, 0)),
                  pl.BlockSpec(wq.shape, lambda i, t: (0, 0)),
                  pl.BlockSpec(sk.shape, lambda i, t: (0, 0, 0))],
        out_specs=[tok(d), tok(d),
                   pl.BlockSpec((1, nhp, PEER_NKEYS, tm), lambda i, t: (i, 0, 0, t))],
        out_shape=[jax.ShapeDtypeStruct((b, l, d), F32), jax.ShapeDtypeStruct((b, l, d), BF16),
                   jax.ShapeDtypeStruct((b, nhp, PEER_NKEYS, l), F32)],
        compiler_params=_cparams(("parallel", "parallel")),
        name="outproj",
    )(x, yna, ydf, yrt, mod3, wo, g2, wq, sk)


def _sort_network(n):
    pairs = []
    p = 1
    while p < n:
        k = p
        while k >= 1:
            for j in range(k % p, n - k, 2 * k):
                for i in range(min(k, n - j - k)):
                    if (i + j) // (2 * p) == (i + j + k) // (2 * p):
                        pairs.append((i + j, i + j + k))
            k //= 2
        p *= 2
    return pairs


def _topk_kernel(st_ref, th_ref, e1_ref, e2_ref, cand_ref, t2_ref):
    neg = -jnp.inf

    def blocks(x):
        return [x[r * 8:(r + 1) * 8, :] for r in range(x.shape[0] // 8)]

    def sort_columns(vs):
        vs = list(vs)
        for i, j in _sort_network(len(vs)):
            vs[i], vs[j] = jnp.maximum(vs[i], vs[j]), jnp.minimum(vs[i], vs[j])
        return vs

    def top_rows(x):
        vs = sort_columns(blocks(x))
        rows = []
        for k in range(PEER_TOPK):
            m = jnp.max(vs[0], axis=0, keepdims=True)
            rows.append(m)
            depth = min(len(vs), PEER_TOPK - 1 - k)
            if depth > 0:
                hit = vs[0] == m
                for r in range(depth):
                    below = vs[r + 1] if r + 1 < len(vs) else neg
                    vs[r] = jnp.where(hit, below, vs[r])
        return rows

    cand_ref[len(PEER_PAIRS):, :] = jnp.full((PEER_NCAND - len(PEER_PAIRS), cand_ref.shape[1]), neg, F32)
    for h in range(PEER_HEADS):
        s1 = st_ref[0, 2 * h]
        s2 = st_ref[0, 2 * h + 1]
        a1 = s1 - jnp.max(s1, axis=0, keepdims=True)
        a2 = s2 - jnp.max(s2, axis=0, keepdims=True)
        t1 = top_rows(a1)
        t2 = top_rows(a2)
        for b in range(PEER_TOPK):
            t2_ref[b:b + 1, :] = t2[b]
        t2s = t2_ref[...]
        off = 0
        for a in range(PEER_TOPK):
            n = PEER_TOPK // (a + 1)
            cand_ref[off:off + n, :] = (t1[a] + t2s)[0:n]
            off += n
        cand = cand_ref[...]
        tau = top_rows(cand)[-1]
        z = jnp.sum(jnp.where(cand >= tau, jnp.exp(cand), 0.0), axis=0, keepdims=True)
        th_ref[0, h] = jnp.exp(tau - a1)
        e1_ref[0, h] = jnp.exp(a1) * (0.5 / z)
        e2_ref[0, h] = jnp.exp(a2)


def _topk(st, tl):
    b, nhp, nk, l = st.shape
    hspec = pl.BlockSpec((1, PEER_HEADS, nk, tl), lambda i, t: (i, 0, 0, t))
    hshape = jax.ShapeDtypeStruct((b, PEER_HEADS, nk, l), F32)
    return pl.pallas_call(
        _topk_kernel,
        grid=(b, l // tl),
        in_specs=[pl.BlockSpec((1, nhp, nk, tl), lambda i, t: (i, 0, 0, t))],
        out_specs=[hspec, hspec, hspec],
        out_shape=[hshape, hshape, hshape],
        scratch_shapes=[pltpu.VMEM((PEER_NCAND, tl), F32), pltpu.VMEM((PEER_TOPK, tl), F32)],
        compiler_params=_cparams(("parallel", "parallel")),
        name="peer_topk",
    )(st)


def _peer_kernel(*refs, final, d, ich):
    if final:
        h2_ref, u_ref, vt_ref, th_ref, e1_ref, e2_ref, x_ref, mod_ref, fg_ref, o_ref = refs[:10]
    else:
        h2_ref, u_ref, vt_ref, th_ref, e1_ref, e2_ref, x_ref, mod_ref, o_ref = refs[:9]
    acc_ref, at_ref, gw_ref, h2t_ref = refs[-4:]
    c = pl.program_id(2)
    tm = h2_ref.shape[1]
    nt = PEER_NKEYS // 8

    @pl.when(c == 0)
    def _():
        acc_ref[...] = jnp.zeros_like(acc_ref)
        h2t_ref[...] = h2_ref[0].T

    at_ref[...] = _dot(u_ref[...], h2t_ref[...])
    for ii in range(ich):
        for lg in range(tm // 128):
            ls = slice(lg * 128, (lg + 1) * 128)
            w = [None] * nt
            for h in range(PEER_HEADS):
                th = jnp.broadcast_to(th_ref[0, h, ii:ii + 1, ls], (8, 128))
                e1 = jnp.broadcast_to(e1_ref[0, h, ii:ii + 1, ls], (8, 128))
                for jt in range(nt):
                    e2 = e2_ref[0, h, jt * 8:(jt + 1) * 8, ls]
                    term = jnp.where(e2 >= th, e2, 0.0) * e1
                    w[jt] = term if w[jt] is None else w[jt] + term
            for jt in range(0, nt, 2):
                r0 = ii * PEER_NKEYS + jt * 8
                a = at_ref[r0:r0 + 16, ls]
                g = a * (1.0 + lax.erf(a * math.sqrt(0.5))) * jnp.concatenate(w[jt:jt + 2], axis=0)
                gw_ref[r0:r0 + 16, ls] = g.astype(BF16)
    acc_ref[...] += _dot(vt_ref[...], gw_ref[...])

    @pl.when(c == pl.num_programs(2) - 1)
    def _():
        mod = mod_ref[0]
        xo = x_ref[0] + mod[:, 5 * d:6 * d] * acc_ref[...].T
        if final:
            xo = _rms(xo) * fg_ref[...]
        o_ref[0] = xo


def _peer(h2, u, vt, th, e1, e2, x, mod3, mod_row, final_g, tm, ich=16):
    b, l, d = x.shape
    ne = u.shape[0]
    ec = ich * PEER_NKEYS
    final = final_g is not None
    ispec = pl.BlockSpec((1, PEER_HEADS, ich, tm), lambda i, t, c: (i, 0, c, t))
    in_specs = [pl.BlockSpec((1, tm, d), lambda i, t, c: (i, t, 0)),
                pl.BlockSpec((ec, d), lambda i, t, c: (c, 0)),
                pl.BlockSpec((d, ec), lambda i, t, c: (0, c)),
                ispec, ispec,
                pl.BlockSpec((1, PEER_HEADS, PEER_NKEYS, tm), lambda i, t, c: (i, 0, 0, t)),
                pl.BlockSpec((1, tm, d), lambda i, t, c: (i, t, 0)),
                pl.BlockSpec((1, 1, 6 * d), lambda i, t, c: (mod_row(i), 0, 0))]
    args = [h2, u, vt, th, e1, e2, x, mod3]
    if final:
        in_specs.append(pl.BlockSpec((1, d), lambda i, t, c: (0, 0)))
        args.append(final_g)
    return pl.pallas_call(
        functools.partial(_peer_kernel, final=final, d=d, ich=ich),
        grid=(b, l // tm, ne // ec),
        in_specs=in_specs,
        out_specs=pl.BlockSpec((1, tm, d), lambda i, t, c: (i, t, 0)),
        out_shape=jax.ShapeDtypeStruct((b, l, d), F32),
        scratch_shapes=[pltpu.VMEM((d, tm), F32), pltpu.VMEM((ec, tm), F32),
                        pltpu.VMEM((ec, tm), BF16), pltpu.VMEM((d, tm), BF16)],
        compiler_params=_cparams(("parallel", "parallel", "arbitrary")),
        name="peer_final" if final else "peer",
    )(*args)


def _rope_tables(l, dim, width):
    t = jnp.arange(l)
    rows = (t // GRID_W).astype(F32)
    cols = (t % GRID_W).astype(F32)
    half = dim // 2
    inv = jnp.power(ROPE_BASE, -jnp.arange(0, half, 2, dtype=F32) / half)
    ang = jnp.concatenate([rows[:, None] * inv, cols[:, None] * inv], axis=-1)
    pair = (np.arange(width) % dim) // 2
    sign = np.where(np.arange(width) % 2 == 0, -1.0, 1.0).astype(np.float32)
    return jnp.cos(ang)[:, pair], jnp.sin(ang)[:, pair] * sign


def _block_mixer_and_peer(x, ctx, mod3, layer, need_ctx, p, tables, final_g):
    b, l, d = x.shape
    lc = ctx.shape[1]
    row_x = lambda i: i
    row_c = lambda i: b
    lam_init = 0.8 - 0.6 * math.exp(-0.3 * layer)

    lat = _inproj(x, mod3, row_x, p["g1"], p["w_in"], tables, min(l, 512))
    cx = _inproj(ctx, mod3, row_c, p["g1"], p["w_in"], None, lc)
    naq, nak, nav, dfq, dfk, dfv, rtq, rtk, rtv, rtg = lat
    cnaq, cnak, cnav, cdfq, cdfk, cdfv, crtq, crtk, crtv, crtg = cx

    y_na = _na_latent(naq, nak, nav, cnak, cnav, p["rpb"], 4)
    y_df = _diff(dfq, cdfk, cdfv, dfk, dfv, p["lam"], p["subln"], lam_init, 256)
    y_rt, yc_rt = _retention(p["decay"], crtq, crtk, crtv, crtg, rtq, rtk, rtv, rtg)

    def channel_mix(xx, yna, ydf, yrt, row, tm, fg):
        xo, h2, st = _outproj(xx, yna, ydf, yrt, mod3, row, p["w_out"], p["g2"], p["wq"], p["sk"], tm)
        th, e1, e2 = _topk(st, min(tm, 256))
        return _peer(h2, p["u"], p["vt"], th, e1, e2, xo, mod3, row, fg, tm)

    x = channel_mix(x, y_na, y_df, y_rt, row_x, min(l, 512), final_g)
    if need_ctx:
        yc_na = _na_ctx(cnaq, cnak, cnav)
        yc_df = _diff(cdfq, cdfk, cdfv, None, None, p["lam"], p["subln"], lam_init, lc)
        flat = lambda t: t.reshape(1, b * lc, t.shape[-1])
        ctx = channel_mix(flat(ctx), flat(yc_na), flat(yc_df), flat(yc_rt), lambda i: b,
                          min(b * lc, 512), None).reshape(b, lc, d)
    return x, ctx


def kernel(x, c, ctx, c_ctx, w_ada, b_ada, norm1_g, w_in, na_rpb, diff_lambda, diff_subln_g,
           ret_decay_logit, w_out, norm2_g, peer_wq, peer_subkeys, peer_u, peer_v, final_g):
    b, l, d = x.shape
    depth = w_ada.shape[0]
    x = x.astype(F32)
    ctx = ctx.astype(F32)
    npad = -(-(b + 1) // 8) * 8
    cpad = jnp.zeros((npad, d), F32).at[:b].set(c.astype(F32)).at[b].set(c_ctx.astype(F32))
    mod = _ada(cpad, w_ada.astype(F32), b_ada.astype(F32))
    tables = _rope_tables(l, DIFF_DIM, DIFF_QK_W) + _rope_tables(l, RET_QK, RET_QK_W)
    for layer in range(depth):
        p = {
            "g1": norm1_g[layer].astype(F32).reshape(1, d),
            "g2": norm2_g[layer].astype(F32).reshape(1, d),
            "w_in": w_in[layer].astype(BF16),
            "rpb": na_rpb[layer],
            "lam": diff_lambda[layer].astype(F32),
            "subln": jnp.tile(diff_subln_g[layer].astype(F32), DIFF_HEADS).reshape(1, DIFF_W),
            "decay": ret_decay_logit[layer],
            "w_out": w_out[layer].astype(BF16),
            "wq": peer_wq[layer].astype(BF16),
            "sk": peer_subkeys[layer].astype(BF16).reshape(2 * PEER_HEADS, PEER_NKEYS, PEER_HALF),
            "u": peer_u[layer].astype(BF16),
            "vt": peer_v[layer].astype(BF16).T,
        }
        last = layer == depth - 1
        x, ctx = _block_mixer_and_peer(x, ctx, mod[layer].reshape(npad, 1, 6 * d), layer, not last, p,
                                       tables, final_g.astype(F32).reshape(1, d) if last else None)
    return x
```

```python
import functools
import math

import numpy as np
import jax
import jax.numpy as jnp
from jax import lax
from jax.experimental import pallas as pl
from jax.experimental.pallas import tpu as pltpu

F32 = jnp.float32
BF16 = jnp.bfloat16

GRID_W = 64
NORM_EPS = 1e-6
ROPE_BASE = 10000.0
NEG_INF = -1e30
LOG2E = math.log2(math.e)

NA_HEADS = 4
NA_DIM = 64
NA_WIN_ROWS = 8
NA_WIN_COLS = 16
DIFF_HEADS = 4
DIFF_DIM = 32
DIFF_VDIM = 64
RET_HEADS = 4
RET_QK = 64
RET_V = 128
RET_CHUNK = 128
RET_UNROLL = 8

NA_W = NA_HEADS * NA_DIM
DIFF_QK_W = DIFF_HEADS * 2 * DIFF_DIM
DIFF_W = DIFF_HEADS * DIFF_VDIM
RET_QK_W = RET_HEADS * RET_QK
RET_W = RET_HEADS * RET_V
MIX_W = NA_W + DIFF_W + RET_W
IN_COLS = 3 * NA_W + 2 * DIFF_QK_W + DIFF_W + 2 * RET_QK_W + 2 * RET_W

PEER_HEADS = 8
PEER_NKEYS = 128
PEER_KDIM = 256
PEER_TOPK = 16
PEER_HALF = PEER_KDIM // 2

C_NAQ, C_NAK, C_NAV = 0, 256, 512
C_DFQ, C_DFK, C_DFV = 768, 1024, 1280
C_RTQ, C_RTK, C_RTV, C_RTG = 1536, 1792, 2048, 2560

PEER_PAIRS = tuple((a, b) for a in range(PEER_TOPK) for b in range(PEER_TOPK // (a + 1)))
PEER_NCAND = 64

VMEM_LIMIT = 56 * 1024 * 1024


def _cparams(sem):
    return pltpu.CompilerParams(dimension_semantics=sem, vmem_limit_bytes=VMEM_LIMIT)


def _nt(a, b):
    return lax.dot_general(a, b, (((1,), (1,)), ((), ())), preferred_element_type=F32)


def _tn(a, b):
    return lax.dot_general(a, b, (((0,), (0,)), ((), ())), preferred_element_type=F32)


def _dot(a, b):
    return jnp.dot(a, b, preferred_element_type=F32)


def _rms(x):
    return x * lax.rsqrt(jnp.mean(x * x, axis=-1, keepdims=True) + NORM_EPS)


def _silu(x):
    return x * jax.nn.sigmoid(x)


def _ada_kernel(c_ref, w_ref, b_ref, o_ref):
    s = _silu(c_ref[...])
    o_ref[0] = jnp.dot(s, w_ref[0], preferred_element_type=F32,
                       precision=lax.Precision.HIGHEST) + b_ref[0]


def _ada(cpad, w_ada, b_ada):
    depth, d, n = w_ada.shape
    tn = 1536
    return pl.pallas_call(
        _ada_kernel,
        grid=(depth, n // tn),
        in_specs=[pl.BlockSpec((cpad.shape[0], d), lambda l, j: (0, 0)),
                  pl.BlockSpec((1, d, tn), lambda l, j: (l, 0, j)),
                  pl.BlockSpec((1, 1, tn), lambda l, j: (l, 0, j))],
        out_specs=pl.BlockSpec((1, cpad.shape[0], tn), lambda l, j: (l, 0, j)),
        out_shape=jax.ShapeDtypeStruct((depth, cpad.shape[0], n), F32),
        compiler_params=_cparams(("parallel", "parallel")),
        name="ada",
    )(cpad, w_ada, b_ada.reshape(depth, 1, n))


def _inproj_kernel(*refs, rope, d):
    if rope:
        (x_ref, mod_ref, g_ref, w_ref, cd_ref, sd_ref, cr_ref, sr_ref,
         naq, nak, nav, dfq, dfk, dfv, rtq, rtk, rtv, rtg) = refs
    else:
        (x_ref, mod_ref, g_ref, w_ref,
         naq, nak, nav, dfq, dfk, dfv, rtq, rtk, rtv, rtg) = refs
    mod = mod_ref[0]
    h = _rms(x_ref[0]) * g_ref[...] * (1.0 + mod[:, d:2 * d]) + mod[:, 0:d]
    hb = h.astype(BF16)

    def proj(c0, n):
        return _dot(hb, w_ref[:, c0:c0 + n])

    even_lane = (lax.broadcasted_iota(jnp.int32, (1, 256), 1) & 1) == 0

    def roped(c0, cos_ref, sin_ref):
        p = proj(c0, 256)
        if not rope:
            return p
        partner = jnp.where(even_lane, pltpu.roll(p, 255, 1), pltpu.roll(p, 1, 1))
        return p * cos_ref[...] + partner * sin_ref[...]

    naq[0] = (proj(C_NAQ, 256) * (NA_DIM ** -0.5 * LOG2E)).astype(BF16)
    nak[0] = proj(C_NAK, 256).astype(BF16)
    nav[0] = proj(C_NAV, 256).astype(BF16)
    dfq[0] = (roped(C_DFQ, cd_ref if rope else None, sd_ref if rope else None)
              * (DIFF_DIM ** -0.5 * LOG2E)).astype(BF16)
    dfk[0] = roped(C_DFK, cd_ref if rope else None, sd_ref if rope else None).astype(BF16)
    dfv[0] = proj(C_DFV, 256).astype(BF16)
    rtq[0] = roped(C_RTQ, cr_ref if rope else None, sr_ref if rope else None).astype(BF16)
    rtk[0] = (roped(C_RTK, cr_ref if rope else None, sr_ref if rope else None)
              * RET_QK ** -0.5).astype(BF16)
    rtv[0] = proj(C_RTV, 512).astype(BF16)
    rtg[0] = proj(C_RTG, 512)


def _inproj(x, mod3, mod_row, g1, w_in, tables, tm):
    b, l, d = x.shape
    rope = tables is not None
    in_specs = [pl.BlockSpec((1, tm, d), lambda i, t: (i, t, 0)),
                pl.BlockSpec((1, 1, 6 * d), lambda i, t: (mod_row(i), 0, 0)),
                pl.BlockSpec((1, d), lambda i, t: (0, 0)),
                pl.BlockSpec((d, IN_COLS), lambda i, t: (0, 0))]
    args = [x, mod3, g1, w_in]
    if rope:
        in_specs += [pl.BlockSpec((tm, 256), lambda i, t: (t, 0))] * 4
        args += list(tables)
    widths = (256, 256, 256, 256, 256, 256, 256, 256, 512, 512)
    dtypes = (BF16,) * 9 + (F32,)
    out_specs = [pl.BlockSpec((1, tm, w), lambda i, t: (i, t, 0)) for w in widths]
    out_shape = [jax.ShapeDtypeStruct((b, l, w), dt) for w, dt in zip(widths, dtypes)]
    return pl.pallas_call(
        functools.partial(_inproj_kernel, rope=rope, d=d),
        grid=(b, l // tm),
        in_specs=in_specs, out_specs=out_specs, out_shape=out_shape,
        compiler_params=_cparams(("parallel", "parallel")),
        name="inproj_rope" if rope else "inproj_ctx",
    )(*args)


def _na_plan(rows, rb):
    wr = min(NA_WIN_ROWS, rows)
    kr = min(rb + wr - 1, rows)
    nblk = rows // rb
    win0 = np.clip(np.arange(rows) - wr // 2, 0, rows - wr)
    ks = np.clip(np.arange(nblk) * rb - wr // 2, 0, rows - kr)
    pats, pat_of = [], np.zeros(nblk, np.int32)
    for g in range(nblk):
        r = g * rb + np.arange(rb)
        krow = ks[g] + np.arange(kr)
        dr = krow[None, :] - r[:, None] + (NA_WIN_ROWS - 1)
        ok = (krow[None, :] >= win0[r][:, None]) & (krow[None, :] < win0[r][:, None] + wr)
        assert ok.sum(axis=1).min() == wr
        key = (np.where(ok, dr, 0).tobytes(), ok.tobytes())
        for p, (k2, _, _) in enumerate(pats):
            if k2 == key:
                pat_of[g] = p
                break
        else:
            pat_of[g] = len(pats)
            pats.append((key, np.where(ok, dr, 0), ok))
    dr_idx = np.stack([p[1] for p in pats])
    row_ok = np.stack([p[2] for p in pats])
    qcol = np.arange(GRID_W)
    kcol = np.arange(GRID_W)
    cs = np.clip(qcol - NA_WIN_COLS // 2, 0, GRID_W - NA_WIN_COLS)
    col_ok = (kcol[None, :] >= cs[:, None]) & (kcol[None, :] < cs[:, None] + NA_WIN_COLS)
    dc_idx = np.clip(kcol[None, :] - qcol[:, None], 1 - NA_WIN_COLS, NA_WIN_COLS - 1) + NA_WIN_COLS - 1
    return kr, ks.astype(np.int32), pat_of, dr_idx, row_ok, dc_idx, col_ok


def _na_bias(rpb, plan, rb):
    kr, _, _, dr_idx, row_ok, dc_idx, col_ok = plan
    p = dr_idx.shape[0]
    nr, nc = 2 * NA_WIN_ROWS - 1, 2 * NA_WIN_COLS - 1
    row_hot = ((np.arange(nr) == dr_idx[..., None]) & row_ok[..., None]).astype(np.float32)
    col_hot = ((np.arange(nc) == dc_idx[..., None]) & col_ok[..., None]).astype(np.float32)
    bias = jnp.einsum("hrc,pqwr,xyc->hpqxwy", rpb.astype(F32), row_hot, col_hot,
                      precision=lax.Precision.HIGHEST)
    ok = row_ok[:, :, None, :, None] & col_ok[None, None, :, None, :]
    bias = jnp.where(ok[None], bias * LOG2E, NEG_INF)
    return bias.reshape(NA_HEADS, p, rb * GRID_W, kr * GRID_W)


def _na_kernel(ks_ref, pat_ref, q_ref, kx_ref, vx_ref, kc_ref, vc_ref, bias_ref, o_ref, *, nkeys):
    g = pl.program_id(1)
    q = q_ref[0]
    k0 = pl.multiple_of(ks_ref[g] * GRID_W, GRID_W)
    kw = kx_ref[0, pl.ds(k0, nkeys), :]
    vw = vx_ref[0, pl.ds(k0, nkeys), :]
    kc = kc_ref[0]
    vc = vc_ref[0]
    pat = pat_ref[g]
    lane = lax.broadcasted_iota(jnp.int32, (1, NA_W), 1)
    out = jnp.zeros((q.shape[0], NA_W), F32)
    for h in range(NA_HEADS):
        hm = (lane >= NA_DIM * h) & (lane < NA_DIM * (h + 1))
        qh = jnp.where(hm, q, jnp.zeros_like(q))
        sw = _nt(qh, kw) + bias_ref[h, pat]
        sc = _nt(qh, kc)
        m = jnp.maximum(jnp.max(sw, axis=-1, keepdims=True), jnp.max(sc, axis=-1, keepdims=True))
        pw = jnp.exp2(sw - m)
        pc = jnp.exp2(sc - m)
        den = jnp.sum(pw, axis=-1, keepdims=True) + jnp.sum(pc, axis=-1, keepdims=True)
        o = _dot(pw.astype(BF16), vw) + _dot(pc.astype(BF16), vc)
        out = out + jnp.where(hm, o / den, 0.0)
    o_ref[0] = out.astype(o_ref.dtype)


def _na_latent(q, kx, vx, kc, vc, rpb, rb):
    b, l, _ = q.shape
    lc = kc.shape[1]
    rows = l // GRID_W
    plan = _na_plan(rows, rb)
    kr, ks, pat_of = plan[0], plan[1], plan[2]
    bias = _na_bias(rpb, plan, rb)
    tq, nkeys = rb * GRID_W, kr * GRID_W
    grid_spec = pltpu.PrefetchScalarGridSpec(
        num_scalar_prefetch=2,
        grid=(b, rows // rb),
        in_specs=[pl.BlockSpec((1, tq, NA_W), lambda i, g, *_: (i, g, 0)),
                  pl.BlockSpec((1, l, NA_W), lambda i, g, *_: (i, 0, 0)),
                  pl.BlockSpec((1, l, NA_W), lambda i, g, *_: (i, 0, 0)),
                  pl.BlockSpec((1, lc, NA_W), lambda i, g, *_: (i, 0, 0)),
                  pl.BlockSpec((1, lc, NA_W), lambda i, g, *_: (i, 0, 0)),
                  pl.BlockSpec(bias.shape, lambda i, g, *_: (0, 0, 0, 0))],
        out_specs=pl.BlockSpec((1, tq, NA_W), lambda i, g, *_: (i, g, 0)),
    )
    return pl.pallas_call(
        functools.partial(_na_kernel, nkeys=nkeys),
        grid_spec=grid_spec,
        out_shape=jax.ShapeDtypeStruct((b, l, NA_W), BF16),
        compiler_params=_cparams(("parallel", "arbitrary")),
        name="na_latent",
    )(jnp.asarray(ks), jnp.asarray(pat_of), q, kx, vx, kc, vc, bias)


def _na_ctx_kernel(q_ref, kc_ref, vc_ref, o_ref):
    q = q_ref[0]
    kc = kc_ref[0]
    vc = vc_ref[0]
    lane = lax.broadcasted_iota(jnp.int32, (1, NA_W), 1)
    out = jnp.zeros((q.shape[0], NA_W), F32)
    for h in range(NA_HEADS):
        hm = (lane >= NA_DIM * h) & (lane < NA_DIM * (h + 1))
        qh = jnp.where(hm, q, jnp.zeros_like(q))
        sc = _nt(qh, kc)
        pc = jnp.exp2(sc - jnp.max(sc, axis=-1, keepdims=True))
        o = _dot(pc.astype(BF16), vc)
        out = out + jnp.where(hm, o / jnp.sum(pc, axis=-1, keepdims=True), 0.0)
    o_ref[0] = out.astype(o_ref.dtype)


def _na_ctx(q, kc, vc):
    b, lc, _ = q.shape
    spec = pl.BlockSpec((1, lc, NA_W), lambda i: (i, 0, 0))
    return pl.pallas_call(
        _na_ctx_kernel, grid=(b,), in_specs=[spec, spec, spec], out_specs=spec,
        out_shape=jax.ShapeDtypeStruct((b, lc, NA_W), BF16),
        compiler_params=_cparams(("parallel",)), name="na_ctx",
    )(q, kc, vc)


def _diff_kernel(*refs, lam_init, has_x):
    if has_x:
        q_ref, kc_ref, vc_ref, kx_ref, vx_ref, lam_ref, g_ref, o_ref = refs
    else:
        q_ref, kc_ref, vc_ref, lam_ref, g_ref, o_ref = refs
    q = q_ref[0]
    kc = kc_ref[0]
    vc = vc_ref[0]
    lp = lam_ref[...]
    lam = (jnp.exp(jnp.sum(lp[0:1] * lp[1:2], axis=-1, keepdims=True))
           - jnp.exp(jnp.sum(lp[2:3] * lp[3:4], axis=-1, keepdims=True)) + lam_init)
    lane = lax.broadcasted_iota(jnp.int32, (1, DIFF_W), 1)
    out = jnp.zeros((q.shape[0], DIFF_W), F32)
    def scores(h):
        res = []
        for c in range(2):
            lo = DIFF_VDIM * h + DIFF_DIM * c
            qm = jnp.where((lane >= lo) & (lane < lo + DIFF_DIM), q, jnp.zeros_like(q))
            res.append((_nt(qm, kc), _nt(qm, kx_ref[0]) if has_x else None))
        return res

    nxt = scores(0)
    for h in range(DIFF_HEADS):
        cur = nxt
        if h + 1 < DIFF_HEADS:
            nxt = scores(h + 1)
        parts = []
        for c in range(2):
            sc, sx = cur[c]
            m = jnp.max(sc, axis=-1, keepdims=True)
            if has_x:
                m = jnp.maximum(m, jnp.max(sx, axis=-1, keepdims=True))
            ec = jnp.exp2(sc - m)
            den = jnp.sum(ec, axis=-1, keepdims=True)
            ex = None
            if has_x:
                ex = jnp.exp2(sx - m)
                den = den + jnp.sum(ex, axis=-1, keepdims=True)
            parts.append((ec, ex, den))
        (ec0, ex0, den0), (ec1, ex1, den1) = parts
        rho = lam * den0 / den1
        o = _dot((ec0 - ec1 * rho).astype(BF16), vc)
        if has_x:
            o = o + _dot((ex0 - ex1 * rho).astype(BF16), vx_ref[0])
        hm = (lane >= DIFF_VDIM * h) & (lane < DIFF_VDIM * (h + 1))
        oh = jnp.where(hm, o / den0, 0.0)
        ms = jnp.sum(oh * oh, axis=-1, keepdims=True) * (1.0 / DIFF_VDIM)
        out = out + oh * lax.rsqrt(ms + NORM_EPS)
    o_ref[0] = (out * g_ref[...] * (1.0 - lam_init)).astype(o_ref.dtype)


def _diff(q, kc, vc, kx, vx, lam_p, g4, lam_init, tq):
    b, l, _ = q.shape
    lc = kc.shape[1]
    has_x = kx is not None
    cspec = pl.BlockSpec((1, lc, DIFF_W), lambda i, t: (i, 0, 0))
    in_specs = [pl.BlockSpec((1, tq, DIFF_W), lambda i, t: (i, t, 0)), cspec, cspec]
    args = [q, kc, vc]
    if has_x:
        xspec = pl.BlockSpec((1, kx.shape[1], DIFF_W), lambda i, t: (i, 0, 0))
        in_specs += [xspec, xspec]
        args += [kx, vx]
    in_specs += [pl.BlockSpec(lam_p.shape, lambda i, t: (0, 0)),
                 pl.BlockSpec((1, DIFF_W), lambda i, t: (0, 0))]
    args += [lam_p, g4]
    return pl.pallas_call(
        functools.partial(_diff_kernel, lam_init=lam_init, has_x=has_x),
        grid=(b, l // tq), in_specs=in_specs,
        out_specs=pl.BlockSpec((1, tq, DIFF_W), lambda i, t: (i, t, 0)),
        out_shape=jax.ShapeDtypeStruct((b, l, DIFF_W), BF16),
        compiler_params=_cparams(("parallel", "arbitrary")),
        name="diff_latent" if has_x else "diff_ctx",
    )(*args)


def _ret_kernel(dl_ref, cq_ref, ck_ref, cv_ref, cg_ref, xq_ref, xk_ref, xv_ref, xg_ref,
                yx_ref, yc_ref, s_ref, *, n_ctx, n_lat):
    h = pl.program_id(1)
    c = RET_CHUNK
    lane = lax.broadcasted_iota(jnp.int32, (1, RET_QK_W), 1)
    hm = (lane >= RET_QK * h) & (lane < RET_QK * (h + 1))
    row = lax.broadcasted_iota(jnp.int32, (c, c), 0).astype(F32)
    col = lax.broadcasted_iota(jnp.int32, (c, c), 1).astype(F32)
    roww = lax.broadcasted_iota(jnp.int32, (c, RET_QK_W), 0).astype(F32)

    def log_sigmoid(d):
        x = jnp.full((1, 1), dl_ref[d, h], F32)
        return jnp.minimum(x, 0.0) - jnp.log1p(jnp.exp(-jnp.abs(x)))

    lgf, lgb = log_sigmoid(0), log_sigmoid(1)
    diff = row - col
    fwd = (jnp.where(diff >= 0, jnp.exp(lgf * jnp.maximum(diff, 0.0)), 0.0),
           jnp.exp(lgf * (roww + 1.0)),
           jnp.exp(lgf * (c - 1.0 - roww)),
           jnp.exp(lgf * c))
    bwd = (jnp.where(diff <= 0, jnp.exp(lgb * jnp.maximum(-diff, 0.0)), 0.0),
           jnp.exp(lgb * (c - roww)),
           jnp.exp(lgb * roww),
           jnp.exp(lgb * c))

    def step(q_ref, k_ref, v_ref, g_ref, out_ref, ci, mats, sdir, second):
        dm, qd, kd, cd = mats
        sl = pl.ds(pl.multiple_of(ci * c, c), c)
        q = q_ref[0, sl, :]
        q = jnp.where(hm, q, jnp.zeros_like(q))
        k = k_ref[0, sl, :]
        v = v_ref[0, sl, :]
        att = _nt(q, k) * dm
        s = s_ref[sdir]
        o = _dot(att.astype(BF16), v) + _dot((q.astype(F32) * qd).astype(BF16), s.astype(BF16))
        s_ref[sdir] = cd * s + _tn((k.astype(F32) * kd).astype(BF16), v)
        if second:
            tot = out_ref[0, sl, :] + o
            out_ref[0, sl, :] = _rms(tot) * _silu(g_ref[0, sl, :])
        else:
            out_ref[0, sl, :] = o

    def both(refs, n, i, second):
        step(*refs, i, fwd, 0, second)
        step(*refs, n - 1 - i, bwd, 1, second)

    s_ref[...] = jnp.zeros_like(s_ref)
    crefs = (cq_ref, ck_ref, cv_ref, cg_ref, yc_ref)
    xrefs = (xq_ref, xk_ref, xv_ref, xg_ref, yx_ref)
    for i in range(n_ctx):
        both(crefs, n_ctx, i, i >= n_ctx // 2)
    for second in (False, True):
        def body(i, carry, second=second):
            both(xrefs, n_lat, i, second)
            return carry

        lax.fori_loop(n_lat // 2 if second else 0, n_lat if second else n_lat // 2, body, 0,
                      unroll=RET_UNROLL if (n_lat // 2) % RET_UNROLL == 0 else 1)


def _retention(decay_logit, cq, ck, cv, cg, xq, xk, xv, xg):
    b, l, _ = xq.shape
    lc = cq.shape[1]

    def qk_spec(n):
        return pl.BlockSpec((1, n, RET_QK_W), lambda i, h: (i, 0, 0))

    def v_spec(n):
        return pl.BlockSpec((1, n, RET_V), lambda i, h: (i, 0, h))

    return pl.pallas_call(
        functools.partial(_ret_kernel, n_ctx=lc // RET_CHUNK, n_lat=l // RET_CHUNK),
        grid=(b, RET_HEADS),
        in_specs=[pl.BlockSpec(memory_space=pltpu.SMEM),
                  qk_spec(lc), qk_spec(lc), v_spec(lc), v_spec(lc),
                  qk_spec(l), qk_spec(l), v_spec(l), v_spec(l)],
        out_specs=[v_spec(l), v_spec(lc)],
        out_shape=[jax.ShapeDtypeStruct((b, l, RET_W), F32), jax.ShapeDtypeStruct((b, lc, RET_W), F32)],
        scratch_shapes=[pltpu.VMEM((2, RET_QK_W, RET_V), F32)],
        compiler_params=_cparams(("parallel", "arbitrary")),
        name="retention",
    )(decay_logit.astype(F32), cq, ck, cv, cg, xq, xk, xv, xg)


def _outproj_kernel(x_ref, yna_ref, ydf_ref, yrt_ref, mod_ref, wo_ref, g2_ref, wq_ref, sk_ref,
                    xo_ref, h2_ref, st_ref, *, d):
    y = (_dot(yna_ref[0], wo_ref[0:NA_W, :])
         + _dot(ydf_ref[0], wo_ref[NA_W:NA_W + DIFF_W, :])
         + _dot(yrt_ref[0].astype(BF16), wo_ref[NA_W + DIFF_W:MIX_W, :]))
    mod = mod_ref[0]
    x = x_ref[0] + mod[:, 2 * d:3 * d] * y
    xo_ref[0] = x
    h2 = (_rms(x) * g2_ref[...] * (1.0 + mod[:, 4 * d:5 * d]) + mod[:, 3 * d:4 * d]).astype(BF16)
    h2_ref[0] = h2
    qp = _dot(h2, wq_ref[...]).astype(BF16)
    for hp in range(2 * PEER_HEADS):
        st_ref[0, hp] = _nt(sk_ref[hp], qp[:, hp * PEER_HALF:(hp + 1) * PEER_HALF])


def _outproj(x, yna, ydf, yrt, mod3, mod_row, wo, g2, wq, sk, tm):
    b, l, d = x.shape
    nhp = 2 * PEER_HEADS

    def tok(w):
        return pl.BlockSpec((1, tm, w), lambda i, t: (i, t, 0))

    return pl.pallas_call(
        functools.partial(_outproj_kernel, d=d),
        grid=(b, l // tm),
        in_specs=[tok(d), tok(NA_W), tok(DIFF_W), tok(RET_W),
                  pl.BlockSpec((1, 1, 6 * d), lambda i, t: (mod_row(i), 0, 0)),
                  pl.BlockSpec(wo.shape, lambda i, t: (0, 0)),
                  pl.BlockSpec((1, d), lambda i, t: (0, 0)),
                  pl.BlockSpec(wq.shape, lambda i, t: (0, 0)),
                  pl.BlockSpec(sk.shape, lambda i, t: (0, 0, 0))],
        out_specs=[tok(d), tok(d),
                   pl.BlockSpec((1, nhp, PEER_NKEYS, tm), lambda i, t: (i, 0, 0, t))],
        out_shape=[jax.ShapeDtypeStruct((b, l, d), F32), jax.ShapeDtypeStruct((b, l, d), BF16),
                   jax.ShapeDtypeStruct((b, nhp, PEER_NKEYS, l), F32)],
        compiler_params=_cparams(("parallel", "parallel")),
        name="outproj",
    )(x, yna, ydf, yrt, mod3, wo, g2, wq, sk)


def _sort_network(n):
    pairs = []
    p = 1
    while p < n:
        k = p
        while k >= 1:
            for j in range(k % p, n - k, 2 * k):
                for i in range(min(k, n - j - k)):
                    if (i + j) // (2 * p) == (i + j + k) // (2 * p):
                        pairs.append((i + j, i + j + k))
            k //= 2
        p *= 2
    return pairs


def _topk_kernel(st_ref, th_ref, e1_ref, e2_ref, cand_ref, t2_ref):
    neg = -jnp.inf

    def blocks(x):
        return [x[r * 8:(r + 1) * 8, :] for r in range(x.shape[0] // 8)]

    def sort_columns(vs):
        vs = list(vs)
        for i, j in _sort_network(len(vs)):
            vs[i], vs[j] = jnp.maximum(vs[i], vs[j]), jnp.minimum(vs[i], vs[j])
        return vs

    def top_rows(x):
        vs = sort_columns(blocks(x))
        rows = []
        for k in range(PEER_TOPK):
            m = jnp.max(vs[0], axis=0, keepdims=True)
            rows.append(m)
            depth = min(len(vs), PEER_TOPK - 1 - k)
            if depth > 0:
                hit = vs[0] == m
                for r in range(depth):
                    below = vs[r + 1] if r + 1 < len(vs) else neg
                    vs[r] = jnp.where(hit, below, vs[r])
        return rows

    cand_ref[len(PEER_PAIRS):, :] = jnp.full((PEER_NCAND - len(PEER_PAIRS), cand_ref.shape[1]), neg, F32)
    for h in range(PEER_HEADS):
        s1 = st_ref[0, 2 * h]
        s2 = st_ref[0, 2 * h + 1]
        a1 = s1 - jnp.max(s1, axis=0, keepdims=True)
        a2 = s2 - jnp.max(s2, axis=0, keepdims=True)
        t1 = top_rows(a1)
        t2 = top_rows(a2)
        for b in range(PEER_TOPK):
            t2_ref[b:b + 1, :] = t2[b]
        t2s = t2_ref[...]
        off = 0
        for a in range(PEER_TOPK):
            n = PEER_TOPK // (a + 1)
            cand_ref[off:off + n, :] = (t1[a] + t2s)[0:n]
            off += n
        cand = cand_ref[...]
        tau = top_rows(cand)[-1]
        z = jnp.sum(jnp.where(cand >= tau, jnp.exp(cand), 0.0), axis=0, keepdims=True)
        th_ref[0, h] = jnp.exp(tau - a1)
        e1_ref[0, h] = jnp.exp(a1) * (0.5 / z)
        e2_ref[0, h] = jnp.exp(a2)


def _topk(st, tl):
    b, nhp, nk, l = st.shape
    hspec = pl.BlockSpec((1, PEER_HEADS, nk, tl), lambda i, t: (i, 0, 0, t))
    hshape = jax.ShapeDtypeStruct((b, PEER_HEADS, nk, l), F32)
    return pl.pallas_call(
        _topk_kernel,
        grid=(b, l // tl),
        in_specs=[pl.BlockSpec((1, nhp, nk, tl), lambda i, t: (i, 0, 0, t))],
        out_specs=[hspec, hspec, hspec],
        out_shape=[hshape, hshape, hshape],
        scratch_shapes=[pltpu.VMEM((PEER_NCAND, tl), F32), pltpu.VMEM((PEER_TOPK, tl), F32)],
        compiler_params=_cparams(("parallel", "parallel")),
        name="peer_topk",
    )(st)


def _peer_kernel(*refs, final, d, ich):
    if final:
        h2_ref, u_ref, vt_ref, th_ref, e1_ref, e2_ref, x_ref, mod_ref, fg_ref, o_ref = refs[:10]
    else:
        h2_ref, u_ref, vt_ref, th_ref, e1_ref, e2_ref, x_ref, mod_ref, o_ref = refs[:9]
    acc_ref, at_ref, gw_ref, h2t_ref = refs[-4:]
    c = pl.program_id(2)
    tm = h2_ref.shape[1]
    nt = PEER_NKEYS // 8

    @pl.when(c == 0)
    def _():
        acc_ref[...] = jnp.zeros_like(acc_ref)
        h2t_ref[...] = h2_ref[0].T

    at_ref[...] = _dot(u_ref[...], h2t_ref[...])
    for ii in range(ich):
        for lg in range(tm // 128):
            ls = slice(lg * 128, (lg + 1) * 128)
            w = [None] * nt
            for h in range(PEER_HEADS):
                th = jnp.broadcast_to(th_ref[0, h, ii:ii + 1, ls], (8, 128))
                e1 = jnp.broadcast_to(e1_ref[0, h, ii:ii + 1, ls], (8, 128))
                for jt in range(nt):
                    e2 = e2_ref[0, h, jt * 8:(jt + 1) * 8, ls]
                    term = jnp.where(e2 >= th, e2, 0.0) * e1
                    w[jt] = term if w[jt] is None else w[jt] + term
            for jt in range(0, nt, 2):
                r0 = ii * PEER_NKEYS + jt * 8
                a = at_ref[r0:r0 + 16, ls]
                g = a * (1.0 + lax.erf(a * math.sqrt(0.5))) * jnp.concatenate(w[jt:jt + 2], axis=0)
                gw_ref[r0:r0 + 16, ls] = g.astype(BF16)
    acc_ref[...] += _tn(gw_ref[...], vt_ref[...])

    @pl.when(c == pl.num_programs(2) - 1)
    def _():
        mod = mod_ref[0]
        xo = x_ref[0] + mod[:, 5 * d:6 * d] * acc_ref[...]
        if final:
            xo = _rms(xo) * fg_ref[...]
        o_ref[0] = xo


def _peer(h2, u, vt, th, e1, e2, x, mod3, mod_row, final_g, tm, ich=16):
    b, l, d = x.shape
    ne = u.shape[0]
    ec = ich * PEER_NKEYS
    final = final_g is not None
    ispec = pl.BlockSpec((1, PEER_HEADS, ich, tm), lambda i, t, c: (i, 0, c, t))
    in_specs = [pl.BlockSpec((1, tm, d), lambda i, t, c: (i, t, 0)),
                pl.BlockSpec((ec, d), lambda i, t, c: (c, 0)),
                pl.BlockSpec((ec, d), lambda i, t, c: (c, 0)),
                ispec, ispec,
                pl.BlockSpec((1, PEER_HEADS, PEER_NKEYS, tm), lambda i, t, c: (i, 0, 0, t)),
                pl.BlockSpec((1, tm, d), lambda i, t, c: (i, t, 0)),
                pl.BlockSpec((1, 1, 6 * d), lambda i, t, c: (mod_row(i), 0, 0))]
    args = [h2, u, vt, th, e1, e2, x, mod3]
    if final:
        in_specs.append(pl.BlockSpec((1, d), lambda i, t, c: (0, 0)))
        args.append(final_g)
    return pl.pallas_call(
        functools.partial(_peer_kernel, final=final, d=d, ich=ich),
        grid=(b, l // tm, ne // ec),
        in_specs=in_specs,
        out_specs=pl.BlockSpec((1, tm, d), lambda i, t, c: (i, t, 0)),
        out_shape=jax.ShapeDtypeStruct((b, l, d), F32),
        scratch_shapes=[pltpu.VMEM((tm, d), F32), pltpu.VMEM((ec, tm), F32),
                        pltpu.VMEM((ec, tm), BF16), pltpu.VMEM((d, tm), BF16)],
        compiler_params=_cparams(("parallel", "parallel", "arbitrary")),
        name="peer_final" if final else "peer",
    )(*args)


def _rope_tables(l, dim, width):
    t = jnp.arange(l)
    rows = (t // GRID_W).astype(F32)
    cols = (t % GRID_W).astype(F32)
    half = dim // 2
    inv = jnp.power(ROPE_BASE, -jnp.arange(0, half, 2, dtype=F32) / half)
    ang = jnp.concatenate([rows[:, None] * inv, cols[:, None] * inv], axis=-1)
    pair = (np.arange(width) % dim) // 2
    sign = np.where(np.arange(width) % 2 == 0, -1.0, 1.0).astype(np.float32)
    return jnp.cos(ang)[:, pair], jnp.sin(ang)[:, pair] * sign


def _block_mixer_and_peer(x, ctx, mod3, layer, need_ctx, p, tables, final_g):
    b, l, d = x.shape
    lc = ctx.shape[1]
    row_x = lambda i: i
    row_c = lambda i: b
    lam_init = 0.8 - 0.6 * math.exp(-0.3 * layer)

    lat = _inproj(x, mod3, row_x, p["g1"], p["w_in"], tables, min(l, 512))
    cx = _inproj(ctx, mod3, row_c, p["g1"], p["w_in"], None, lc)
    naq, nak, nav, dfq, dfk, dfv, rtq, rtk, rtv, rtg = lat
    cnaq, cnak, cnav, cdfq, cdfk, cdfv, crtq, crtk, crtv, crtg = cx

    y_na = _na_latent(naq, nak, nav, cnak, cnav, p["rpb"], 4)
    y_df = _diff(dfq, cdfk, cdfv, dfk, dfv, p["lam"], p["subln"], lam_init, 256)
    y_rt, yc_rt = _retention(p["decay"], crtq, crtk, crtv, crtg, rtq, rtk, rtv, rtg)

    def channel_mix(xx, yna, ydf, yrt, row, tm, fg):
        xo, h2, st = _outproj(xx, yna, ydf, yrt, mod3, row, p["w_out"], p["g2"], p["wq"], p["sk"], tm)
        th, e1, e2 = _topk(st, min(tm, 256))
        return _peer(h2, p["u"], p["vt"], th, e1, e2, xo, mod3, row, fg, tm)

    x = channel_mix(x, y_na, y_df, y_rt, row_x, min(l, 512), final_g)
    if need_ctx:
        yc_na = _na_ctx(cnaq, cnak, cnav)
        yc_df = _diff(cdfq, cdfk, cdfv, None, None, p["lam"], p["subln"], lam_init, lc)
        flat = lambda t: t.reshape(1, b * lc, t.shape[-1])
        ctx = channel_mix(flat(ctx), flat(yc_na), flat(yc_df), flat(yc_rt), lambda i: b,
                          min(b * lc, 512), None).reshape(b, lc, d)
    return x, ctx


def kernel(x, c, ctx, c_ctx, w_ada, b_ada, norm1_g, w_in, na_rpb, diff_lambda, diff_subln_g,
           ret_decay_logit, w_out, norm2_g, peer_wq, peer_subkeys, peer_u, peer_v, final_g):
    b, l, d = x.shape
    depth = w_ada.shape[0]
    x = x.astype(F32)
    ctx = ctx.astype(F32)
    npad = -(-(b + 1) // 8) * 8
    cpad = jnp.zeros((npad, d), F32).at[:b].set(c.astype(F32)).at[b].set(c_ctx.astype(F32))
    mod = _ada(cpad, w_ada.astype(F32), b_ada.astype(F32))
    tables = _rope_tables(l, DIFF_DIM, DIFF_QK_W) + _rope_tables(l, RET_QK, RET_QK_W)
    for layer in range(depth):
        p = {
            "g1": norm1_g[layer].astype(F32).reshape(1, d),
            "g2": norm2_g[layer].astype(F32).reshape(1, d),
            "w_in": w_in[layer].astype(BF16),
            "rpb": na_rpb[layer],
            "lam": diff_lambda[layer].astype(F32),
            "subln": jnp.tile(diff_subln_g[layer].astype(F32), DIFF_HEADS).reshape(1, DIFF_W),
            "decay": ret_decay_logit[layer],
            "w_out": w_out[layer].astype(BF16),
            "wq": peer_wq[layer].astype(BF16),
            "sk": peer_subkeys[layer].astype(BF16).reshape(2 * PEER_HEADS, PEER_NKEYS, PEER_HALF),
            "u": peer_u[layer].astype(BF16),
            "vt": peer_v[layer].astype(BF16),
        }
        last = layer == depth - 1
        x, ctx = _block_mixer_and_peer(x, ctx, mod[layer].reshape(npad, 1, 6 * d), layer, not last, p,
                                       tables, final_g.astype(F32).reshape(1, d) if last else None)
    return x
```

```python
import functools
import math

import numpy as np
import jax
import jax.numpy as jnp
from jax import lax
from jax.experimental import pallas as pl
from jax.experimental.pallas import tpu as pltpu

F32 = jnp.float32
BF16 = jnp.bfloat16

GRID_W = 64
NORM_EPS = 1e-6
ROPE_BASE = 10000.0
NEG_INF = -1e30
LOG2E = math.log2(math.e)

NA_HEADS = 4
NA_DIM = 64
NA_WIN_ROWS = 8
NA_WIN_COLS = 16
DIFF_HEADS = 4
DIFF_DIM = 32
DIFF_VDIM = 64
RET_HEADS = 4
RET_QK = 64
RET_V = 128
RET_CHUNK = 128
RET_UNROLL = 8

NA_W = NA_HEADS * NA_DIM
DIFF_QK_W = DIFF_HEADS * 2 * DIFF_DIM
DIFF_W = DIFF_HEADS * DIFF_VDIM
RET_QK_W = RET_HEADS * RET_QK
RET_W = RET_HEADS * RET_V
MIX_W = NA_W + DIFF_W + RET_W
IN_COLS = 3 * NA_W + 2 * DIFF_QK_W + DIFF_W + 2 * RET_QK_W + 2 * RET_W

PEER_HEADS = 8
PEER_NKEYS = 128
PEER_KDIM = 256
PEER_TOPK = 16
PEER_HALF = PEER_KDIM // 2

C_NAQ, C_NAK, C_NAV = 0, 256, 512
C_DFQ, C_DFK, C_DFV = 768, 1024, 1280
C_RTQ, C_RTK, C_RTV, C_RTG = 1536, 1792, 2048, 2560

PEER_PAIRS = tuple((a, b) for a in range(PEER_TOPK) for b in range(PEER_TOPK // (a + 1)))
PEER_NCAND = 64

VMEM_LIMIT = 56 * 1024 * 1024


def _cparams(sem):
    return pltpu.CompilerParams(dimension_semantics=sem, vmem_limit_bytes=VMEM_LIMIT)


def _nt(a, b):
    return lax.dot_general(a, b, (((1,), (1,)), ((), ())), preferred_element_type=F32)


def _tn(a, b):
    return lax.dot_general(a, b, (((0,), (0,)), ((), ())), preferred_element_type=F32)


def _dot(a, b):
    return jnp.dot(a, b, preferred_element_type=F32)


def _rms(x):
    return x * lax.rsqrt(jnp.mean(x * x, axis=-1, keepdims=True) + NORM_EPS)


def _silu(x):
    return x * jax.nn.sigmoid(x)


def _ada_kernel(c_ref, w_ref, b_ref, o_ref):
    s = _silu(c_ref[...])
    o_ref[0] = jnp.dot(s, w_ref[0], preferred_element_type=F32,
                       precision=lax.Precision.HIGHEST) + b_ref[0]


def _ada(cpad, w_ada, b_ada):
    depth, d, n = w_ada.shape
    tn = 1536
    return pl.pallas_call(
        _ada_kernel,
        grid=(depth, n // tn),
        in_specs=[pl.BlockSpec((cpad.shape[0], d), lambda l, j: (0, 0)),
                  pl.BlockSpec((1, d, tn), lambda l, j: (l, 0, j)),
                  pl.BlockSpec((1, 1, tn), lambda l, j: (l, 0, j))],
        out_specs=pl.BlockSpec((1, cpad.shape[0], tn), lambda l, j: (l, 0, j)),
        out_shape=jax.ShapeDtypeStruct((depth, cpad.shape[0], n), F32),
        compiler_params=_cparams(("parallel", "parallel")),
        name="ada",
    )(cpad, w_ada, b_ada.reshape(depth, 1, n))


def _inproj_kernel(*refs, rope, d):
    if rope:
        (x_ref, mod_ref, g_ref, w_ref, cd_ref, sd_ref, cr_ref, sr_ref,
         naq, nak, nav, dfq, dfk, dfv, rtq, rtk, rtv, rtg) = refs
    else:
        (x_ref, mod_ref, g_ref, w_ref,
         naq, nak, nav, dfq, dfk, dfv, rtq, rtk, rtv, rtg) = refs
    mod = mod_ref[0]
    h = _rms(x_ref[0]) * g_ref[...] * (1.0 + mod[:, d:2 * d]) + mod[:, 0:d]
    hb = h.astype(BF16)

    def proj(c0, n):
        return _dot(hb, w_ref[:, c0:c0 + n])

    even_lane = (lax.broadcasted_iota(jnp.int32, (1, 256), 1) & 1) == 0

    def roped(c0, cos_ref, sin_ref):
        p = proj(c0, 256)
        if not rope:
            return p
        partner = jnp.where(even_lane, pltpu.roll(p, 255, 1), pltpu.roll(p, 1, 1))
        return p * cos_ref[...] + partner * sin_ref[...]

    naq[0] = (proj(C_NAQ, 256) * (NA_DIM ** -0.5 * LOG2E)).astype(BF16)
    nak[0] = proj(C_NAK, 256).astype(BF16)
    nav[0] = proj(C_NAV, 256).astype(BF16)
    dfq[0] = (roped(C_DFQ, cd_ref if rope else None, sd_ref if rope else None)
              * (DIFF_DIM ** -0.5 * LOG2E)).astype(BF16)
    dfk[0] = roped(C_DFK, cd_ref if rope else None, sd_ref if rope else None).astype(BF16)
    dfv[0] = proj(C_DFV, 256).astype(BF16)
    rtq[0] = roped(C_RTQ, cr_ref if rope else None, sr_ref if rope else None).astype(BF16)
    rtk[0] = (roped(C_RTK, cr_ref if rope else None, sr_ref if rope else None)
              * RET_QK ** -0.5).astype(BF16)
    rtv[0] = proj(C_RTV, 512).astype(BF16)
    rtg[0] = proj(C_RTG, 512)


def _inproj(x, mod3, mod_row, g1, w_in, tables, tm):
    b, l, d = x.shape
    rope = tables is not None
    in_specs = [pl.BlockSpec((1, tm, d), lambda i, t: (i, t, 0)),
                pl.BlockSpec((1, 1, 6 * d), lambda i, t: (mod_row(i), 0, 0)),
                pl.BlockSpec((1, d), lambda i, t: (0, 0)),
                pl.BlockSpec((d, IN_COLS), lambda i, t: (0, 0))]
    args = [x, mod3, g1, w_in]
    if rope:
        in_specs += [pl.BlockSpec((tm, 256), lambda i, t: (t, 0))] * 4
        args += list(tables)
    widths = (256, 256, 256, 256, 256, 256, 256, 256, 512, 512)
    dtypes = (BF16,) * 9 + (F32,)
    out_specs = [pl.BlockSpec((1, tm, w), lambda i, t: (i, t, 0)) for w in widths]
    out_shape = [jax.ShapeDtypeStruct((b, l, w), dt) for w, dt in zip(widths, dtypes)]
    return pl.pallas_call(
        functools.partial(_inproj_kernel, rope=rope, d=d),
        grid=(b, l // tm),
        in_specs=in_specs, out_specs=out_specs, out_shape=out_shape,
        compiler_params=_cparams(("parallel", "parallel")),
        name="inproj_rope" if rope else "inproj_ctx",
    )(*args)


def _na_plan(rows, rb):
    wr = min(NA_WIN_ROWS, rows)
    kr = min(rb + wr - 1, rows)
    nblk = rows // rb
    win0 = np.clip(np.arange(rows) - wr // 2, 0, rows - wr)
    ks = np.clip(np.arange(nblk) * rb - wr // 2, 0, rows - kr)
    pats, pat_of = [], np.zeros(nblk, np.int32)
    for g in range(nblk):
        r = g * rb + np.arange(rb)
        krow = ks[g] + np.arange(kr)
        dr = krow[None, :] - r[:, None] + (NA_WIN_ROWS - 1)
        ok = (krow[None, :] >= win0[r][:, None]) & (krow[None, :] < win0[r][:, None] + wr)
        assert ok.sum(axis=1).min() == wr
        key = (np.where(ok, dr, 0).tobytes(), ok.tobytes())
        for p, (k2, _, _) in enumerate(pats):
            if k2 == key:
                pat_of[g] = p
                break
        else:
            pat_of[g] = len(pats)
            pats.append((key, np.where(ok, dr, 0), ok))
    dr_idx = np.stack([p[1] for p in pats])
    row_ok = np.stack([p[2] for p in pats])
    qcol = np.arange(GRID_W)
    kcol = np.arange(GRID_W)
    cs = np.clip(qcol - NA_WIN_COLS // 2, 0, GRID_W - NA_WIN_COLS)
    col_ok = (kcol[None, :] >= cs[:, None]) & (kcol[None, :] < cs[:, None] + NA_WIN_COLS)
    dc_idx = np.clip(kcol[None, :] - qcol[:, None], 1 - NA_WIN_COLS, NA_WIN_COLS - 1) + NA_WIN_COLS - 1
    return kr, ks.astype(np.int32), pat_of, dr_idx, row_ok, dc_idx, col_ok


def _na_bias(rpb, plan, rb):
    kr, _, _, dr_idx, row_ok, dc_idx, col_ok = plan
    p = dr_idx.shape[0]
    nr, nc = 2 * NA_WIN_ROWS - 1, 2 * NA_WIN_COLS - 1
    row_hot = ((np.arange(nr) == dr_idx[..., None]) & row_ok[..., None]).astype(np.float32)
    col_hot = ((np.arange(nc) == dc_idx[..., None]) & col_ok[..., None]).astype(np.float32)
    bias = jnp.einsum("hrc,pqwr,xyc->hpqxwy", rpb.astype(F32), row_hot, col_hot,
                      precision=lax.Precision.HIGHEST)
    ok = row_ok[:, :, None, :, None] & col_ok[None, None, :, None, :]
    bias = jnp.where(ok[None], bias * LOG2E, NEG_INF)
    return bias.reshape(NA_HEADS, p, rb * GRID_W, kr * GRID_W)


def _na_kernel(ks_ref, pat_ref, q_ref, kx_ref, vx_ref, kc_ref, vc_ref, bias_ref, o_ref, *, nkeys):
    g = pl.program_id(1)
    q = q_ref[0]
    k0 = pl.multiple_of(ks_ref[g] * GRID_W, GRID_W)
    kw = kx_ref[0, pl.ds(k0, nkeys), :]
    vw = vx_ref[0, pl.ds(k0, nkeys), :]
    kc = kc_ref[0]
    vc = vc_ref[0]
    pat = pat_ref[g]
    lane = lax.broadcasted_iota(jnp.int32, (1, NA_W), 1)
    out = jnp.zeros((q.shape[0], NA_W), F32)
    for h in range(NA_HEADS):
        hm = (lane >= NA_DIM * h) & (lane < NA_DIM * (h + 1))
        qh = jnp.where(hm, q, jnp.zeros_like(q))
        sw = _nt(qh, kw) + bias_ref[h, pat]
        sc = _nt(qh, kc)
        m = jnp.maximum(jnp.max(sw, axis=-1, keepdims=True), jnp.max(sc, axis=-1, keepdims=True))
        pw = jnp.exp2(sw - m)
        pc = jnp.exp2(sc - m)
        den = jnp.sum(pw, axis=-1, keepdims=True) + jnp.sum(pc, axis=-1, keepdims=True)
        o = _dot(pw.astype(BF16), vw) + _dot(pc.astype(BF16), vc)
        out = out + jnp.where(hm, o / den, 0.0)
    o_ref[0] = out.astype(o_ref.dtype)


def _na_latent(q, kx, vx, kc, vc, rpb, rb):
    b, l, _ = q.shape
    lc = kc.shape[1]
    rows = l // GRID_W
    plan = _na_plan(rows, rb)
    kr, ks, pat_of = plan[0], plan[1], plan[2]
    bias = _na_bias(rpb, plan, rb)
    tq, nkeys = rb * GRID_W, kr * GRID_W
    grid_spec = pltpu.PrefetchScalarGridSpec(
        num_scalar_prefetch=2,
        grid=(b, rows // rb),
        in_specs=[pl.BlockSpec((1, tq, NA_W), lambda i, g, *_: (i, g, 0)),
                  pl.BlockSpec((1, l, NA_W), lambda i, g, *_: (i, 0, 0)),
                  pl.BlockSpec((1, l, NA_W), lambda i, g, *_: (i, 0, 0)),
                  pl.BlockSpec((1, lc, NA_W), lambda i, g, *_: (i, 0, 0)),
                  pl.BlockSpec((1, lc, NA_W), lambda i, g, *_: (i, 0, 0)),
                  pl.BlockSpec(bias.shape, lambda i, g, *_: (0, 0, 0, 0))],
        out_specs=pl.BlockSpec((1, tq, NA_W), lambda i, g, *_: (i, g, 0)),
    )
    return pl.pallas_call(
        functools.partial(_na_kernel, nkeys=nkeys),
        grid_spec=grid_spec,
        out_shape=jax.ShapeDtypeStruct((b, l, NA_W), BF16),
        compiler_params=_cparams(("parallel", "arbitrary")),
        name="na_latent",
    )(jnp.asarray(ks), jnp.asarray(pat_of), q, kx, vx, kc, vc, bias)


def _na_ctx_kernel(q_ref, kc_ref, vc_ref, o_ref):
    q = q_ref[0]
    kc = kc_ref[0]
    vc = vc_ref[0]
    lane = lax.broadcasted_iota(jnp.int32, (1, NA_W), 1)
    out = jnp.zeros((q.shape[0], NA_W), F32)
    for h in range(NA_HEADS):
        hm = (lane >= NA_DIM * h) & (lane < NA_DIM * (h + 1))
        qh = jnp.where(hm, q, jnp.zeros_like(q))
        sc = _nt(qh, kc)
        pc = jnp.exp2(sc - jnp.max(sc, axis=-1, keepdims=True))
        o = _dot(pc.astype(BF16), vc)
        out = out + jnp.where(hm, o / jnp.sum(pc, axis=-1, keepdims=True), 0.0)
    o_ref[0] = out.astype(o_ref.dtype)


def _na_ctx(q, kc, vc):
    b, lc, _ = q.shape
    spec = pl.BlockSpec((1, lc, NA_W), lambda i: (i, 0, 0))
    return pl.pallas_call(
        _na_ctx_kernel, grid=(b,), in_specs=[spec, spec, spec], out_specs=spec,
        out_shape=jax.ShapeDtypeStruct((b, lc, NA_W), BF16),
        compiler_params=_cparams(("parallel",)), name="na_ctx",
    )(q, kc, vc)


def _diff_kernel(*refs, lam_init, has_x):
    if has_x:
        q_ref, kc_ref, vc_ref, kx_ref, vx_ref, lam_ref, g_ref, o_ref = refs
    else:
        q_ref, kc_ref, vc_ref, lam_ref, g_ref, o_ref = refs
    q = q_ref[0]
    kc = kc_ref[0]
    vc = vc_ref[0]
    lp = lam_ref[...]
    lam = (jnp.exp(jnp.sum(lp[0:1] * lp[1:2], axis=-1, keepdims=True))
           - jnp.exp(jnp.sum(lp[2:3] * lp[3:4], axis=-1, keepdims=True)) + lam_init)
    lane = lax.broadcasted_iota(jnp.int32, (1, DIFF_W), 1)
    out = jnp.zeros((q.shape[0], DIFF_W), F32)
    def scores(h):
        res = []
        for c in range(2):
            lo = DIFF_VDIM * h + DIFF_DIM * c
            qm = jnp.where((lane >= lo) & (lane < lo + DIFF_DIM), q, jnp.zeros_like(q))
            res.append((_nt(qm, kc), _nt(qm, kx_ref[0]) if has_x else None))
        return res

    nxt = scores(0)
    for h in range(DIFF_HEADS):
        cur = nxt
        if h + 1 < DIFF_HEADS:
            nxt = scores(h + 1)
        parts = []
        for c in range(2):
            sc, sx = cur[c]
            m = jnp.max(sc, axis=-1, keepdims=True)
            if has_x:
                m = jnp.maximum(m, jnp.max(sx, axis=-1, keepdims=True))
            ec = jnp.exp2(sc - m)
            den = jnp.sum(ec, axis=-1, keepdims=True)
            ex = None
            if has_x:
                ex = jnp.exp2(sx - m)
                den = den + jnp.sum(ex, axis=-1, keepdims=True)
            parts.append((ec, ex, den))
        (ec0, ex0, den0), (ec1, ex1, den1) = parts
        rho = lam * den0 / den1
        o = _dot((ec0 - ec1 * rho).astype(BF16), vc)
        if has_x:
            o = o + _dot((ex0 - ex1 * rho).astype(BF16), vx_ref[0])
        hm = (lane >= DIFF_VDIM * h) & (lane < DIFF_VDIM * (h + 1))
        oh = jnp.where(hm, o / den0, 0.0)
        ms = jnp.sum(oh * oh, axis=-1, keepdims=True) * (1.0 / DIFF_VDIM)
        out = out + oh * lax.rsqrt(ms + NORM_EPS)
    o_ref[0] = (out * g_ref[...] * (1.0 - lam_init)).astype(o_ref.dtype)


def _diff(q, kc, vc, kx, vx, lam_p, g4, lam_init, tq):
    b, l, _ = q.shape
    lc = kc.shape[1]
    has_x = kx is not None
    cspec = pl.BlockSpec((1, lc, DIFF_W), lambda i, t: (i, 0, 0))
    in_specs = [pl.BlockSpec((1, tq, DIFF_W), lambda i, t: (i, t, 0)), cspec, cspec]
    args = [q, kc, vc]
    if has_x:
        xspec = pl.BlockSpec((1, kx.shape[1], DIFF_W), lambda i, t: (i, 0, 0))
        in_specs += [xspec, xspec]
        args += [kx, vx]
    in_specs += [pl.BlockSpec(lam_p.shape, lambda i, t: (0, 0)),
                 pl.BlockSpec((1, DIFF_W), lambda i, t: (0, 0))]
    args += [lam_p, g4]
    return pl.pallas_call(
        functools.partial(_diff_kernel, lam_init=lam_init, has_x=has_x),
        grid=(b, l // tq), in_specs=in_specs,
        out_specs=pl.BlockSpec((1, tq, DIFF_W), lambda i, t: (i, t, 0)),
        out_shape=jax.ShapeDtypeStruct((b, l, DIFF_W), BF16),
        compiler_params=_cparams(("parallel", "arbitrary")),
        name="diff_latent" if has_x else "diff_ctx",
    )(*args)


def _ret_kernel(dl_ref, cq_ref, ck_ref, cv_ref, cg_ref, xq_ref, xk_ref, xv_ref, xg_ref,
                yx_ref, yc_ref, s_ref, *, n_ctx, n_lat):
    h = pl.program_id(1)
    c = RET_CHUNK
    lane = lax.broadcasted_iota(jnp.int32, (1, RET_QK_W), 1)
    hm = (lane >= RET_QK * h) & (lane < RET_QK * (h + 1))
    row = lax.broadcasted_iota(jnp.int32, (c, c), 0).astype(F32)
    col = lax.broadcasted_iota(jnp.int32, (c, c), 1).astype(F32)
    roww = lax.broadcasted_iota(jnp.int32, (c, RET_QK_W), 0).astype(F32)

    def log_sigmoid(d):
        x = jnp.full((1, 1), dl_ref[d, h], F32)
        return jnp.minimum(x, 0.0) - jnp.log1p(jnp.exp(-jnp.abs(x)))

    lgf, lgb = log_sigmoid(0), log_sigmoid(1)
    diff = row - col
    fwd = (jnp.where(diff >= 0, jnp.exp(lgf * jnp.maximum(diff, 0.0)), 0.0),
           jnp.exp(lgf * (roww + 1.0)),
           jnp.exp(lgf * (c - 1.0 - roww)),
           jnp.exp(lgf * c))
    bwd = (jnp.where(diff <= 0, jnp.exp(lgb * jnp.maximum(-diff, 0.0)), 0.0),
           jnp.exp(lgb * (c - roww)),
           jnp.exp(lgb * roww),
           jnp.exp(lgb * c))

    def step(q_ref, k_ref, v_ref, g_ref, out_ref, ci, mats, sdir, second):
        dm, qd, kd, cd = mats
        sl = pl.ds(pl.multiple_of(ci * c, c), c)
        q = q_ref[0, sl, :]
        q = jnp.where(hm, q, jnp.zeros_like(q))
        k = k_ref[0, sl, :]
        v = v_ref[0, sl, :]
        att = _nt(q, k) * dm
        s = s_ref[sdir]
        o = _dot(att.astype(BF16), v) + _dot((q.astype(F32) * qd).astype(BF16), s.astype(BF16))
        s_ref[sdir] = cd * s + _tn((k.astype(F32) * kd).astype(BF16), v)
        if second:
            tot = out_ref[0, sl, :] + o
            out_ref[0, sl, :] = _rms(tot) * _silu(g_ref[0, sl, :])
        else:
            out_ref[0, sl, :] = o

    def both(refs, n, i, second):
        step(*refs, i, fwd, 0, second)
        step(*refs, n - 1 - i, bwd, 1, second)

    s_ref[...] = jnp.zeros_like(s_ref)
    crefs = (cq_ref, ck_ref, cv_ref, cg_ref, yc_ref)
    xrefs = (xq_ref, xk_ref, xv_ref, xg_ref, yx_ref)
    for i in range(n_ctx):
        both(crefs, n_ctx, i, i >= n_ctx // 2)
    for second in (False, True):
        def body(i, carry, second=second):
            both(xrefs, n_lat, i, second)
            return carry

        lax.fori_loop(n_lat // 2 if second else 0, n_lat if second else n_lat // 2, body, 0,
                      unroll=RET_UNROLL if (n_lat // 2) % RET_UNROLL == 0 else 1)


def _retention(decay_logit, cq, ck, cv, cg, xq, xk, xv, xg):
    b, l, _ = xq.shape
    lc = cq.shape[1]

    def qk_spec(n):
        return pl.BlockSpec((1, n, RET_QK_W), lambda i, h: (i, 0, 0))

    def v_spec(n):
        return pl.BlockSpec((1, n, RET_V), lambda i, h: (i, 0, h))

    return pl.pallas_call(
        functools.partial(_ret_kernel, n_ctx=lc // RET_CHUNK, n_lat=l // RET_CHUNK),
        grid=(b, RET_HEADS),
        in_specs=[pl.BlockSpec(memory_space=pltpu.SMEM),
                  qk_spec(lc), qk_spec(lc), v_spec(lc), v_spec(lc),
                  qk_spec(l), qk_spec(l), v_spec(l), v_spec(l)],
        out_specs=[v_spec(l), v_spec(lc)],
        out_shape=[jax.ShapeDtypeStruct((b, l, RET_W), F32), jax.ShapeDtypeStruct((b, lc, RET_W), F32)],
        scratch_shapes=[pltpu.VMEM((2, RET_QK_W, RET_V), F32)],
        compiler_params=_cparams(("parallel", "arbitrary")),
        name="retention",
    )(decay_logit.astype(F32), cq, ck, cv, cg, xq, xk, xv, xg)


def _outproj_kernel(x_ref, yna_ref, ydf_ref, yrt_ref, mod_ref, wo_ref, g2_ref, wq_ref, sk_ref,
                    xo_ref, h2_ref, st_ref, *, d):
    y = (_dot(yna_ref[0], wo_ref[0:NA_W, :])
         + _dot(ydf_ref[0], wo_ref[NA_W:NA_W + DIFF_W, :])
         + _dot(yrt_ref[0].astype(BF16), wo_ref[NA_W + DIFF_W:MIX_W, :]))
    mod = mod_ref[0]
    x = x_ref[0] + mod[:, 2 * d:3 * d] * y
    xo_ref[0] = x
    h2 = (_rms(x) * g2_ref[...] * (1.0 + mod[:, 4 * d:5 * d]) + mod[:, 3 * d:4 * d]).astype(BF16)
    h2_ref[0] = h2
    qp = _dot(h2, wq_ref[...]).astype(BF16)
    for h in range(PEER_HEADS):
        s2 = _nt(sk_ref[h], qp[:, h * PEER_KDIM:(h + 1) * PEER_KDIM])
        st_ref[0, 2 * h] = s2[0:PEER_NKEYS]
        st_ref[0, 2 * h + 1] = s2[PEER_NKEYS:2 * PEER_NKEYS]


def _outproj(x, yna, ydf, yrt, mod3, mod_row, wo, g2, wq, sk, tm):
    b, l, d = x.shape
    nhp = 2 * PEER_HEADS

    def tok(w):
        return pl.BlockSpec((1, tm, w), lambda i, t: (i, t, 0))

    return pl.pallas_call(
        functools.partial(_outproj_kernel, d=d),
        grid=(b, l // tm),
        in_specs=[tok(d), tok(NA_W), tok(DIFF_W), tok(RET_W),
                  pl.BlockSpec((1, 1, 6 * d), lambda i, t: (mod_row(i), 0, 0)),
                  pl.BlockSpec(wo.shape, lambda i, t: (0, 0)),
                  pl.BlockSpec((1, d), lambda i, t: (0, 0)),
                  pl.BlockSpec(wq.shape, lambda i, t: (0, 0)),
                  pl.BlockSpec(sk.shape, lambda i, t: (0, 0, 0))],
        out_specs=[tok(d), tok(d),
                   pl.BlockSpec((1, nhp, PEER_NKEYS, tm), lambda i, t: (i, 0, 0, t))],
        out_shape=[jax.ShapeDtypeStruct((b, l, d), F32), jax.ShapeDtypeStruct((b, l, d), BF16),
                   jax.ShapeDtypeStruct((b, nhp, PEER_NKEYS, l), F32)],
        compiler_params=_cparams(("parallel", "parallel")),
        name="outproj",
    )(x, yna, ydf, yrt, mod3, wo, g2, wq, sk)


def _sort_network(n):
    pairs = []
    p = 1
    while p < n:
        k = p
        while k >= 1:
            for j in range(k % p, n - k, 2 * k):
                for i in range(min(k, n - j - k)):
                    if (i + j) // (2 * p) == (i + j + k) // (2 * p):
                        pairs.append((i + j, i + j + k))
            k //= 2
        p *= 2
    return pairs


def _topk_kernel(st_ref, th_ref, e1_ref, e2_ref, cand_ref, t2_ref):
    neg = -jnp.inf

    def blocks(x):
        return [x[r * 8:(r + 1) * 8, :] for r in range(x.shape[0] // 8)]

    def sort_columns(vs):
        vs = list(vs)
        for i, j in _sort_network(len(vs)):
            vs[i], vs[j] = jnp.maximum(vs[i], vs[j]), jnp.minimum(vs[i], vs[j])
        return vs

    def top_rows(x):
        vs = sort_columns(blocks(x))
        rows = []
        for k in range(PEER_TOPK):
            m = jnp.max(vs[0], axis=0, keepdims=True)
            rows.append(m)
            depth = min(len(vs), PEER_TOPK - 1 - k)
            if depth > 0:
                hit = vs[0] == m
                for r in range(depth):
                    below = vs[r + 1] if r + 1 < len(vs) else neg
                    vs[r] = jnp.where(hit, below, vs[r])
        return rows

    cand_ref[len(PEER_PAIRS):, :] = jnp.full((PEER_NCAND - len(PEER_PAIRS), cand_ref.shape[1]), neg, F32)
    for h in range(PEER_HEADS):
        s1 = st_ref[0, 2 * h]
        s2 = st_ref[0, 2 * h + 1]
        a1 = s1 - jnp.max(s1, axis=0, keepdims=True)
        a2 = s2 - jnp.max(s2, axis=0, keepdims=True)
        t1 = top_rows(a1)
        t2 = top_rows(a2)
        for b in range(PEER_TOPK):
            t2_ref[b:b + 1, :] = t2[b]
        t2s = t2_ref[...]
        off = 0
        for a in range(PEER_TOPK):
            n = PEER_TOPK // (a + 1)
            cand_ref[off:off + n, :] = (t1[a] + t2s)[0:n]
            off += n
        cand = cand_ref[...]
        tau = top_rows(cand)[-1]
        z = jnp.sum(jnp.where(cand >= tau, jnp.exp(cand), 0.0), axis=0, keepdims=True)
        th_ref[0, h] = jnp.exp(tau - a1)
        e1_ref[0, h] = jnp.exp(a1) * (0.5 / z)
        e2_ref[0, h] = jnp.exp(a2)


def _topk(st, tl):
    b, nhp, nk, l = st.shape
    hspec = pl.BlockSpec((1, PEER_HEADS, nk, tl), lambda i, t: (i, 0, 0, t))
    hshape = jax.ShapeDtypeStruct((b, PEER_HEADS, nk, l), F32)
    return pl.pallas_call(
        _topk_kernel,
        grid=(b, l // tl),
        in_specs=[pl.BlockSpec((1, nhp, nk, tl), lambda i, t: (i, 0, 0, t))],
        out_specs=[hspec, hspec, hspec],
        out_shape=[hshape, hshape, hshape],
        scratch_shapes=[pltpu.VMEM((PEER_NCAND, tl), F32), pltpu.VMEM((PEER_TOPK, tl), F32)],
        compiler_params=_cparams(("parallel", "parallel")),
        name="peer_topk",
    )(st)


def _peer_kernel(*refs, final, d, ich):
    if final:
        h2_ref, u_ref, vt_ref, th_ref, e1_ref, e2_ref, x_ref, mod_ref, fg_ref, o_ref = refs[:10]
    else:
        h2_ref, u_ref, vt_ref, th_ref, e1_ref, e2_ref, x_ref, mod_ref, o_ref = refs[:9]
    acc_ref, at_ref, gw_ref, h2t_ref = refs[-4:]
    c = pl.program_id(2)
    tm = h2_ref.shape[1]
    nt = PEER_NKEYS // 8

    @pl.when(c == 0)
    def _():
        acc_ref[...] = jnp.zeros_like(acc_ref)
        h2t_ref[...] = h2_ref[0].T

    at_ref[...] = _dot(u_ref[...], h2t_ref[...])
    for ii in range(ich):
        for lg in range(tm // 128):
            ls = slice(lg * 128, (lg + 1) * 128)
            w = [None] * nt
            for h in range(PEER_HEADS):
                th = jnp.broadcast_to(th_ref[0, h, ii:ii + 1, ls], (8, 128))
                e1 = jnp.broadcast_to(e1_ref[0, h, ii:ii + 1, ls], (8, 128))
                for jt in range(nt):
                    e2 = e2_ref[0, h, jt * 8:(jt + 1) * 8, ls]
                    term = jnp.where(e2 >= th, e2, 0.0) * e1
                    w[jt] = term if w[jt] is None else w[jt] + term
            for jt in range(0, nt, 2):
                r0 = ii * PEER_NKEYS + jt * 8
                a = at_ref[r0:r0 + 16, ls]
                g = a * (1.0 + lax.erf(a * math.sqrt(0.5))) * jnp.concatenate(w[jt:jt + 2], axis=0)
                gw_ref[r0:r0 + 16, ls] = g.astype(BF16)
    acc_ref[...] += _dot(vt_ref[...], gw_ref[...])

    @pl.when(c == pl.num_programs(2) - 1)
    def _():
        mod = mod_ref[0]
        xo = x_ref[0] + mod[:, 5 * d:6 * d] * acc_ref[...].T
        if final:
            xo = _rms(xo) * fg_ref[...]
        o_ref[0] = xo


def _peer(h2, u, vt, th, e1, e2, x, mod3, mod_row, final_g, tm, ich=16):
    b, l, d = x.shape
    ne = u.shape[0]
    ec = ich * PEER_NKEYS
    final = final_g is not None
    ispec = pl.BlockSpec((1, PEER_HEADS, ich, tm), lambda i, t, c: (i, 0, c, t))
    in_specs = [pl.BlockSpec((1, tm, d), lambda i, t, c: (i, t, 0)),
                pl.BlockSpec((ec, d), lambda i, t, c: (c, 0)),
                pl.BlockSpec((d, ec), lambda i, t, c: (0, c)),
                ispec, ispec,
                pl.BlockSpec((1, PEER_HEADS, PEER_NKEYS, tm), lambda i, t, c: (i, 0, 0, t)),
                pl.BlockSpec((1, tm, d), lambda i, t, c: (i, t, 0)),
                pl.BlockSpec((1, 1, 6 * d), lambda i, t, c: (mod_row(i), 0, 0))]
    args = [h2, u, vt, th, e1, e2, x, mod3]
    if final:
        in_specs.append(pl.BlockSpec((1, d), lambda i, t, c: (0, 0)))
        args.append(final_g)
    return pl.pallas_call(
        functools.partial(_peer_kernel, final=final, d=d, ich=ich),
        grid=(b, l // tm, ne // ec),
        in_specs=in_specs,
        out_specs=pl.BlockSpec((1, tm, d), lambda i, t, c: (i, t, 0)),
        out_shape=jax.ShapeDtypeStruct((b, l, d), F32),
        scratch_shapes=[pltpu.VMEM((d, tm), F32), pltpu.VMEM((ec, tm), F32),
                        pltpu.VMEM((ec, tm), BF16), pltpu.VMEM((d, tm), BF16)],
        compiler_params=_cparams(("parallel", "parallel", "arbitrary")),
        name="peer_final" if final else "peer",
    )(*args)


def _rope_tables(l, dim, width):
    t = jnp.arange(l)
    rows = (t // GRID_W).astype(F32)
    cols = (t % GRID_W).astype(F32)
    half = dim // 2
    inv = jnp.power(ROPE_BASE, -jnp.arange(0, half, 2, dtype=F32) / half)
    ang = jnp.concatenate([rows[:, None] * inv, cols[:, None] * inv], axis=-1)
    pair = (np.arange(width) % dim) // 2
    sign = np.where(np.arange(width) % 2 == 0, -1.0, 1.0).astype(np.float32)
    return jnp.cos(ang)[:, pair], jnp.sin(ang)[:, pair] * sign


def _subkey_blocks(sk):
    h, _, n, dh = sk.shape
    z = jnp.zeros((h, n, dh), sk.dtype)
    return jnp.concatenate([jnp.concatenate([sk[:, 0], z], axis=2),
                            jnp.concatenate([z, sk[:, 1]], axis=2)], axis=1)


def _block_mixer_and_peer(x, ctx, mod3, layer, need_ctx, p, tables, final_g):
    b, l, d = x.shape
    lc = ctx.shape[1]
    row_x = lambda i: i
    row_c = lambda i: b
    lam_init = 0.8 - 0.6 * math.exp(-0.3 * layer)

    lat = _inproj(x, mod3, row_x, p["g1"], p["w_in"], tables, min(l, 512))
    cx = _inproj(ctx, mod3, row_c, p["g1"], p["w_in"], None, lc)
    naq, nak, nav, dfq, dfk, dfv, rtq, rtk, rtv, rtg = lat
    cnaq, cnak, cnav, cdfq, cdfk, cdfv, crtq, crtk, crtv, crtg = cx

    y_na = _na_latent(naq, nak, nav, cnak, cnav, p["rpb"], 4)
    y_df = _diff(dfq, cdfk, cdfv, dfk, dfv, p["lam"], p["subln"], lam_init, 256)
    y_rt, yc_rt = _retention(p["decay"], crtq, crtk, crtv, crtg, rtq, rtk, rtv, rtg)

    def channel_mix(xx, yna, ydf, yrt, row, tm, fg):
        xo, h2, st = _outproj(xx, yna, ydf, yrt, mod3, row, p["w_out"], p["g2"], p["wq"], p["sk"], tm)
        th, e1, e2 = _topk(st, min(tm, 256))
        return _peer(h2, p["u"], p["vt"], th, e1, e2, xo, mod3, row, fg, tm)

    x = channel_mix(x, y_na, y_df, y_rt, row_x, min(l, 512), final_g)
    if need_ctx:
        yc_na = _na_ctx(cnaq, cnak, cnav)
        yc_df = _diff(cdfq, cdfk, cdfv, None, None, p["lam"], p["subln"], lam_init, lc)
        flat = lambda t: t.reshape(1, b * lc, t.shape[-1])
        ctx = channel_mix(flat(ctx), flat(yc_na), flat(yc_df), flat(yc_rt), lambda i: b,
                          min(b * lc, 512), None).reshape(b, lc, d)
    return x, ctx


def kernel(x, c, ctx, c_ctx, w_ada, b_ada, norm1_g, w_in, na_rpb, diff_lambda, diff_subln_g,
           ret_decay_logit, w_out, norm2_g, peer_wq, peer_subkeys, peer_u, peer_v, final_g):
    b, l, d = x.shape
    depth = w_ada.shape[0]
    x = x.astype(F32)
    ctx = ctx.astype(F32)
    npad = -(-(b + 1) // 8) * 8
    cpad = jnp.zeros((npad, d), F32).at[:b].set(c.astype(F32)).at[b].set(c_ctx.astype(F32))
    mod = _ada(cpad, w_ada.astype(F32), b_ada.astype(F32))
    tables = _rope_tables(l, DIFF_DIM, DIFF_QK_W) + _rope_tables(l, RET_QK, RET_QK_W)
    for layer in range(depth):
        p = {
            "g1": norm1_g[layer].astype(F32).reshape(1, d),
            "g2": norm2_g[layer].astype(F32).reshape(1, d),
            "w_in": w_in[layer].astype(BF16),
            "rpb": na_rpb[layer],
            "lam": diff_lambda[layer].astype(F32),
            "subln": jnp.tile(diff_subln_g[layer].astype(F32), DIFF_HEADS).reshape(1, DIFF_W),
            "decay": ret_decay_logit[layer],
            "w_out": w_out[layer].astype(BF16),
            "wq": peer_wq[layer].astype(BF16),
            "sk": _subkey_blocks(peer_subkeys[layer].astype(BF16)),
            "u": peer_u[layer].astype(BF16),
            "vt": peer_v[layer].astype(BF16).T,
        }
        last = layer == depth - 1
        x, ctx = _block_mixer_and_peer(x, ctx, mod[layer].reshape(npad, 1, 6 * d), layer, not last, p,
                                       tables, final_g.astype(F32).reshape(1, d) if last else None)
    return x
```

```python
import functools
import math

import numpy as np
import jax
import jax.numpy as jnp
from jax import lax
from jax.experimental import pallas as pl
from jax.experimental.pallas import tpu as pltpu

F32 = jnp.float32
BF16 = jnp.bfloat16

GRID_W = 64
NORM_EPS = 1e-6
ROPE_BASE = 10000.0
NEG_INF = -1e30
LOG2E = math.log2(math.e)

NA_HEADS = 4
NA_DIM = 64
NA_WIN_ROWS = 8
NA_WIN_COLS = 16
DIFF_HEADS = 4
DIFF_DIM = 32
DIFF_VDIM = 64
RET_HEADS = 4
RET_QK = 64
RET_V = 128
RET_CHUNK = 128
RET_UNROLL = 8

NA_W = NA_HEADS * NA_DIM
DIFF_QK_W = DIFF_HEADS * 2 * DIFF_DIM
DIFF_W = DIFF_HEADS * DIFF_VDIM
RET_QK_W = RET_HEADS * RET_QK
RET_W = RET_HEADS * RET_V
MIX_W = NA_W + DIFF_W + RET_W
IN_COLS = 3 * NA_W + 2 * DIFF_QK_W + DIFF_W + 2 * RET_QK_W + 2 * RET_W

PEER_HEADS = 8
PEER_NKEYS = 128
PEER_KDIM = 256
PEER_TOPK = 16
PEER_HALF = PEER_KDIM // 2

C_NAQ, C_NAK, C_NAV = 0, 256, 512
C_DFQ, C_DFK, C_DFV = 768, 1024, 1280
C_RTQ, C_RTK, C_RTV, C_RTG = 1536, 1792, 2048, 2560

PEER_PAIRS = tuple((a, b) for a in range(PEER_TOPK) for b in range(PEER_TOPK // (a + 1)))
PEER_NCAND = 64

VMEM_LIMIT = 56 * 1024 * 1024


def _cparams(sem):
    return pltpu.CompilerParams(dimension_semantics=sem, vmem_limit_bytes=VMEM_LIMIT)


def _nt(a, b):
    return lax.dot_general(a, b, (((1,), (1,)), ((), ())), preferred_element_type=F32)


def _tn(a, b):
    return lax.dot_general(a, b, (((0,), (0,)), ((), ())), preferred_element_type=F32)


def _dot(a, b):
    return jnp.dot(a, b, preferred_element_type=F32)


def _rms(x):
    return x * lax.rsqrt(jnp.mean(x * x, axis=-1, keepdims=True) + NORM_EPS)


def _silu(x):
    return x * jax.nn.sigmoid(x)


def _ada_kernel(c_ref, w_ref, b_ref, o_ref):
    s = _silu(c_ref[...])
    o_ref[0] = jnp.dot(s, w_ref[0], preferred_element_type=F32,
                       precision=lax.Precision.HIGHEST) + b_ref[0]


def _ada(cpad, w_ada, b_ada):
    depth, d, n = w_ada.shape
    tn = 1536
    return pl.pallas_call(
        _ada_kernel,
        grid=(depth, n // tn),
        in_specs=[pl.BlockSpec((cpad.shape[0], d), lambda l, j: (0, 0)),
                  pl.BlockSpec((1, d, tn), lambda l, j: (l, 0, j)),
                  pl.BlockSpec((1, 1, tn), lambda l, j: (l, 0, j))],
        out_specs=pl.BlockSpec((1, cpad.shape[0], tn), lambda l, j: (l, 0, j)),
        out_shape=jax.ShapeDtypeStruct((depth, cpad.shape[0], n), F32),
        compiler_params=_cparams(("parallel", "parallel")),
        name="ada",
    )(cpad, w_ada, b_ada.reshape(depth, 1, n))


def _inproj_kernel(*refs, rope, d):
    if rope:
        (x_ref, mod_ref, g_ref, w_ref, cd_ref, sd_ref, cr_ref, sr_ref,
         naq, nak, nav, dfq, dfk, dfv, rtq, rtk, rtv, rtg) = refs
    else:
        (x_ref, mod_ref, g_ref, w_ref,
         naq, nak, nav, dfq, dfk, dfv, rtq, rtk, rtv, rtg) = refs
    mod = mod_ref[0]
    h = _rms(x_ref[0]) * g_ref[...] * (1.0 + mod[:, d:2 * d]) + mod[:, 0:d]
    hb = h.astype(BF16)

    def proj(c0, n):
        return _dot(hb, w_ref[:, c0:c0 + n])

    even_lane = (lax.broadcasted_iota(jnp.int32, (1, 256), 1) & 1) == 0

    def roped(c0, cos_ref, sin_ref):
        p = proj(c0, 256)
        if not rope:
            return p
        partner = jnp.where(even_lane, pltpu.roll(p, 255, 1), pltpu.roll(p, 1, 1))
        return p * cos_ref[...] + partner * sin_ref[...]

    naq[0] = (proj(C_NAQ, 256) * (NA_DIM ** -0.5 * LOG2E)).astype(BF16)
    nak[0] = proj(C_NAK, 256).astype(BF16)
    nav[0] = proj(C_NAV, 256).astype(BF16)
    dfq[0] = (roped(C_DFQ, cd_ref if rope else None, sd_ref if rope else None)
              * (DIFF_DIM ** -0.5 * LOG2E)).astype(BF16)
    dfk[0] = roped(C_DFK, cd_ref if rope else None, sd_ref if rope else None).astype(BF16)
    dfv[0] = proj(C_DFV, 256).astype(BF16)
    rtq[0] = roped(C_RTQ, cr_ref if rope else None, sr_ref if rope else None).astype(BF16)
    rtk[0] = (roped(C_RTK, cr_ref if rope else None, sr_ref if rope else None)
              * RET_QK ** -0.5).astype(BF16)
    rtv[0] = proj(C_RTV, 512).astype(BF16)
    rtg[0] = proj(C_RTG, 512)


def _inproj(x, mod3, mod_row, g1, w_in, tables, tm):
    b, l, d = x.shape
    rope = tables is not None
    in_specs = [pl.BlockSpec((1, tm, d), lambda i, t: (i, t, 0)),
                pl.BlockSpec((1, 1, 6 * d), lambda i, t: (mod_row(i), 0, 0)),
                pl.BlockSpec((1, d), lambda i, t: (0, 0)),
                pl.BlockSpec((d, IN_COLS), lambda i, t: (0, 0))]
    args = [x, mod3, g1, w_in]
    if rope:
        in_specs += [pl.BlockSpec((tm, 256), lambda i, t: (t, 0))] * 4
        args += list(tables)
    widths = (256, 256, 256, 256, 256, 256, 256, 256, 512, 512)
    dtypes = (BF16,) * 9 + (F32,)
    out_specs = [pl.BlockSpec((1, tm, w), lambda i, t: (i, t, 0)) for w in widths]
    out_shape = [jax.ShapeDtypeStruct((b, l, w), dt) for w, dt in zip(widths, dtypes)]
    return pl.pallas_call(
        functools.partial(_inproj_kernel, rope=rope, d=d),
        grid=(b, l // tm),
        in_specs=in_specs, out_specs=out_specs, out_shape=out_shape,
        compiler_params=_cparams(("parallel", "parallel")),
        name="inproj_rope" if rope else "inproj_ctx",
    )(*args)


def _na_plan(rows, rb):
    wr = min(NA_WIN_ROWS, rows)
    kr = min(rb + wr - 1, rows)
    nblk = rows // rb
    win0 = np.clip(np.arange(rows) - wr // 2, 0, rows - wr)
    ks = np.clip(np.arange(nblk) * rb - wr // 2, 0, rows - kr)
    pats, pat_of = [], np.zeros(nblk, np.int32)
    for g in range(nblk):
        r = g * rb + np.arange(rb)
        krow = ks[g] + np.arange(kr)
        dr = krow[None, :] - r[:, None] + (NA_WIN_ROWS - 1)
        ok = (krow[None, :] >= win0[r][:, None]) & (krow[None, :] < win0[r][:, None] + wr)
        assert ok.sum(axis=1).min() == wr
        key = (np.where(ok, dr, 0).tobytes(), ok.tobytes())
        for p, (k2, _, _) in enumerate(pats):
            if k2 == key:
                pat_of[g] = p
                break
        else:
            pat_of[g] = len(pats)
            pats.append((key, np.where(ok, dr, 0), ok))
    dr_idx = np.stack([p[1] for p in pats])
    row_ok = np.stack([p[2] for p in pats])
    qcol = np.arange(GRID_W)
    kcol = np.arange(GRID_W)
    cs = np.clip(qcol - NA_WIN_COLS // 2, 0, GRID_W - NA_WIN_COLS)
    col_ok = (kcol[None, :] >= cs[:, None]) & (kcol[None, :] < cs[:, None] + NA_WIN_COLS)
    dc_idx = np.clip(kcol[None, :] - qcol[:, None], 1 - NA_WIN_COLS, NA_WIN_COLS - 1) + NA_WIN_COLS - 1
    return kr, ks.astype(np.int32), pat_of, dr_idx, row_ok, dc_idx, col_ok


def _na_bias(rpb, plan, rb):
    kr, _, _, dr_idx, row_ok, dc_idx, col_ok = plan
    p = dr_idx.shape[0]
    nr, nc = 2 * NA_WIN_ROWS - 1, 2 * NA_WIN_COLS - 1
    row_hot = ((np.arange(nr) == dr_idx[..., None]) & row_ok[..., None]).astype(np.float32)
    col_hot = ((np.arange(nc) == dc_idx[..., None]) & col_ok[..., None]).astype(np.float32)
    bias = jnp.einsum("hrc,pqwr,xyc->hpqxwy", rpb.astype(F32), row_hot, col_hot,
                      precision=lax.Precision.HIGHEST)
    ok = row_ok[:, :, None, :, None] & col_ok[None, None, :, None, :]
    bias = jnp.where(ok[None], bias * LOG2E, NEG_INF)
    return bias.reshape(NA_HEADS, p, rb * GRID_W, kr * GRID_W)


def _na_kernel(ks_ref, pat_ref, q_ref, kx_ref, vx_ref, kc_ref, vc_ref, bias_ref, o_ref, *, nkeys):
    g = pl.program_id(1)
    q = q_ref[0]
    k0 = pl.multiple_of(ks_ref[g] * GRID_W, GRID_W)
    kw = kx_ref[0, pl.ds(k0, nkeys), :]
    vw = vx_ref[0, pl.ds(k0, nkeys), :]
    kc = kc_ref[0]
    vc = vc_ref[0]
    pat = pat_ref[g]
    lane = lax.broadcasted_iota(jnp.int32, (1, NA_W), 1)
    out = jnp.zeros((q.shape[0], NA_W), F32)
    for h in range(NA_HEADS):
        hm = (lane >= NA_DIM * h) & (lane < NA_DIM * (h + 1))
        qh = jnp.where(hm, q, jnp.zeros_like(q))
        sw = _nt(qh, kw) + bias_ref[h, pat]
        sc = _nt(qh, kc)
        m = jnp.maximum(jnp.max(sw, axis=-1, keepdims=True), jnp.max(sc, axis=-1, keepdims=True))
        pw = jnp.exp2(sw - m)
        pc = jnp.exp2(sc - m)
        den = jnp.sum(pw, axis=-1, keepdims=True) + jnp.sum(pc, axis=-1, keepdims=True)
        o = _dot(pw.astype(BF16), vw) + _dot(pc.astype(BF16), vc)
        out = out + jnp.where(hm, o / den, 0.0)
    o_ref[0] = out.astype(o_ref.dtype)


def _na_latent(q, kx, vx, kc, vc, rpb, rb):
    b, l, _ = q.shape
    lc = kc.shape[1]
    rows = l // GRID_W
    plan = _na_plan(rows, rb)
    kr, ks, pat_of = plan[0], plan[1], plan[2]
    bias = _na_bias(rpb, plan, rb)
    tq, nkeys = rb * GRID_W, kr * GRID_W
    grid_spec = pltpu.PrefetchScalarGridSpec(
        num_scalar_prefetch=2,
        grid=(b, rows // rb),
        in_specs=[pl.BlockSpec((1, tq, NA_W), lambda i, g, *_: (i, g, 0)),
                  pl.BlockSpec((1, l, NA_W), lambda i, g, *_: (i, 0, 0)),
                  pl.BlockSpec((1, l, NA_W), lambda i, g, *_: (i, 0, 0)),
                  pl.BlockSpec((1, lc, NA_W), lambda i, g, *_: (i, 0, 0)),
                  pl.BlockSpec((1, lc, NA_W), lambda i, g, *_: (i, 0, 0)),
                  pl.BlockSpec(bias.shape, lambda i, g, *_: (0, 0, 0, 0))],
        out_specs=pl.BlockSpec((1, tq, NA_W), lambda i, g, *_: (i, g, 0)),
    )
    return pl.pallas_call(
        functools.partial(_na_kernel, nkeys=nkeys),
        grid_spec=grid_spec,
        out_shape=jax.ShapeDtypeStruct((b, l, NA_W), BF16),
        compiler_params=_cparams(("parallel", "arbitrary")),
        name="na_latent",
    )(jnp.asarray(ks), jnp.asarray(pat_of), q, kx, vx, kc, vc, bias)


def _na_ctx_kernel(q_ref, kc_ref, vc_ref, o_ref):
    q = q_ref[0]
    kc = kc_ref[0]
    vc = vc_ref[0]
    lane = lax.broadcasted_iota(jnp.int32, (1, NA_W), 1)
    out = jnp.zeros((q.shape[0], NA_W), F32)
    for h in range(NA_HEADS):
        hm = (lane >= NA_DIM * h) & (lane < NA_DIM * (h + 1))
        qh = jnp.where(hm, q, jnp.zeros_like(q))
        sc = _nt(qh, kc)
        pc = jnp.exp2(sc - jnp.max(sc, axis=-1, keepdims=True))
        o = _dot(pc.astype(BF16), vc)
        out = out + jnp.where(hm, o / jnp.sum(pc, axis=-1, keepdims=True), 0.0)
    o_ref[0] = out.astype(o_ref.dtype)


def _na_ctx(q, kc, vc):
    b, lc, _ = q.shape
    spec = pl.BlockSpec((1, lc, NA_W), lambda i: (i, 0, 0))
    return pl.pallas_call(
        _na_ctx_kernel, grid=(b,), in_specs=[spec, spec, spec], out_specs=spec,
        out_shape=jax.ShapeDtypeStruct((b, lc, NA_W), BF16),
        compiler_params=_cparams(("parallel",)), name="na_ctx",
    )(q, kc, vc)


def _diff_kernel(*refs, lam_init, has_x):
    if has_x:
        q_ref, kc_ref, vc_ref, kx_ref, vx_ref, lam_ref, g_ref, o_ref = refs
    else:
        q_ref, kc_ref, vc_ref, lam_ref, g_ref, o_ref = refs
    q = q_ref[0]
    kc = kc_ref[0]
    vc = vc_ref[0]
    lp = lam_ref[...]
    lam = (jnp.exp(jnp.sum(lp[0:1] * lp[1:2], axis=-1, keepdims=True))
           - jnp.exp(jnp.sum(lp[2:3] * lp[3:4], axis=-1, keepdims=True)) + lam_init)
    lane = lax.broadcasted_iota(jnp.int32, (1, DIFF_W), 1)
    out = jnp.zeros((q.shape[0], DIFF_W), F32)
    def scores(h):
        res = []
        for c in range(2):
            lo = DIFF_VDIM * h + DIFF_DIM * c
            qm = jnp.where((lane >= lo) & (lane < lo + DIFF_DIM), q, jnp.zeros_like(q))
            res.append((_nt(qm, kc), _nt(qm, kx_ref[0]) if has_x else None))
        return res

    nxt = scores(0)
    for h in range(DIFF_HEADS):
        cur = nxt
        if h + 1 < DIFF_HEADS:
            nxt = scores(h + 1)
        parts = []
        for c in range(2):
            sc, sx = cur[c]
            m = jnp.max(sc, axis=-1, keepdims=True)
            if has_x:
                m = jnp.maximum(m, jnp.max(sx, axis=-1, keepdims=True))
            ec = jnp.exp2(sc - m)
            den = jnp.sum(ec, axis=-1, keepdims=True)
            ex = None
            if has_x:
                ex = jnp.exp2(sx - m)
                den = den + jnp.sum(ex, axis=-1, keepdims=True)
            parts.append((ec, ex, den))
        (ec0, ex0, den0), (ec1, ex1, den1) = parts
        rho = lam * den0 / den1
        o = _dot((ec0 - ec1 * rho).astype(BF16), vc)
        if has_x:
            o = o + _dot((ex0 - ex1 * rho).astype(BF16), vx_ref[0])
        hm = (lane >= DIFF_VDIM * h) & (lane < DIFF_VDIM * (h + 1))
        oh = jnp.where(hm, o / den0, 0.0)
        ms = jnp.sum(oh * oh, axis=-1, keepdims=True) * (1.0 / DIFF_VDIM)
        out = out + oh * lax.rsqrt(ms + NORM_EPS)
    o_ref[0] = (out * g_ref[...] * (1.0 - lam_init)).astype(o_ref.dtype)


def _diff(q, kc, vc, kx, vx, lam_p, g4, lam_init, tq):
    b, l, _ = q.shape
    lc = kc.shape[1]
    has_x = kx is not None
    cspec = pl.BlockSpec((1, lc, DIFF_W), lambda i, t: (i, 0, 0))
    in_specs = [pl.BlockSpec((1, tq, DIFF_W), lambda i, t: (i, t, 0)), cspec, cspec]
    args = [q, kc, vc]
    if has_x:
        xspec = pl.BlockSpec((1, kx.shape[1], DIFF_W), lambda i, t: (i, 0, 0))
        in_specs += [xspec, xspec]
        args += [kx, vx]
    in_specs += [pl.BlockSpec(lam_p.shape, lambda i, t: (0, 0)),
                 pl.BlockSpec((1, DIFF_W), lambda i, t: (0, 0))]
    args += [lam_p, g4]
    return pl.pallas_call(
        functools.partial(_diff_kernel, lam_init=lam_init, has_x=has_x),
        grid=(b, l // tq), in_specs=in_specs,
        out_specs=pl.BlockSpec((1, tq, DIFF_W), lambda i, t: (i, t, 0)),
        out_shape=jax.ShapeDtypeStruct((b, l, DIFF_W), BF16),
        compiler_params=_cparams(("parallel", "arbitrary")),
        name="diff_latent" if has_x else "diff_ctx",
    )(*args)


def _ret_kernel(dl_ref, cq_ref, ck_ref, cv_ref, cg_ref, xq_ref, xk_ref, xv_ref, xg_ref,
                yx_ref, yc_ref, s_ref, *, n_ctx, n_lat):
    h = pl.program_id(1)
    c = RET_CHUNK
    lane = lax.broadcasted_iota(jnp.int32, (1, RET_QK_W), 1)
    hm = (lane >= RET_QK * h) & (lane < RET_QK * (h + 1))
    row = lax.broadcasted_iota(jnp.int32, (c, c), 0).astype(F32)
    col = lax.broadcasted_iota(jnp.int32, (c, c), 1).astype(F32)
    roww = lax.broadcasted_iota(jnp.int32, (c, RET_QK_W), 0).astype(F32)

    def log_sigmoid(d):
        x = jnp.full((1, 1), dl_ref[d, h], F32)
        return jnp.minimum(x, 0.0) - jnp.log1p(jnp.exp(-jnp.abs(x)))

    lgf, lgb = log_sigmoid(0), log_sigmoid(1)
    diff = row - col
    fwd = (jnp.where(diff >= 0, jnp.exp(lgf * jnp.maximum(diff, 0.0)), 0.0),
           jnp.exp(lgf * (roww + 1.0)),
           jnp.exp(lgf * (c - 1.0 - roww)),
           jnp.exp(lgf * c))
    bwd = (jnp.where(diff <= 0, jnp.exp(lgb * jnp.maximum(-diff, 0.0)), 0.0),
           jnp.exp(lgb * (c - roww)),
           jnp.exp(lgb * roww),
           jnp.exp(lgb * c))

    def step(q_ref, k_ref, v_ref, g_ref, out_ref, ci, mats, sdir, second):
        dm, qd, kd, cd = mats
        sl = pl.ds(pl.multiple_of(ci * c, c), c)
        q = q_ref[0, sl, :]
        q = jnp.where(hm, q, jnp.zeros_like(q))
        k = k_ref[0, sl, :]
        v = v_ref[0, sl, :]
        att = _nt(q, k) * dm
        s = s_ref[sdir]
        o = _dot(att.astype(BF16), v) + _dot((q.astype(F32) * qd).astype(BF16), s.astype(BF16))
        s_ref[sdir] = cd * s + _tn((k.astype(F32) * kd).astype(BF16), v)
        if second:
            tot = out_ref[0, sl, :] + o
            out_ref[0, sl, :] = _rms(tot) * _silu(g_ref[0, sl, :])
        else:
            out_ref[0, sl, :] = o

    def both(refs, n, i, second):
        step(*refs, i, fwd, 0, second)
        step(*refs, n - 1 - i, bwd, 1, second)

    s_ref[...] = jnp.zeros_like(s_ref)
    crefs = (cq_ref, ck_ref, cv_ref, cg_ref, yc_ref)
    xrefs = (xq_ref, xk_ref, xv_ref, xg_ref, yx_ref)
    for i in range(n_ctx):
        both(crefs, n_ctx, i, i >= n_ctx // 2)
    for second in (False, True):
        def body(i, carry, second=second):
            both(xrefs, n_lat, i, second)
            return carry

        lax.fori_loop(n_lat // 2 if second else 0, n_lat if second else n_lat // 2, body, 0,
                      unroll=RET_UNROLL if (n_lat // 2) % RET_UNROLL == 0 else 1)


def _retention(decay_logit, cq, ck, cv, cg, xq, xk, xv, xg):
    b, l, _ = xq.shape
    lc = cq.shape[1]

    def qk_spec(n):
        return pl.BlockSpec((1, n, RET_QK_W), lambda i, h: (i, 0, 0))

    def v_spec(n):
        return pl.BlockSpec((1, n, RET_V), lambda i, h: (i, 0, h))

    return pl.pallas_call(
        functools.partial(_ret_kernel, n_ctx=lc // RET_CHUNK, n_lat=l // RET_CHUNK),
        grid=(b, RET_HEADS),
        in_specs=[pl.BlockSpec(memory_space=pltpu.SMEM),
                  qk_spec(lc), qk_spec(lc), v_spec(lc), v_spec(lc),
                  qk_spec(l), qk_spec(l), v_spec(l), v_spec(l)],
        out_specs=[v_spec(l), v_spec(lc)],
        out_shape=[jax.ShapeDtypeStruct((b, l, RET_W), F32), jax.ShapeDtypeStruct((b, lc, RET_W), F32)],
        scratch_shapes=[pltpu.VMEM((2, RET_QK_W, RET_V), F32)],
        compiler_params=_cparams(("parallel", "arbitrary")),
        name="retention",
    )(decay_logit.astype(F32), cq, ck, cv, cg, xq, xk, xv, xg)


def _outproj_kernel(x_ref, yna_ref, ydf_ref, yrt_ref, mod_ref, wo_ref, g2_ref, wq_ref, sk_ref,
                    xo_ref, h2_ref, st_ref, *, d):
    y = (_dot(yna_ref[0], wo_ref[0:NA_W, :])
         + _dot(ydf_ref[0], wo_ref[NA_W:NA_W + DIFF_W, :])
         + _dot(yrt_ref[0].astype(BF16), wo_ref[NA_W + DIFF_W:MIX_W, :]))
    mod = mod_ref[0]
    x = x_ref[0] + mod[:, 2 * d:3 * d] * y
    xo_ref[0] = x
    h2 = (_rms(x) * g2_ref[...] * (1.0 + mod[:, 4 * d:5 * d]) + mod[:, 3 * d:4 * d]).astype(BF16)
    h2_ref[0] = h2
    qp = _dot(h2, wq_ref[...]).astype(BF16)
    for hp in range(2 * PEER_HEADS):
        st_ref[0, hp] = _nt(sk_ref[hp], qp[:, hp * PEER_HALF:(hp + 1) * PEER_HALF])


def _outproj(x, yna, ydf, yrt, mod3, mod_row, wo, g2, wq, sk, tm):
    b, l, d = x.shape
    nhp = 2 * PEER_HEADS

    def tok(w):
        return pl.BlockSpec((1, tm, w), lambda i, t: (i, t, 0))

    return pl.pallas_call(
        functools.partial(_outproj_kernel, d=d),
        grid=(b, l // tm),
        in_specs=[tok(d), tok(NA_W), tok(DIFF_W), tok(RET_W),
                  pl.BlockSpec((1, 1, 6 * d), lambda i, t: (mod_row(i), 0, 0)),
                  pl.BlockSpec(wo.shape, lambda i, t: (0, 0)),
                  pl.BlockSpec((1, d), lambda i, t: (0, 0)),
                  pl.BlockSpec(wq.shape, lambda i, t: (0, 0)),
                  pl.BlockSpec(sk.shape, lambda i, t: (0, 0, 0))],
        out_specs=[tok(d), tok(d),
                   pl.BlockSpec((1, nhp, PEER_NKEYS, tm), lambda i, t: (i, 0, 0, t))],
        out_shape=[jax.ShapeDtypeStruct((b, l, d), F32), jax.ShapeDtypeStruct((b, l, d), BF16),
                   jax.ShapeDtypeStruct((b, nhp, PEER_NKEYS, l), F32)],
        compiler_params=_cparams(("parallel", "parallel")),
        name="outproj",
    )(x, yna, ydf, yrt, mod3, wo, g2, wq, sk)


def _sort_network(n):
    pairs = []
    p = 1
    while p < n:
        k = p
        while k >= 1:
            for j in range(k % p, n - k, 2 * k):
                for i in range(min(k, n - j - k)):
                    if (i + j) // (2 * p) == (i + j + k) // (2 * p):
                        pairs.append((i + j, i + j + k))
            k //= 2
        p *= 2
    return pairs


def _topk_kernel(st_ref, th_ref, e1_ref, e2_ref, cand_ref, t2_ref):
    neg = -jnp.inf

    def blocks(x):
        return [x[r * 8:(r + 1) * 8, :] for r in range(x.shape[0] // 8)]

    def sort_columns(vs):
        vs = list(vs)
        for i, j in _sort_network(len(vs)):
            vs[i], vs[j] = jnp.maximum(vs[i], vs[j]), jnp.minimum(vs[i], vs[j])
        return vs

    def top_rows(x):
        vs = sort_columns(blocks(x))
        rows = []
        for k in range(PEER_TOPK):
            m = jnp.max(vs[0], axis=0, keepdims=True)
            rows.append(m)
            depth = min(len(vs), PEER_TOPK - 1 - k)
            if depth > 0:
                hit = vs[0] == m
                for r in range(depth):
                    below = vs[r + 1] if r + 1 < len(vs) else neg
                    vs[r] = jnp.where(hit, below, vs[r])
        return rows

    cand_ref[len(PEER_PAIRS):, :] = jnp.full((PEER_NCAND - len(PEER_PAIRS), cand_ref.shape[1]), neg, F32)
    for h in range(PEER_HEADS):
        s1 = st_ref[0, 2 * h]
        s2 = st_ref[0, 2 * h + 1]
        a1 = s1 - jnp.max(s1, axis=0, keepdims=True)
        a2 = s2 - jnp.max(s2, axis=0, keepdims=True)
        t1 = top_rows(a1)
        t2 = top_rows(a2)
        for b in range(PEER_TOPK):
            t2_ref[b:b + 1, :] = t2[b]
        t2s = t2_ref[...]
        off = 0
        for a in range(PEER_TOPK):
            n = PEER_TOPK // (a + 1)
            cand_ref[off:off + n, :] = (t1[a] + t2s)[0:n]
            off += n
        cand = cand_ref[...]
        tau = top_rows(cand)[-1]
        z = jnp.sum(jnp.where(cand >= tau, jnp.exp(cand), 0.0), axis=0, keepdims=True)
        th_ref[0, h] = jnp.exp(tau - a1)
        e1_ref[0, h] = jnp.exp(a1) * (0.5 / z)
        e2_ref[0, h] = jnp.exp(a2)


def _topk(st, tl):
    b, nhp, nk, l = st.shape
    hspec = pl.BlockSpec((1, PEER_HEADS, nk, tl), lambda i, t: (i, 0, 0, t))
    hshape = jax.ShapeDtypeStruct((b, PEER_HEADS, nk, l), F32)
    return pl.pallas_call(
        _topk_kernel,
        grid=(b, l // tl),
        in_specs=[pl.BlockSpec((1, nhp, nk, tl), lambda i, t: (i, 0, 0, t))],
        out_specs=[hspec, hspec, hspec],
        out_shape=[hshape, hshape, hshape],
        scratch_shapes=[pltpu.VMEM((PEER_NCAND, tl), F32), pltpu.VMEM((PEER_TOPK, tl), F32)],
        compiler_params=_cparams(("parallel", "parallel")),
        name="peer_topk",
    )(st)


def _peer_kernel(*refs, final, d, ich):
    if final:
        h2_ref, u_ref, vt_ref, th_ref, e1_ref, e2_ref, x_ref, mod_ref, fg_ref, o_ref = refs[:10]
    else:
        h2_ref, u_ref, vt_ref, th_ref, e1_ref, e2_ref, x_ref, mod_ref, o_ref = refs[:9]
    acc_ref, at_ref, gw_ref, h2t_ref = refs[-4:]
    c = pl.program_id(2)
    tm = h2_ref.shape[1]
    nt = PEER_NKEYS // 8

    @pl.when(c == 0)
    def _():
        acc_ref[...] = jnp.zeros_like(acc_ref)
        h2t_ref[...] = h2_ref[0].T

    a = _dot(u_ref[...], h2t_ref[...])
    at_ref[...] = a * (1.0 + lax.erf(a * math.sqrt(0.5)))
    for ii in range(ich):
        for lg in range(tm // 128):
            ls = slice(lg * 128, (lg + 1) * 128)
            w = [None] * nt
            for h in range(PEER_HEADS):
                th = jnp.broadcast_to(th_ref[0, h, ii:ii + 1, ls], (8, 128))
                e1 = jnp.broadcast_to(e1_ref[0, h, ii:ii + 1, ls], (8, 128))
                for jt in range(nt):
                    e2 = e2_ref[0, h, jt * 8:(jt + 1) * 8, ls]
                    term = jnp.where(e2 >= th, e2, 0.0) * e1
                    w[jt] = term if w[jt] is None else w[jt] + term
            for jt in range(0, nt, 2):
                r0 = ii * PEER_NKEYS + jt * 8
                g = at_ref[r0:r0 + 16, ls] * jnp.concatenate(w[jt:jt + 2], axis=0)
                gw_ref[r0:r0 + 16, ls] = g.astype(BF16)
    acc_ref[...] += _dot(vt_ref[...], gw_ref[...])

    @pl.when(c == pl.num_programs(2) - 1)
    def _():
        mod = mod_ref[0]
        xo = x_ref[0] + mod[:, 5 * d:6 * d] * acc_ref[...].T
        if final:
            xo = _rms(xo) * fg_ref[...]
        o_ref[0] = xo


def _peer(h2, u, vt, th, e1, e2, x, mod3, mod_row, final_g, tm, ich=16):
    b, l, d = x.shape
    ne = u.shape[0]
    ec = ich * PEER_NKEYS
    final = final_g is not None
    ispec = pl.BlockSpec((1, PEER_HEADS, ich, tm), lambda i, t, c: (i, 0, c, t))
    in_specs = [pl.BlockSpec((1, tm, d), lambda i, t, c: (i, t, 0)),
                pl.BlockSpec((ec, d), lambda i, t, c: (c, 0)),
                pl.BlockSpec((d, ec), lambda i, t, c: (0, c)),
                ispec, ispec,
                pl.BlockSpec((1, PEER_HEADS, PEER_NKEYS, tm), lambda i, t, c: (i, 0, 0, t)),
                pl.BlockSpec((1, tm, d), lambda i, t, c: (i, t, 0)),
                pl.BlockSpec((1, 1, 6 * d), lambda i, t, c: (mod_row(i), 0, 0))]
    args = [h2, u, vt, th, e1, e2, x, mod3]
    if final:
        in_specs.append(pl.BlockSpec((1, d), lambda i, t, c: (0, 0)))
        args.append(final_g)
    return pl.pallas_call(
        functools.partial(_peer_kernel, final=final, d=d, ich=ich),
        grid=(b, l // tm, ne // ec),
        in_specs=in_specs,
        out_specs=pl.BlockSpec((1, tm, d), lambda i, t, c: (i, t, 0)),
        out_shape=jax.ShapeDtypeStruct((b, l, d), F32),
        scratch_shapes=[pltpu.VMEM((d, tm), F32), pltpu.VMEM((ec, tm), F32),
                        pltpu.VMEM((ec, tm), BF16), pltpu.VMEM((d, tm), BF16)],
        compiler_params=_cparams(("parallel", "parallel", "arbitrary")),
        name="peer_final" if final else "peer",
    )(*args)


def _rope_tables(l, dim, width):
    t = jnp.arange(l)
    rows = (t // GRID_W).astype(F32)
    cols = (t % GRID_W).astype(F32)
    half = dim // 2
    inv = jnp.power(ROPE_BASE, -jnp.arange(0, half, 2, dtype=F32) / half)
    ang = jnp.concatenate([rows[:, None] * inv, cols[:, None] * inv], axis=-1)
    pair = (np.arange(width) % dim) // 2
    sign = np.where(np.arange(width) % 2 == 0, -1.0, 1.0).astype(np.float32)
    return jnp.cos(ang)[:, pair], jnp.sin(ang)[:, pair] * sign


def _block_mixer_and_peer(x, ctx, mod3, layer, need_ctx, p, tables, final_g):
    b, l, d = x.shape
    lc = ctx.shape[1]
    row_x = lambda i: i
    row_c = lambda i: b
    lam_init = 0.8 - 0.6 * math.exp(-0.3 * layer)

    lat = _inproj(x, mod3, row_x, p["g1"], p["w_in"], tables, min(l, 512))
    cx = _inproj(ctx, mod3, row_c, p["g1"], p["w_in"], None, lc)
    naq, nak, nav, dfq, dfk, dfv, rtq, rtk, rtv, rtg = lat
    cnaq, cnak, cnav, cdfq, cdfk, cdfv, crtq, crtk, crtv, crtg = cx

    y_na = _na_latent(naq, nak, nav, cnak, cnav, p["rpb"], 4)
    y_df = _diff(dfq, cdfk, cdfv, dfk, dfv, p["lam"], p["subln"], lam_init, 256)
    y_rt, yc_rt = _retention(p["decay"], crtq, crtk, crtv, crtg, rtq, rtk, rtv, rtg)

    def channel_mix(xx, yna, ydf, yrt, row, tm, fg):
        xo, h2, st = _outproj(xx, yna, ydf, yrt, mod3, row, p["w_out"], p["g2"], p["wq"], p["sk"], tm)
        th, e1, e2 = _topk(st, min(tm, 256))
        return _peer(h2, p["u"], p["vt"], th, e1, e2, xo, mod3, row, fg, tm)

    x = channel_mix(x, y_na, y_df, y_rt, row_x, min(l, 512), final_g)
    if need_ctx:
        yc_na = _na_ctx(cnaq, cnak, cnav)
        yc_df = _diff(cdfq, cdfk, cdfv, None, None, p["lam"], p["subln"], lam_init, lc)
        flat = lambda t: t.reshape(1, b * lc, t.shape[-1])
        ctx = channel_mix(flat(ctx), flat(yc_na), flat(yc_df), flat(yc_rt), lambda i: b,
                          min(b * lc, 512), None).reshape(b, lc, d)
    return x, ctx


def kernel(x, c, ctx, c_ctx, w_ada, b_ada, norm1_g, w_in, na_rpb, diff_lambda, diff_subln_g,
           ret_decay_logit, w_out, norm2_g, peer_wq, peer_subkeys, peer_u, peer_v, final_g):
    b, l, d = x.shape
    depth = w_ada.shape[0]
    x = x.astype(F32)
    ctx = ctx.astype(F32)
    npad = -(-(b + 1) // 8) * 8
    cpad = jnp.zeros((npad, d), F32).at[:b].set(c.astype(F32)).at[b].set(c_ctx.astype(F32))
    mod = _ada(cpad, w_ada.astype(F32), b_ada.astype(F32))
    tables = _rope_tables(l, DIFF_DIM, DIFF_QK_W) + _rope_tables(l, RET_QK, RET_QK_W)
    for layer in range(depth):
        p = {
            "g1": norm1_g[layer].astype(F32).reshape(1, d),
            "g2": norm2_g[layer].astype(F32).reshape(1, d),
            "w_in": w_in[layer].astype(BF16),
            "rpb": na_rpb[layer],
            "lam": diff_lambda[layer].astype(F32),
            "subln": jnp.tile(diff_subln_g[layer].astype(F32), DIFF_HEADS).reshape(1, DIFF_W),
            "decay": ret_decay_logit[layer],
            "w_out": w_out[layer].astype(BF16),
            "wq": peer_wq[layer].astype(BF16),
            "sk": peer_subkeys[layer].astype(BF16).reshape(2 * PEER_HEADS, PEER_NKEYS, PEER_HALF),
            "u": peer_u[layer].astype(BF16),
            "vt": peer_v[layer].astype(BF16).T,
        }
        last = layer == depth - 1
        x, ctx = _block_mixer_and_peer(x, ctx, mod[layer].reshape(npad, 1, 6 * d), layer, not last, p,
                                       tables, final_g.astype(F32).reshape(1, d) if last else None)
    return x
```

```python
import functools
import math

import numpy as np
import jax
import jax.numpy as jnp
from jax import lax
from jax.experimental import pallas as pl
from jax.experimental.pallas import tpu as pltpu

F32 = jnp.float32
BF16 = jnp.bfloat16

GRID_W = 64
NORM_EPS = 1e-6
ROPE_BASE = 10000.0
NEG_INF = -1e30
LOG2E = math.log2(math.e)
GELU_ARG_SCALE = math.sqrt(0.5)

NA_HEADS = 4
NA_DIM = 64
NA_WIN_ROWS = 8
NA_WIN_COLS = 16
DIFF_HEADS = 4
DIFF_DIM = 32
DIFF_VDIM = 64
RET_HEADS = 4
RET_QK = 64
RET_V = 128
RET_CHUNK = 128
RET_UNROLL = 8

NA_W = NA_HEADS * NA_DIM
DIFF_QK_W = DIFF_HEADS * 2 * DIFF_DIM
DIFF_W = DIFF_HEADS * DIFF_VDIM
RET_QK_W = RET_HEADS * RET_QK
RET_W = RET_HEADS * RET_V
MIX_W = NA_W + DIFF_W + RET_W
IN_COLS = 3 * NA_W + 2 * DIFF_QK_W + DIFF_W + 2 * RET_QK_W + 2 * RET_W

PEER_HEADS = 8
PEER_NKEYS = 128
PEER_KDIM = 256
PEER_TOPK = 16
PEER_HALF = PEER_KDIM // 2

C_NAQ, C_NAK, C_NAV = 0, 256, 512
C_DFQ, C_DFK, C_DFV = 768, 1024, 1280
C_RTQ, C_RTK, C_RTV, C_RTG = 1536, 1792, 2048, 2560

PEER_PAIRS = tuple((a, b) for a in range(PEER_TOPK) for b in range(PEER_TOPK // (a + 1)))
PEER_NCAND = 64

VMEM_LIMIT = 56 * 1024 * 1024


def _cparams(sem):
    return pltpu.CompilerParams(dimension_semantics=sem, vmem_limit_bytes=VMEM_LIMIT)


def _nt(a, b):
    return lax.dot_general(a, b, (((1,), (1,)), ((), ())), preferred_element_type=F32)


def _tn(a, b):
    return lax.dot_general(a, b, (((0,), (0,)), ((), ())), preferred_element_type=F32)


def _dot(a, b):
    return jnp.dot(a, b, preferred_element_type=F32)


def _rms(x):
    return x * lax.rsqrt(jnp.mean(x * x, axis=-1, keepdims=True) + NORM_EPS)


def _silu(x):
    return x * jax.nn.sigmoid(x)


def _ada_kernel(c_ref, w_ref, b_ref, o_ref):
    s = _silu(c_ref[...])
    o_ref[0] = jnp.dot(s, w_ref[0], preferred_element_type=F32,
                       precision=lax.Precision.HIGHEST) + b_ref[0]


def _ada(cpad, w_ada, b_ada):
    depth, d, n = w_ada.shape
    tn = 1536
    return pl.pallas_call(
        _ada_kernel,
        grid=(depth, n // tn),
        in_specs=[pl.BlockSpec((cpad.shape[0], d), lambda l, j: (0, 0)),
                  pl.BlockSpec((1, d, tn), lambda l, j: (l, 0, j)),
                  pl.BlockSpec((1, 1, tn), lambda l, j: (l, 0, j))],
        out_specs=pl.BlockSpec((1, cpad.shape[0], tn), lambda l, j: (l, 0, j)),
        out_shape=jax.ShapeDtypeStruct((depth, cpad.shape[0], n), F32),
        compiler_params=_cparams(("parallel", "parallel")),
        name="ada",
    )(cpad, w_ada, b_ada.reshape(depth, 1, n))


def _inproj_kernel(*refs, rope, d):
    if rope:
        (x_ref, mod_ref, g_ref, w_ref, cd_ref, sd_ref, cr_ref, sr_ref,
         naq, nak, nav, dfq, dfk, dfv, rtq, rtk, rtv, rtg) = refs
    else:
        (x_ref, mod_ref, g_ref, w_ref,
         naq, nak, nav, dfq, dfk, dfv, rtq, rtk, rtv, rtg) = refs
    mod = mod_ref[0]
    h = _rms(x_ref[0]) * g_ref[...] * (1.0 + mod[:, d:2 * d]) + mod[:, 0:d]
    hb = h.astype(BF16)

    def proj(c0, n):
        return _dot(hb, w_ref[:, c0:c0 + n])

    even_lane = (lax.broadcasted_iota(jnp.int32, (1, 256), 1) & 1) == 0

    def roped(c0, cos_ref, sin_ref):
        p = proj(c0, 256)
        if not rope:
            return p
        partner = jnp.where(even_lane, pltpu.roll(p, 255, 1), pltpu.roll(p, 1, 1))
        return p * cos_ref[...] + partner * sin_ref[...]

    naq[0] = (proj(C_NAQ, 256) * (NA_DIM ** -0.5 * LOG2E)).astype(BF16)
    nak[0] = proj(C_NAK, 256).astype(BF16)
    nav[0] = proj(C_NAV, 256).astype(BF16)
    dfq[0] = (roped(C_DFQ, cd_ref if rope else None, sd_ref if rope else None)
              * (DIFF_DIM ** -0.5 * LOG2E)).astype(BF16)
    dfk[0] = roped(C_DFK, cd_ref if rope else None, sd_ref if rope else None).astype(BF16)
    dfv[0] = proj(C_DFV, 256).astype(BF16)
    rtq[0] = roped(C_RTQ, cr_ref if rope else None, sr_ref if rope else None).astype(BF16)
    rtk[0] = (roped(C_RTK, cr_ref if rope else None, sr_ref if rope else None)
              * RET_QK ** -0.5).astype(BF16)
    rtv[0] = proj(C_RTV, 512).astype(BF16)
    rtg[0] = proj(C_RTG, 512)


def _inproj(x, mod3, mod_row, g1, w_in, tables, tm):
    b, l, d = x.shape
    rope = tables is not None
    in_specs = [pl.BlockSpec((1, tm, d), lambda i, t: (i, t, 0)),
                pl.BlockSpec((1, 1, 6 * d), lambda i, t: (mod_row(i), 0, 0)),
                pl.BlockSpec((1, d), lambda i, t: (0, 0)),
                pl.BlockSpec((d, IN_COLS), lambda i, t: (0, 0))]
    args = [x, mod3, g1, w_in]
    if rope:
        in_specs += [pl.BlockSpec((tm, 256), lambda i, t: (t, 0))] * 4
        args += list(tables)
    widths = (256, 256, 256, 256, 256, 256, 256, 256, 512, 512)
    dtypes = (BF16,) * 9 + (F32,)
    out_specs = [pl.BlockSpec((1, tm, w), lambda i, t: (i, t, 0)) for w in widths]
    out_shape = [jax.ShapeDtypeStruct((b, l, w), dt) for w, dt in zip(widths, dtypes)]
    return pl.pallas_call(
        functools.partial(_inproj_kernel, rope=rope, d=d),
        grid=(b, l // tm),
        in_specs=in_specs, out_specs=out_specs, out_shape=out_shape,
        compiler_params=_cparams(("parallel", "parallel")),
        name="inproj_rope" if rope else "inproj_ctx",
    )(*args)


def _na_plan(rows, rb):
    wr = min(NA_WIN_ROWS, rows)
    kr = min(rb + wr - 1, rows)
    nblk = rows // rb
    win0 = np.clip(np.arange(rows) - wr // 2, 0, rows - wr)
    ks = np.clip(np.arange(nblk) * rb - wr // 2, 0, rows - kr)
    pats, pat_of = [], np.zeros(nblk, np.int32)
    for g in range(nblk):
        r = g * rb + np.arange(rb)
        krow = ks[g] + np.arange(kr)
        dr = krow[None, :] - r[:, None] + (NA_WIN_ROWS - 1)
        ok = (krow[None, :] >= win0[r][:, None]) & (krow[None, :] < win0[r][:, None] + wr)
        assert ok.sum(axis=1).min() == wr
        key = (np.where(ok, dr, 0).tobytes(), ok.tobytes())
        for p, (k2, _, _) in enumerate(pats):
            if k2 == key:
                pat_of[g] = p
                break
        else:
            pat_of[g] = len(pats)
            pats.append((key, np.where(ok, dr, 0), ok))
    dr_idx = np.stack([p[1] for p in pats])
    row_ok = np.stack([p[2] for p in pats])
    qcol = np.arange(GRID_W)
    kcol = np.arange(GRID_W)
    cs = np.clip(qcol - NA_WIN_COLS // 2, 0, GRID_W - NA_WIN_COLS)
    col_ok = (kcol[None, :] >= cs[:, None]) & (kcol[None, :] < cs[:, None] + NA_WIN_COLS)
    dc_idx = np.clip(kcol[None, :] - qcol[:, None], 1 - NA_WIN_COLS, NA_WIN_COLS - 1) + NA_WIN_COLS - 1
    return kr, ks.astype(np.int32), pat_of, dr_idx, row_ok, dc_idx, col_ok


def _na_bias(rpb, plan, rb):
    kr, _, _, dr_idx, row_ok, dc_idx, col_ok = plan
    p = dr_idx.shape[0]
    nr, nc = 2 * NA_WIN_ROWS - 1, 2 * NA_WIN_COLS - 1
    row_hot = ((np.arange(nr) == dr_idx[..., None]) & row_ok[..., None]).astype(np.float32)
    col_hot = ((np.arange(nc) == dc_idx[..., None]) & col_ok[..., None]).astype(np.float32)
    bias = jnp.einsum("hrc,pqwr,xyc->hpqxwy", rpb.astype(F32), row_hot, col_hot,
                      precision=lax.Precision.HIGHEST)
    ok = row_ok[:, :, None, :, None] & col_ok[None, None, :, None, :]
    bias = jnp.where(ok[None], bias * LOG2E, NEG_INF)
    return bias.reshape(NA_HEADS, p, rb * GRID_W, kr * GRID_W)


def _na_kernel(ks_ref, pat_ref, q_ref, kx_ref, vx_ref, kc_ref, vc_ref, bias_ref, o_ref, *, nkeys):
    g = pl.program_id(1)
    q = q_ref[0]
    k0 = pl.multiple_of(ks_ref[g] * GRID_W, GRID_W)
    kw = kx_ref[0, pl.ds(k0, nkeys), :]
    vw = vx_ref[0, pl.ds(k0, nkeys), :]
    kc = kc_ref[0]
    vc = vc_ref[0]
    pat = pat_ref[g]
    lane = lax.broadcasted_iota(jnp.int32, (1, NA_W), 1)
    out = jnp.zeros((q.shape[0], NA_W), F32)
    for h in range(NA_HEADS):
        hm = (lane >= NA_DIM * h) & (lane < NA_DIM * (h + 1))
        qh = jnp.where(hm, q, jnp.zeros_like(q))
        sw = _nt(qh, kw) + bias_ref[h, pat]
        sc = _nt(qh, kc)
        m = jnp.maximum(jnp.max(sw, axis=-1, keepdims=True), jnp.max(sc, axis=-1, keepdims=True))
        pw = jnp.exp2(sw - m)
        pc = jnp.exp2(sc - m)
        den = jnp.sum(pw, axis=-1, keepdims=True) + jnp.sum(pc, axis=-1, keepdims=True)
        o = _dot(pw.astype(BF16), vw) + _dot(pc.astype(BF16), vc)
        out = out + jnp.where(hm, o / den, 0.0)
    o_ref[0] = out.astype(o_ref.dtype)


def _na_latent(q, kx, vx, kc, vc, rpb, rb):
    b, l, _ = q.shape
    lc = kc.shape[1]
    rows = l // GRID_W
    plan = _na_plan(rows, rb)
    kr, ks, pat_of = plan[0], plan[1], plan[2]
    bias = _na_bias(rpb, plan, rb)
    tq, nkeys = rb * GRID_W, kr * GRID_W
    grid_spec = pltpu.PrefetchScalarGridSpec(
        num_scalar_prefetch=2,
        grid=(b, rows // rb),
        in_specs=[pl.BlockSpec((1, tq, NA_W), lambda i, g, *_: (i, g, 0)),
                  pl.BlockSpec((1, l, NA_W), lambda i, g, *_: (i, 0, 0)),
                  pl.BlockSpec((1, l, NA_W), lambda i, g, *_: (i, 0, 0)),
                  pl.BlockSpec((1, lc, NA_W), lambda i, g, *_: (i, 0, 0)),
                  pl.BlockSpec((1, lc, NA_W), lambda i, g, *_: (i, 0, 0)),
                  pl.BlockSpec(bias.shape, lambda i, g, *_: (0, 0, 0, 0))],
        out_specs=pl.BlockSpec((1, tq, NA_W), lambda i, g, *_: (i, g, 0)),
    )
    return pl.pallas_call(
        functools.partial(_na_kernel, nkeys=nkeys),
        grid_spec=grid_spec,
        out_shape=jax.ShapeDtypeStruct((b, l, NA_W), BF16),
        compiler_params=_cparams(("parallel", "arbitrary")),
        name="na_latent",
    )(jnp.asarray(ks), jnp.asarray(pat_of), q, kx, vx, kc, vc, bias)


def _na_ctx_kernel(q_ref, kc_ref, vc_ref, o_ref):
    q = q_ref[0]
    kc = kc_ref[0]
    vc = vc_ref[0]
    lane = lax.broadcasted_iota(jnp.int32, (1, NA_W), 1)
    out = jnp.zeros((q.shape[0], NA_W), F32)
    for h in range(NA_HEADS):
        hm = (lane >= NA_DIM * h) & (lane < NA_DIM * (h + 1))
        qh = jnp.where(hm, q, jnp.zeros_like(q))
        sc = _nt(qh, kc)
        pc = jnp.exp2(sc - jnp.max(sc, axis=-1, keepdims=True))
        o = _dot(pc.astype(BF16), vc)
        out = out + jnp.where(hm, o / jnp.sum(pc, axis=-1, keepdims=True), 0.0)
    o_ref[0] = out.astype(o_ref.dtype)


def _na_ctx(q, kc, vc):
    b, lc, _ = q.shape
    spec = pl.BlockSpec((1, lc, NA_W), lambda i: (i, 0, 0))
    return pl.pallas_call(
        _na_ctx_kernel, grid=(b,), in_specs=[spec, spec, spec], out_specs=spec,
        out_shape=jax.ShapeDtypeStruct((b, lc, NA_W), BF16),
        compiler_params=_cparams(("parallel",)), name="na_ctx",
    )(q, kc, vc)


def _diff_kernel(*refs, lam_init, has_x):
    if has_x:
        q_ref, kc_ref, vc_ref, kx_ref, vx_ref, lam_ref, g_ref, o_ref = refs
    else:
        q_ref, kc_ref, vc_ref, lam_ref, g_ref, o_ref = refs
    q = q_ref[0]
    kc = kc_ref[0]
    vc = vc_ref[0]
    lp = lam_ref[...]
    lam = (jnp.exp(jnp.sum(lp[0:1] * lp[1:2], axis=-1, keepdims=True))
           - jnp.exp(jnp.sum(lp[2:3] * lp[3:4], axis=-1, keepdims=True)) + lam_init)
    lane = lax.broadcasted_iota(jnp.int32, (1, DIFF_W), 1)
    out = jnp.zeros((q.shape[0], DIFF_W), F32)
    def scores(h):
        res = []
        for c in range(2):
            lo = DIFF_VDIM * h + DIFF_DIM * c
            qm = jnp.where((lane >= lo) & (lane < lo + DIFF_DIM), q, jnp.zeros_like(q))
            res.append((_nt(qm, kc), _nt(qm, kx_ref[0]) if has_x else None))
        return res

    nxt = scores(0)
    for h in range(DIFF_HEADS):
        cur = nxt
        if h + 1 < DIFF_HEADS:
            nxt = scores(h + 1)
        parts = []
        for c in range(2):
            sc, sx = cur[c]
            m = jnp.max(sc, axis=-1, keepdims=True)
            if has_x:
                m = jnp.maximum(m, jnp.max(sx, axis=-1, keepdims=True))
            ec = jnp.exp2(sc - m)
            den = jnp.sum(ec, axis=-1, keepdims=True)
            ex = None
            if has_x:
                ex = jnp.exp2(sx - m)
                den = den + jnp.sum(ex, axis=-1, keepdims=True)
            parts.append((ec, ex, den))
        (ec0, ex0, den0), (ec1, ex1, den1) = parts
        rho = lam * den0 / den1
        o = _dot((ec0 - ec1 * rho).astype(BF16), vc)
        if has_x:
            o = o + _dot((ex0 - ex1 * rho).astype(BF16), vx_ref[0])
        hm = (lane >= DIFF_VDIM * h) & (lane < DIFF_VDIM * (h + 1))
        oh = jnp.where(hm, o / den0, 0.0)
        ms = jnp.sum(oh * oh, axis=-1, keepdims=True) * (1.0 / DIFF_VDIM)
        out = out + oh * lax.rsqrt(ms + NORM_EPS)
    o_ref[0] = (out * g_ref[...] * (1.0 - lam_init)).astype(o_ref.dtype)


def _diff(q, kc, vc, kx, vx, lam_p, g4, lam_init, tq):
    b, l, _ = q.shape
    lc = kc.shape[1]
    has_x = kx is not None
    cspec = pl.BlockSpec((1, lc, DIFF_W), lambda i, t: (i, 0, 0))
    in_specs = [pl.BlockSpec((1, tq, DIFF_W), lambda i, t: (i, t, 0)), cspec, cspec]
    args = [q, kc, vc]
    if has_x:
        xspec = pl.BlockSpec((1, kx.shape[1], DIFF_W), lambda i, t: (i, 0, 0))
        in_specs += [xspec, xspec]
        args += [kx, vx]
    in_specs += [pl.BlockSpec(lam_p.shape, lambda i, t: (0, 0)),
                 pl.BlockSpec((1, DIFF_W), lambda i, t: (0, 0))]
    args += [lam_p, g4]
    return pl.pallas_call(
        functools.partial(_diff_kernel, lam_init=lam_init, has_x=has_x),
        grid=(b, l // tq), in_specs=in_specs,
        out_specs=pl.BlockSpec((1, tq, DIFF_W), lambda i, t: (i, t, 0)),
        out_shape=jax.ShapeDtypeStruct((b, l, DIFF_W), BF16),
        compiler_params=_cparams(("parallel", "arbitrary")),
        name="diff_latent" if has_x else "diff_ctx",
    )(*args)


def _ret_kernel(dl_ref, cq_ref, ck_ref, cv_ref, cg_ref, xq_ref, xk_ref, xv_ref, xg_ref,
                yx_ref, yc_ref, s_ref, *, n_ctx, n_lat):
    h = pl.program_id(1)
    c = RET_CHUNK
    lane = lax.broadcasted_iota(jnp.int32, (1, RET_QK_W), 1)
    hm = (lane >= RET_QK * h) & (lane < RET_QK * (h + 1))
    row = lax.broadcasted_iota(jnp.int32, (c, c), 0).astype(F32)
    col = lax.broadcasted_iota(jnp.int32, (c, c), 1).astype(F32)
    roww = lax.broadcasted_iota(jnp.int32, (c, RET_QK_W), 0).astype(F32)

    def log_sigmoid(d):
        x = jnp.full((1, 1), dl_ref[d, h], F32)
        return jnp.minimum(x, 0.0) - jnp.log1p(jnp.exp(-jnp.abs(x)))

    lgf, lgb = log_sigmoid(0), log_sigmoid(1)
    diff = row - col
    fwd = (jnp.where(diff >= 0, jnp.exp(lgf * jnp.maximum(diff, 0.0)), 0.0),
           jnp.exp(lgf * (roww + 1.0)),
           jnp.exp(lgf * (c - 1.0 - roww)),
           jnp.exp(lgf * c))
    bwd = (jnp.where(diff <= 0, jnp.exp(lgb * jnp.maximum(-diff, 0.0)), 0.0),
           jnp.exp(lgb * (c - roww)),
           jnp.exp(lgb * roww),
           jnp.exp(lgb * c))

    def step(q_ref, k_ref, v_ref, g_ref, out_ref, ci, mats, sdir, second):
        dm, qd, kd, cd = mats
        sl = pl.ds(pl.multiple_of(ci * c, c), c)
        q = q_ref[0, sl, :]
        q = jnp.where(hm, q, jnp.zeros_like(q))
        k = k_ref[0, sl, :]
        v = v_ref[0, sl, :]
        att = _nt(q, k) * dm
        s = s_ref[sdir]
        o = _dot(att.astype(BF16), v) + _dot((q.astype(F32) * qd).astype(BF16), s.astype(BF16))
        s_ref[sdir] = cd * s + _tn((k.astype(F32) * kd).astype(BF16), v)
        if second:
            tot = out_ref[0, sl, :] + o
            out_ref[0, sl, :] = _rms(tot) * _silu(g_ref[0, sl, :])
        else:
            out_ref[0, sl, :] = o

    def both(refs, n, i, second):
        step(*refs, i, fwd, 0, second)
        step(*refs, n - 1 - i, bwd, 1, second)

    s_ref[...] = jnp.zeros_like(s_ref)
    crefs = (cq_ref, ck_ref, cv_ref, cg_ref, yc_ref)
    xrefs = (xq_ref, xk_ref, xv_ref, xg_ref, yx_ref)
    for i in range(n_ctx):
        both(crefs, n_ctx, i, i >= n_ctx // 2)
    for second in (False, True):
        def body(i, carry, second=second):
            both(xrefs, n_lat, i, second)
            return carry

        lax.fori_loop(n_lat // 2 if second else 0, n_lat if second else n_lat // 2, body, 0,
                      unroll=RET_UNROLL if (n_lat // 2) % RET_UNROLL == 0 else 1)


def _retention(decay_logit, cq, ck, cv, cg, xq, xk, xv, xg):
    b, l, _ = xq.shape
    lc = cq.shape[1]

    def qk_spec(n):
        return pl.BlockSpec((1, n, RET_QK_W), lambda i, h: (i, 0, 0))

    def v_spec(n):
        return pl.BlockSpec((1, n, RET_V), lambda i, h: (i, 0, h))

    return pl.pallas_call(
        functools.partial(_ret_kernel, n_ctx=lc // RET_CHUNK, n_lat=l // RET_CHUNK),
        grid=(b, RET_HEADS),
        in_specs=[pl.BlockSpec(memory_space=pltpu.SMEM),
                  qk_spec(lc), qk_spec(lc), v_spec(lc), v_spec(lc),
                  qk_spec(l), qk_spec(l), v_spec(l), v_spec(l)],
        out_specs=[v_spec(l), v_spec(lc)],
        out_shape=[jax.ShapeDtypeStruct((b, l, RET_W), F32), jax.ShapeDtypeStruct((b, lc, RET_W), F32)],
        scratch_shapes=[pltpu.VMEM((2, RET_QK_W, RET_V), F32)],
        compiler_params=_cparams(("parallel", "arbitrary")),
        name="retention",
    )(decay_logit.astype(F32), cq, ck, cv, cg, xq, xk, xv, xg)


def _outproj_kernel(x_ref, yna_ref, ydf_ref, yrt_ref, mod_ref, wo_ref, g2_ref, wq_ref, sk_ref,
                    xo_ref, h2_ref, st_ref, *, d):
    y = (_dot(yna_ref[0], wo_ref[0:NA_W, :])
         + _dot(ydf_ref[0], wo_ref[NA_W:NA_W + DIFF_W, :])
         + _dot(yrt_ref[0].astype(BF16), wo_ref[NA_W + DIFF_W:MIX_W, :]))
    mod = mod_ref[0]
    x = x_ref[0] + mod[:, 2 * d:3 * d] * y
    xo_ref[0] = x
    h2 = _rms(x) * g2_ref[...] * (1.0 + mod[:, 4 * d:5 * d]) + mod[:, 3 * d:4 * d]
    h2_ref[0] = (h2 * GELU_ARG_SCALE).astype(BF16)
    qp = _dot(h2.astype(BF16), wq_ref[...]).astype(BF16)
    for hp in range(2 * PEER_HEADS):
        st_ref[0, hp] = _nt(sk_ref[hp], qp[:, hp * PEER_HALF:(hp + 1) * PEER_HALF])


def _outproj(x, yna, ydf, yrt, mod3, mod_row, wo, g2, wq, sk, tm):
    b, l, d = x.shape
    nhp = 2 * PEER_HEADS

    def tok(w):
        return pl.BlockSpec((1, tm, w), lambda i, t: (i, t, 0))

    return pl.pallas_call(
        functools.partial(_outproj_kernel, d=d),
        grid=(b, l // tm),
        in_specs=[tok(d), tok(NA_W), tok(DIFF_W), tok(RET_W),
                  pl.BlockSpec((1, 1, 6 * d), lambda i, t: (mod_row(i), 0, 0)),
                  pl.BlockSpec(wo.shape, lambda i, t: (0, 0)),
                  pl.BlockSpec((1, d), lambda i, t: (0, 0)),
                  pl.BlockSpec(wq.shape, lambda i, t: (0, 0)),
                  pl.BlockSpec(sk.shape, lambda i, t: (0, 0, 0))],
        out_specs=[tok(d), tok(d),
                   pl.BlockSpec((1, nhp, PEER_NKEYS, tm), lambda i, t: (i, 0, 0, t))],
        out_shape=[jax.ShapeDtypeStruct((b, l, d), F32), jax.ShapeDtypeStruct((b, l, d), BF16),
                   jax.ShapeDtypeStruct((b, nhp, PEER_NKEYS, l), F32)],
        compiler_params=_cparams(("parallel", "parallel")),
        name="outproj",
    )(x, yna, ydf, yrt, mod3, wo, g2, wq, sk)


def _sort_network(n):
    pairs = []
    p = 1
    while p < n:
        k = p
        while k >= 1:
            for j in range(k % p, n - k, 2 * k):
                for i in range(min(k, n - j - k)):
                    if (i + j) // (2 * p) == (i + j + k) // (2 * p):
                        pairs.append((i + j, i + j + k))
            k //= 2
        p *= 2
    return pairs


def _topk_kernel(st_ref, th_ref, e1_ref, e2_ref, cand_ref, t2_ref):
    neg = -jnp.inf

    def blocks(x):
        return [x[r * 8:(r + 1) * 8, :] for r in range(x.shape[0] // 8)]

    def sort_columns(vs):
        vs = list(vs)
        for i, j in _sort_network(len(vs)):
            vs[i], vs[j] = jnp.maximum(vs[i], vs[j]), jnp.minimum(vs[i], vs[j])
        return vs

    def top_rows(x):
        vs = sort_columns(blocks(x))
        rows = []
        for k in range(PEER_TOPK):
            m = jnp.max(vs[0], axis=0, keepdims=True)
            rows.append(m)
            depth = min(len(vs), PEER_TOPK - 1 - k)
            if depth > 0:
                hit = vs[0] == m
                for r in range(depth):
                    below = vs[r + 1] if r + 1 < len(vs) else neg
                    vs[r] = jnp.where(hit, below, vs[r])
        return rows

    cand_ref[len(PEER_PAIRS):, :] = jnp.full((PEER_NCAND - len(PEER_PAIRS), cand_ref.shape[1]), neg, F32)
    for h in range(PEER_HEADS):
        s1 = st_ref[0, 2 * h]
        s2 = st_ref[0, 2 * h + 1]
        a1 = s1 - jnp.max(s1, axis=0, keepdims=True)
        a2 = s2 - jnp.max(s2, axis=0, keepdims=True)
        t1 = top_rows(a1)
        t2 = top_rows(a2)
        for b in range(PEER_TOPK):
            t2_ref[b:b + 1, :] = t2[b]
        t2s = t2_ref[...]
        off = 0
        for a in range(PEER_TOPK):
            n = PEER_TOPK // (a + 1)
            cand_ref[off:off + n, :] = (t1[a] + t2s)[0:n]
            off += n
        cand = cand_ref[...]
        tau = top_rows(cand)[-1]
        z = jnp.sum(jnp.where(cand >= tau, jnp.exp(cand), 0.0), axis=0, keepdims=True)
        th_ref[0, h] = jnp.exp(tau - a1)
        e1_ref[0, h] = jnp.exp(a1) * (GELU_ARG_SCALE / z)
        e2_ref[0, h] = jnp.exp(a2)


def _topk(st, tl):
    b, nhp, nk, l = st.shape
    hspec = pl.BlockSpec((1, PEER_HEADS, nk, tl), lambda i, t: (i, 0, 0, t))
    hshape = jax.ShapeDtypeStruct((b, PEER_HEADS, nk, l), F32)
    return pl.pallas_call(
        _topk_kernel,
        grid=(b, l // tl),
        in_specs=[pl.BlockSpec((1, nhp, nk, tl), lambda i, t: (i, 0, 0, t))],
        out_specs=[hspec, hspec, hspec],
        out_shape=[hshape, hshape, hshape],
        scratch_shapes=[pltpu.VMEM((PEER_NCAND, tl), F32), pltpu.VMEM((PEER_TOPK, tl), F32)],
        compiler_params=_cparams(("parallel", "parallel")),
        name="peer_topk",
    )(st)


def _peer_kernel(*refs, final, d, ich):
    if final:
        h2_ref, u_ref, vt_ref, th_ref, e1_ref, e2_ref, x_ref, mod_ref, fg_ref, o_ref = refs[:10]
    else:
        h2_ref, u_ref, vt_ref, th_ref, e1_ref, e2_ref, x_ref, mod_ref, o_ref = refs[:9]
    acc_ref, at_ref, gw_ref, h2t_ref = refs[-4:]
    c = pl.program_id(2)
    tm = h2_ref.shape[1]
    nt = PEER_NKEYS // 8

    @pl.when(c == 0)
    def _():
        acc_ref[...] = jnp.zeros_like(acc_ref)
        h2t_ref[...] = h2_ref[0].T

    at_ref[...] = _dot(u_ref[...], h2t_ref[...])
    for ii in range(ich):
        for lg in range(tm // 128):
            ls = slice(lg * 128, (lg + 1) * 128)
            w = [None] * nt
            for h in range(PEER_HEADS):
                th = jnp.broadcast_to(th_ref[0, h, ii:ii + 1, ls], (8, 128))
                e1 = jnp.broadcast_to(e1_ref[0, h, ii:ii + 1, ls], (8, 128))
                for jt in range(nt):
                    e2 = e2_ref[0, h, jt * 8:(jt + 1) * 8, ls]
                    term = jnp.where(e2 >= th, e2, 0.0) * e1
                    w[jt] = term if w[jt] is None else w[jt] + term
            for jt in range(0, nt, 2):
                r0 = ii * PEER_NKEYS + jt * 8
                a = at_ref[r0:r0 + 16, ls]
                g = a * (1.0 + lax.erf(a)) * jnp.concatenate(w[jt:jt + 2], axis=0)
                gw_ref[r0:r0 + 16, ls] = g.astype(BF16)
    acc_ref[...] += _dot(vt_ref[...], gw_ref[...])

    @pl.when(c == pl.num_programs(2) - 1)
    def _():
        mod = mod_ref[0]
        xo = x_ref[0] + mod[:, 5 * d:6 * d] * acc_ref[...].T
        if final:
            xo = _rms(xo) * fg_ref[...]
        o_ref[0] = xo


def _peer(h2, u, vt, th, e1, e2, x, mod3, mod_row, final_g, tm, ich=16):
    b, l, d = x.shape
    ne = u.shape[0]
    ec = ich * PEER_NKEYS
    final = final_g is not None
    ispec = pl.BlockSpec((1, PEER_HEADS, ich, tm), lambda i, t, c: (i, 0, c, t))
    in_specs = [pl.BlockSpec((1, tm, d), lambda i, t, c: (i, t, 0)),
                pl.BlockSpec((ec, d), lambda i, t, c: (c, 0)),
                pl.BlockSpec((d, ec), lambda i, t, c: (0, c)),
                ispec, ispec,
                pl.BlockSpec((1, PEER_HEADS, PEER_NKEYS, tm), lambda i, t, c: (i, 0, 0, t)),
                pl.BlockSpec((1, tm, d), lambda i, t, c: (i, t, 0)),
                pl.BlockSpec((1, 1, 6 * d), lambda i, t, c: (mod_row(i), 0, 0))]
    args = [h2, u, vt, th, e1, e2, x, mod3]
    if final:
        in_specs.append(pl.BlockSpec((1, d), lambda i, t, c: (0, 0)))
        args.append(final_g)
    return pl.pallas_call(
        functools.partial(_peer_kernel, final=final, d=d, ich=ich),
        grid=(b, l // tm, ne // ec),
        in_specs=in_specs,
        out_specs=pl.BlockSpec((1, tm, d), lambda i, t, c: (i, t, 0)),
        out_shape=jax.ShapeDtypeStruct((b, l, d), F32),
        scratch_shapes=[pltpu.VMEM((d, tm), F32), pltpu.VMEM((ec, tm), F32),
                        pltpu.VMEM((ec, tm), BF16), pltpu.VMEM((d, tm), BF16)],
        compiler_params=_cparams(("parallel", "parallel", "arbitrary")),
        name="peer_final" if final else "peer",
    )(*args)


def _rope_tables(l, dim, width):
    t = jnp.arange(l)
    rows = (t // GRID_W).astype(F32)
    cols = (t % GRID_W).astype(F32)
    half = dim // 2
    inv = jnp.power(ROPE_BASE, -jnp.arange(0, half, 2, dtype=F32) / half)
    ang = jnp.concatenate([rows[:, None] * inv, cols[:, None] * inv], axis=-1)
    pair = (np.arange(width) % dim) // 2
    sign = np.where(np.arange(width) % 2 == 0, -1.0, 1.0).astype(np.float32)
    return jnp.cos(ang)[:, pair], jnp.sin(ang)[:, pair] * sign


def _block_mixer_and_peer(x, ctx, mod3, layer, need_ctx, p, tables, final_g):
    b, l, d = x.shape
    lc = ctx.shape[1]
    row_x = lambda i: i
    row_c = lambda i: b
    lam_init = 0.8 - 0.6 * math.exp(-0.3 * layer)

    lat = _inproj(x, mod3, row_x, p["g1"], p["w_in"], tables, min(l, 512))
    cx = _inproj(ctx, mod3, row_c, p["g1"], p["w_in"], None, lc)
    naq, nak, nav, dfq, dfk, dfv, rtq, rtk, rtv, rtg = lat
    cnaq, cnak, cnav, cdfq, cdfk, cdfv, crtq, crtk, crtv, crtg = cx

    y_na = _na_latent(naq, nak, nav, cnak, cnav, p["rpb"], 4)
    y_df = _diff(dfq, cdfk, cdfv, dfk, dfv, p["lam"], p["subln"], lam_init, 256)
    y_rt, yc_rt = _retention(p["decay"], crtq, crtk, crtv, crtg, rtq, rtk, rtv, rtg)

    def channel_mix(xx, yna, ydf, yrt, row, tm, fg):
        xo, h2, st = _outproj(xx, yna, ydf, yrt, mod3, row, p["w_out"], p["g2"], p["wq"], p["sk"], tm)
        th, e1, e2 = _topk(st, min(tm, 256))
        return _peer(h2, p["u"], p["vt"], th, e1, e2, xo, mod3, row, fg, tm)

    x = channel_mix(x, y_na, y_df, y_rt, row_x, min(l, 512), final_g)
    if need_ctx:
        yc_na = _na_ctx(cnaq, cnak, cnav)
        yc_df = _diff(cdfq, cdfk, cdfv, None, None, p["lam"], p["subln"], lam_init, lc)
        flat = lambda t: t.reshape(1, b * lc, t.shape[-1])
        ctx = channel_mix(flat(ctx), flat(yc_na), flat(yc_df), flat(yc_rt), lambda i: b,
                          min(b * lc, 512), None).reshape(b, lc, d)
    return x, ctx


def kernel(x, c, ctx, c_ctx, w_ada, b_ada, norm1_g, w_in, na_rpb, diff_lambda, diff_subln_g,
           ret_decay_logit, w_out, norm2_g, peer_wq, peer_subkeys, peer_u, peer_v, final_g):
    b, l, d = x.shape
    depth = w_ada.shape[0]
    x = x.astype(F32)
    ctx = ctx.astype(F32)
    npad = -(-(b + 1) // 8) * 8
    cpad = jnp.zeros((npad, d), F32).at[:b].set(c.astype(F32)).at[b].set(c_ctx.astype(F32))
    mod = _ada(cpad, w_ada.astype(F32), b_ada.astype(F32))
    tables = _rope_tables(l, DIFF_DIM, DIFF_QK_W) + _rope_tables(l, RET_QK, RET_QK_W)
    for layer in range(depth):
        p = {
            "g1": norm1_g[layer].astype(F32).reshape(1, d),
            "g2": norm2_g[layer].astype(F32).reshape(1, d),
            "w_in": w_in[layer].astype(BF16),
            "rpb": na_rpb[layer],
            "lam": diff_lambda[layer].astype(F32),
            "subln": jnp.tile(diff_subln_g[layer].astype(F32), DIFF_HEADS).reshape(1, DIFF_W),
            "decay": ret_decay_logit[layer],
            "w_out": w_out[layer].astype(BF16),
            "wq": peer_wq[layer].astype(BF16),
            "sk": peer_subkeys[layer].astype(BF16).reshape(2 * PEER_HEADS, PEER_NKEYS, PEER_HALF),
            "u": peer_u[layer].astype(BF16),
            "vt": peer_v[layer].astype(BF16).T,
        }
        last = layer == depth - 1
        x, ctx = _block_mixer_and_peer(x, ctx, mod[layer].reshape(npad, 1, 6 * d), layer, not last, p,
                                       tables, final_g.astype(F32).reshape(1, d) if last else None)
    return x
```

```python
import functools
import math

import numpy as np
import jax
import jax.numpy as jnp
from jax import lax
from jax.experimental import pallas as pl
from jax.experimental.pallas import tpu as pltpu

F32 = jnp.float32
BF16 = jnp.bfloat16

GRID_W = 64
NORM_EPS = 1e-6
ROPE_BASE = 10000.0
NEG_INF = -1e30
LOG2E = math.log2(math.e)
GELU_ARG_SCALE = math.sqrt(0.5)

NA_HEADS = 4
NA_DIM = 64
NA_WIN_ROWS = 8
NA_WIN_COLS = 16
DIFF_HEADS = 4
DIFF_DIM = 32
DIFF_VDIM = 64
RET_HEADS = 4
RET_QK = 64
RET_V = 128
RET_CHUNK = 128
RET_UNROLL = 8

NA_W = NA_HEADS * NA_DIM
DIFF_QK_W = DIFF_HEADS * 2 * DIFF_DIM
DIFF_W = DIFF_HEADS * DIFF_VDIM
RET_QK_W = RET_HEADS * RET_QK
RET_W = RET_HEADS * RET_V
MIX_W = NA_W + DIFF_W + RET_W
IN_COLS = 3 * NA_W + 2 * DIFF_QK_W + DIFF_W + 2 * RET_QK_W + 2 * RET_W

PEER_HEADS = 8
PEER_NKEYS = 128
PEER_KDIM = 256
PEER_TOPK = 16
PEER_HALF = PEER_KDIM // 2

C_NAQ, C_NAK, C_NAV = 0, 256, 512
C_DFQ, C_DFK, C_DFV = 768, 1024, 1280
C_RTQ, C_RTK, C_RTV, C_RTG = 1536, 1792, 2048, 2560

PEER_PAIRS = tuple((a, b) for a in range(PEER_TOPK) for b in range(PEER_TOPK // (a + 1)))
PEER_NCAND = 64

VMEM_LIMIT = 56 * 1024 * 1024


def _cparams(sem):
    return pltpu.CompilerParams(dimension_semantics=sem, vmem_limit_bytes=VMEM_LIMIT)


def _nt(a, b):
    return lax.dot_general(a, b, (((1,), (1,)), ((), ())), preferred_element_type=F32)


def _tn(a, b):
    return lax.dot_general(a, b, (((0,), (0,)), ((), ())), preferred_element_type=F32)


def _dot(a, b):
    return jnp.dot(a, b, preferred_element_type=F32)


def _rms(x):
    return x * lax.rsqrt(jnp.mean(x * x, axis=-1, keepdims=True) + NORM_EPS)


def _silu(x):
    return x * jax.nn.sigmoid(x)


def _ada_kernel(c_ref, w_ref, b_ref, o_ref):
    s = _silu(c_ref[...])
    o_ref[0] = jnp.dot(s, w_ref[0], preferred_element_type=F32,
                       precision=lax.Precision.HIGHEST) + b_ref[0]


def _ada(cpad, w_ada, b_ada):
    depth, d, n = w_ada.shape
    tn = 1536
    return pl.pallas_call(
        _ada_kernel,
        grid=(depth, n // tn),
        in_specs=[pl.BlockSpec((cpad.shape[0], d), lambda l, j: (0, 0)),
                  pl.BlockSpec((1, d, tn), lambda l, j: (l, 0, j)),
                  pl.BlockSpec((1, 1, tn), lambda l, j: (l, 0, j))],
        out_specs=pl.BlockSpec((1, cpad.shape[0], tn), lambda l, j: (l, 0, j)),
        out_shape=jax.ShapeDtypeStruct((depth, cpad.shape[0], n), F32),
        compiler_params=_cparams(("parallel", "parallel")),
        name="ada",
    )(cpad, w_ada, b_ada.reshape(depth, 1, n))


def _inproj_kernel(*refs, rope, d):
    if rope:
        (x_ref, mod_ref, g_ref, w_ref, cd_ref, sd_ref, cr_ref, sr_ref,
         naq, nak, nav, dfq, dfk, dfv, rtq, rtk, rtv, rtg) = refs
    else:
        (x_ref, mod_ref, g_ref, w_ref,
         naq, nak, nav, dfq, dfk, dfv, rtq, rtk, rtv, rtg) = refs
    mod = mod_ref[0]
    h = _rms(x_ref[0]) * g_ref[...] * (1.0 + mod[:, d:2 * d]) + mod[:, 0:d]
    hb = h.astype(BF16)

    def proj(c0, n):
        return _dot(hb, w_ref[:, c0:c0 + n])

    even_lane = (lax.broadcasted_iota(jnp.int32, (1, 256), 1) & 1) == 0

    def roped(c0, cos_ref, sin_ref):
        p = proj(c0, 256)
        if not rope:
            return p
        partner = jnp.where(even_lane, pltpu.roll(p, 255, 1), pltpu.roll(p, 1, 1))
        return p * cos_ref[...] + partner * sin_ref[...]

    naq[0] = (proj(C_NAQ, 256) * (NA_DIM ** -0.5 * LOG2E)).astype(BF16)
    nak[0] = proj(C_NAK, 256).astype(BF16)
    nav[0] = proj(C_NAV, 256).astype(BF16)
    dfq[0] = (roped(C_DFQ, cd_ref if rope else None, sd_ref if rope else None)
              * (DIFF_DIM ** -0.5 * LOG2E)).astype(BF16)
    dfk[0] = roped(C_DFK, cd_ref if rope else None, sd_ref if rope else None).astype(BF16)
    dfv[0] = proj(C_DFV, 256).astype(BF16)
    rtq[0] = roped(C_RTQ, cr_ref if rope else None, sr_ref if rope else None).astype(BF16)
    rtk[0] = (roped(C_RTK, cr_ref if rope else None, sr_ref if rope else None)
              * RET_QK ** -0.5).astype(BF16)
    rtv[0] = proj(C_RTV, 512).astype(BF16)
    rtg[0] = proj(C_RTG, 512)


def _inproj(x, mod3, mod_row, g1, w_in, tables, tm):
    b, l, d = x.shape
    rope = tables is not None
    in_specs = [pl.BlockSpec((1, tm, d), lambda i, t: (i, t, 0)),
                pl.BlockSpec((1, 1, 6 * d), lambda i, t: (mod_row(i), 0, 0)),
                pl.BlockSpec((1, d), lambda i, t: (0, 0)),
                pl.BlockSpec((d, IN_COLS), lambda i, t: (0, 0))]
    args = [x, mod3, g1, w_in]
    if rope:
        in_specs += [pl.BlockSpec((tm, 256), lambda i, t: (t, 0))] * 4
        args += list(tables)
    widths = (256, 256, 256, 256, 256, 256, 256, 256, 512, 512)
    dtypes = (BF16,) * 9 + (F32,)
    out_specs = [pl.BlockSpec((1, tm, w), lambda i, t: (i, t, 0)) for w in widths]
    out_shape = [jax.ShapeDtypeStruct((b, l, w), dt) for w, dt in zip(widths, dtypes)]
    return pl.pallas_call(
        functools.partial(_inproj_kernel, rope=rope, d=d),
        grid=(b, l // tm),
        in_specs=in_specs, out_specs=out_specs, out_shape=out_shape,
        compiler_params=_cparams(("parallel", "parallel")),
        name="inproj_rope" if rope else "inproj_ctx",
    )(*args)


def _na_plan(rows, rb):
    wr = min(NA_WIN_ROWS, rows)
    kr = min(rb + wr - 1, rows)
    nblk = rows // rb
    win0 = np.clip(np.arange(rows) - wr // 2, 0, rows - wr)
    ks = np.clip(np.arange(nblk) * rb - wr // 2, 0, rows - kr)
    pats, pat_of = [], np.zeros(nblk, np.int32)
    for g in range(nblk):
        r = g * rb + np.arange(rb)
        krow = ks[g] + np.arange(kr)
        dr = krow[None, :] - r[:, None] + (NA_WIN_ROWS - 1)
        ok = (krow[None, :] >= win0[r][:, None]) & (krow[None, :] < win0[r][:, None] + wr)
        assert ok.sum(axis=1).min() == wr
        key = (np.where(ok, dr, 0).tobytes(), ok.tobytes())
        for p, (k2, _, _) in enumerate(pats):
            if k2 == key:
                pat_of[g] = p
                break
        else:
            pat_of[g] = len(pats)
            pats.append((key, np.where(ok, dr, 0), ok))
    dr_idx = np.stack([p[1] for p in pats])
    row_ok = np.stack([p[2] for p in pats])
    qcol = np.arange(GRID_W)
    kcol = np.arange(GRID_W)
    cs = np.clip(qcol - NA_WIN_COLS // 2, 0, GRID_W - NA_WIN_COLS)
    col_ok = (kcol[None, :] >= cs[:, None]) & (kcol[None, :] < cs[:, None] + NA_WIN_COLS)
    dc_idx = np.clip(kcol[None, :] - qcol[:, None], 1 - NA_WIN_COLS, NA_WIN_COLS - 1) + NA_WIN_COLS - 1
    return kr, ks.astype(np.int32), pat_of, dr_idx, row_ok, dc_idx, col_ok


def _na_bias(rpb, plan, rb):
    kr, _, _, dr_idx, row_ok, dc_idx, col_ok = plan
    p = dr_idx.shape[0]
    nr, nc = 2 * NA_WIN_ROWS - 1, 2 * NA_WIN_COLS - 1
    row_hot = ((np.arange(nr) == dr_idx[..., None]) & row_ok[..., None]).astype(np.float32)
    col_hot = ((np.arange(nc) == dc_idx[..., None]) & col_ok[..., None]).astype(np.float32)
    bias = jnp.einsum("hrc,pqwr,xyc->hpqxwy", rpb.astype(F32), row_hot, col_hot,
                      precision=lax.Precision.HIGHEST)
    ok = row_ok[:, :, None, :, None] & col_ok[None, None, :, None, :]
    bias = jnp.where(ok[None], bias * LOG2E, NEG_INF)
    return bias.reshape(NA_HEADS, p, rb * GRID_W, kr * GRID_W)


def _na_kernel(ks_ref, pat_ref, q_ref, kx_ref, vx_ref, kc_ref, vc_ref, bias_ref, o_ref, *, nkeys):
    g = pl.program_id(1)
    q = q_ref[0]
    k0 = pl.multiple_of(ks_ref[g] * GRID_W, GRID_W)
    kw = kx_ref[0, pl.ds(k0, nkeys), :]
    vw = vx_ref[0, pl.ds(k0, nkeys), :]
    kc = kc_ref[0]
    vc = vc_ref[0]
    pat = pat_ref[g]
    lane = lax.broadcasted_iota(jnp.int32, (1, NA_W), 1)
    out = jnp.zeros((q.shape[0], NA_W), F32)
    for h in range(NA_HEADS):
        hm = (lane >= NA_DIM * h) & (lane < NA_DIM * (h + 1))
        qh = jnp.where(hm, q, jnp.zeros_like(q))
        sw = _nt(qh, kw) + bias_ref[h, pat]
        sc = _nt(qh, kc)
        m = jnp.maximum(jnp.max(sw, axis=-1, keepdims=True), jnp.max(sc, axis=-1, keepdims=True))
        pw = jnp.exp2(sw - m)
        pc = jnp.exp2(sc - m)
        den = jnp.sum(pw, axis=-1, keepdims=True) + jnp.sum(pc, axis=-1, keepdims=True)
        o = _dot(pw.astype(BF16), vw) + _dot(pc.astype(BF16), vc)
        out = out + jnp.where(hm, o / den, 0.0)
    o_ref[0] = out.astype(o_ref.dtype)


def _na_latent(q, kx, vx, kc, vc, rpb, rb):
    b, l, _ = q.shape
    lc = kc.shape[1]
    rows = l // GRID_W
    plan = _na_plan(rows, rb)
    kr, ks, pat_of = plan[0], plan[1], plan[2]
    bias = _na_bias(rpb, plan, rb)
    tq, nkeys = rb * GRID_W, kr * GRID_W
    grid_spec = pltpu.PrefetchScalarGridSpec(
        num_scalar_prefetch=2,
        grid=(b, rows // rb),
        in_specs=[pl.BlockSpec((1, tq, NA_W), lambda i, g, *_: (i, g, 0)),
                  pl.BlockSpec((1, l, NA_W), lambda i, g, *_: (i, 0, 0)),
                  pl.BlockSpec((1, l, NA_W), lambda i, g, *_: (i, 0, 0)),
                  pl.BlockSpec((1, lc, NA_W), lambda i, g, *_: (i, 0, 0)),
                  pl.BlockSpec((1, lc, NA_W), lambda i, g, *_: (i, 0, 0)),
                  pl.BlockSpec(bias.shape, lambda i, g, *_: (0, 0, 0, 0))],
        out_specs=pl.BlockSpec((1, tq, NA_W), lambda i, g, *_: (i, g, 0)),
    )
    return pl.pallas_call(
        functools.partial(_na_kernel, nkeys=nkeys),
        grid_spec=grid_spec,
        out_shape=jax.ShapeDtypeStruct((b, l, NA_W), BF16),
        compiler_params=_cparams(("parallel", "arbitrary")),
        name="na_latent",
    )(jnp.asarray(ks), jnp.asarray(pat_of), q, kx, vx, kc, vc, bias)


def _na_ctx_kernel(q_ref, kc_ref, vc_ref, o_ref):
    q = q_ref[0]
    kc = kc_ref[0]
    vc = vc_ref[0]
    lane = lax.broadcasted_iota(jnp.int32, (1, NA_W), 1)
    out = jnp.zeros((q.shape[0], NA_W), F32)
    for h in range(NA_HEADS):
        hm = (lane >= NA_DIM * h) & (lane < NA_DIM * (h + 1))
        qh = jnp.where(hm, q, jnp.zeros_like(q))
        sc = _nt(qh, kc)
        pc = jnp.exp2(sc - jnp.max(sc, axis=-1, keepdims=True))
        o = _dot(pc.astype(BF16), vc)
        out = out + jnp.where(hm, o / jnp.sum(pc, axis=-1, keepdims=True), 0.0)
    o_ref[0] = out.astype(o_ref.dtype)


def _na_ctx(q, kc, vc):
    b, lc, _ = q.shape
    spec = pl.BlockSpec((1, lc, NA_W), lambda i: (i, 0, 0))
    return pl.pallas_call(
        _na_ctx_kernel, grid=(b,), in_specs=[spec, spec, spec], out_specs=spec,
        out_shape=jax.ShapeDtypeStruct((b, lc, NA_W), BF16),
        compiler_params=_cparams(("parallel",)), name="na_ctx",
    )(q, kc, vc)


def _diff_kernel(*refs, lam_init, has_x):
    if has_x:
        q_ref, kc_ref, vc_ref, kx_ref, vx_ref, lam_ref, g_ref, o_ref = refs
    else:
        q_ref, kc_ref, vc_ref, lam_ref, g_ref, o_ref = refs
    q = q_ref[0]
    kc = kc_ref[0]
    vc = vc_ref[0]
    lp = lam_ref[...]
    lam = (jnp.exp(jnp.sum(lp[0:1] * lp[1:2], axis=-1, keepdims=True))
           - jnp.exp(jnp.sum(lp[2:3] * lp[3:4], axis=-1, keepdims=True)) + lam_init)
    lane = lax.broadcasted_iota(jnp.int32, (1, DIFF_W), 1)
    out = jnp.zeros((q.shape[0], DIFF_W), F32)
    def scores(h):
        res = []
        for c in range(2):
            lo = DIFF_VDIM * h + DIFF_DIM * c
            qm = jnp.where((lane >= lo) & (lane < lo + DIFF_DIM), q, jnp.zeros_like(q))
            res.append((_nt(qm, kc), _nt(qm, kx_ref[0]) if has_x else None))
        return res

    nxt = scores(0)
    for h in range(DIFF_HEADS):
        cur = nxt
        if h + 1 < DIFF_HEADS:
            nxt = scores(h + 1)
        parts = []
        for c in range(2):
            sc, sx = cur[c]
            m = jnp.max(sc, axis=-1, keepdims=True)
            if has_x:
                m = jnp.maximum(m, jnp.max(sx, axis=-1, keepdims=True))
            ec = jnp.exp2(sc - m)
            den = jnp.sum(ec, axis=-1, keepdims=True)
            ex = None
            if has_x:
                ex = jnp.exp2(sx - m)
                den = den + jnp.sum(ex, axis=-1, keepdims=True)
            parts.append((ec, ex, den))
        (ec0, ex0, den0), (ec1, ex1, den1) = parts
        rho = lam * den0 / den1
        o = _dot((ec0 - ec1 * rho).astype(BF16), vc)
        if has_x:
            o = o + _dot((ex0 - ex1 * rho).astype(BF16), vx_ref[0])
        hm = (lane >= DIFF_VDIM * h) & (lane < DIFF_VDIM * (h + 1))
        oh = jnp.where(hm, o / den0, 0.0)
        ms = jnp.sum(oh * oh, axis=-1, keepdims=True) * (1.0 / DIFF_VDIM)
        out = out + oh * lax.rsqrt(ms + NORM_EPS)
    o_ref[0] = (out * g_ref[...] * (1.0 - lam_init)).astype(o_ref.dtype)


def _diff(q, kc, vc, kx, vx, lam_p, g4, lam_init, tq):
    b, l, _ = q.shape
    lc = kc.shape[1]
    has_x = kx is not None
    cspec = pl.BlockSpec((1, lc, DIFF_W), lambda i, t: (i, 0, 0))
    in_specs = [pl.BlockSpec((1, tq, DIFF_W), lambda i, t: (i, t, 0)), cspec, cspec]
    args = [q, kc, vc]
    if has_x:
        xspec = pl.BlockSpec((1, kx.shape[1], DIFF_W), lambda i, t: (i, 0, 0))
        in_specs += [xspec, xspec]
        args += [kx, vx]
    in_specs += [pl.BlockSpec(lam_p.shape, lambda i, t: (0, 0)),
                 pl.BlockSpec((1, DIFF_W), lambda i, t: (0, 0))]
    args += [lam_p, g4]
    return pl.pallas_call(
        functools.partial(_diff_kernel, lam_init=lam_init, has_x=has_x),
        grid=(b, l // tq), in_specs=in_specs,
        out_specs=pl.BlockSpec((1, tq, DIFF_W), lambda i, t: (i, t, 0)),
        out_shape=jax.ShapeDtypeStruct((b, l, DIFF_W), BF16),
        compiler_params=_cparams(("parallel", "arbitrary")),
        name="diff_latent" if has_x else "diff_ctx",
    )(*args)


def _ret_kernel(dl_ref, cq_ref, ck_ref, cv_ref, cg_ref, xq_ref, xk_ref, xv_ref, xg_ref,
                yx_ref, yc_ref, s_ref, *, n_ctx, n_lat):
    h = pl.program_id(1)
    c = RET_CHUNK
    lane = lax.broadcasted_iota(jnp.int32, (1, RET_QK_W), 1)
    hm = (lane >= RET_QK * h) & (lane < RET_QK * (h + 1))
    row = lax.broadcasted_iota(jnp.int32, (c, c), 0).astype(F32)
    col = lax.broadcasted_iota(jnp.int32, (c, c), 1).astype(F32)
    roww = lax.broadcasted_iota(jnp.int32, (c, RET_QK_W), 0).astype(F32)

    def log_sigmoid(d):
        x = jnp.full((1, 1), dl_ref[d, h], F32)
        return jnp.minimum(x, 0.0) - jnp.log1p(jnp.exp(-jnp.abs(x)))

    lgf, lgb = log_sigmoid(0), log_sigmoid(1)
    diff = row - col
    fwd = (jnp.where(diff >= 0, jnp.exp(lgf * jnp.maximum(diff, 0.0)), 0.0),
           jnp.exp(lgf * (roww + 1.0)),
           jnp.exp(lgf * (c - 1.0 - roww)),
           jnp.exp(lgf * c))
    bwd = (jnp.where(diff <= 0, jnp.exp(lgb * jnp.maximum(-diff, 0.0)), 0.0),
           jnp.exp(lgb * (c - roww)),
           jnp.exp(lgb * roww),
           jnp.exp(lgb * c))

    def step(q_ref, k_ref, v_ref, g_ref, out_ref, ci, mats, sdir, second):
        dm, qd, kd, cd = mats
        sl = pl.ds(pl.multiple_of(ci * c, c), c)
        q = q_ref[0, sl, :]
        q = jnp.where(hm, q, jnp.zeros_like(q))
        k = k_ref[0, sl, :]
        v = v_ref[0, sl, :]
        att = _nt(q, k) * dm
        s = s_ref[sdir]
        o = _dot(att.astype(BF16), v) + _dot((q.astype(F32) * qd).astype(BF16), s.astype(BF16))
        s_ref[sdir] = cd * s + _tn((k.astype(F32) * kd).astype(BF16), v)
        if second:
            tot = out_ref[0, sl, :] + o
            out_ref[0, sl, :] = _rms(tot) * _silu(g_ref[0, sl, :])
        else:
            out_ref[0, sl, :] = o

    def both(refs, n, i, second):
        step(*refs, i, fwd, 0, second)
        step(*refs, n - 1 - i, bwd, 1, second)

    s_ref[...] = jnp.zeros_like(s_ref)
    crefs = (cq_ref, ck_ref, cv_ref, cg_ref, yc_ref)
    xrefs = (xq_ref, xk_ref, xv_ref, xg_ref, yx_ref)
    for i in range(n_ctx):
        both(crefs, n_ctx, i, i >= n_ctx // 2)
    for second in (False, True):
        def body(i, carry, second=second):
            both(xrefs, n_lat, i, second)
            return carry

        lax.fori_loop(n_lat // 2 if second else 0, n_lat if second else n_lat // 2, body, 0,
                      unroll=RET_UNROLL if (n_lat // 2) % RET_UNROLL == 0 else 1)


def _retention(decay_logit, cq, ck, cv, cg, xq, xk, xv, xg):
    b, l, _ = xq.shape
    lc = cq.shape[1]

    def qk_spec(n):
        return pl.BlockSpec((1, n, RET_QK_W), lambda i, h: (i, 0, 0))

    def v_spec(n):
        return pl.BlockSpec((1, n, RET_V), lambda i, h: (i, 0, h))

    return pl.pallas_call(
        functools.partial(_ret_kernel, n_ctx=lc // RET_CHUNK, n_lat=l // RET_CHUNK),
        grid=(b, RET_HEADS),
        in_specs=[pl.BlockSpec(memory_space=pltpu.SMEM),
                  qk_spec(lc), qk_spec(lc), v_spec(lc), v_spec(lc),
                  qk_spec(l), qk_spec(l), v_spec(l), v_spec(l)],
        out_specs=[v_spec(l), v_spec(lc)],
        out_shape=[jax.ShapeDtypeStruct((b, l, RET_W), F32), jax.ShapeDtypeStruct((b, lc, RET_W), F32)],
        scratch_shapes=[pltpu.VMEM((2, RET_QK_W, RET_V), F32)],
        compiler_params=_cparams(("parallel", "arbitrary")),
        name="retention",
    )(decay_logit.astype(F32), cq, ck, cv, cg, xq, xk, xv, xg)


def _outproj_kernel(x_ref, yna_ref, ydf_ref, yrt_ref, mod_ref, wo_ref, g2_ref, wq_ref, sk_ref,
                    xo_ref, h2_ref, st_ref, *, d):
    y = (_dot(yna_ref[0], wo_ref[0:NA_W, :])
         + _dot(ydf_ref[0], wo_ref[NA_W:NA_W + DIFF_W, :])
         + _dot(yrt_ref[0].astype(BF16), wo_ref[NA_W + DIFF_W:MIX_W, :]))
    mod = mod_ref[0]
    x = x_ref[0] + mod[:, 2 * d:3 * d] * y
    xo_ref[0] = x
    h2 = _rms(x) * g2_ref[...] * (1.0 + mod[:, 4 * d:5 * d]) + mod[:, 3 * d:4 * d]
    h2_ref[0] = (h2 * GELU_ARG_SCALE).astype(BF16)
    qp = _dot(h2.astype(BF16), wq_ref[...]).astype(BF16)
    for hp in range(2 * PEER_HEADS):
        st_ref[0, hp] = _nt(sk_ref[hp], qp[:, hp * PEER_HALF:(hp + 1) * PEER_HALF])


def _outproj(x, yna, ydf, yrt, mod3, mod_row, wo, g2, wq, sk, tm):
    b, l, d = x.shape
    nhp = 2 * PEER_HEADS

    def tok(w):
        return pl.BlockSpec((1, tm, w), lambda i, t: (i, t, 0))

    return pl.pallas_call(
        functools.partial(_outproj_kernel, d=d),
        grid=(b, l // tm),
        in_specs=[tok(d), tok(NA_W), tok(DIFF_W), tok(RET_W),
                  pl.BlockSpec((1, 1, 6 * d), lambda i, t: (mod_row(i), 0, 0)),
                  pl.BlockSpec(wo.shape, lambda i, t: (0, 0)),
                  pl.BlockSpec((1, d), lambda i, t: (0, 0)),
                  pl.BlockSpec(wq.shape, lambda i, t: (0, 0)),
                  pl.BlockSpec(sk.shape, lambda i, t: (0, 0, 0))],
        out_specs=[tok(d), tok(d),
                   pl.BlockSpec((1, nhp, PEER_NKEYS, tm), lambda i, t: (i, 0, 0, t))],
        out_shape=[jax.ShapeDtypeStruct((b, l, d), F32), jax.ShapeDtypeStruct((b, l, d), BF16),
                   jax.ShapeDtypeStruct((b, nhp, PEER_NKEYS, l), F32)],
        compiler_params=_cparams(("parallel", "parallel")),
        name="outproj",
    )(x, yna, ydf, yrt, mod3, wo, g2, wq, sk)


def _sort_network(n):
    pairs = []
    p = 1
    while p < n:
        k = p
        while k >= 1:
            for j in range(k % p, n - k, 2 * k):
                for i in range(min(k, n - j - k)):
                    if (i + j) // (2 * p) == (i + j + k) // (2 * p):
                        pairs.append((i + j, i + j + k))
            k //= 2
        p *= 2
    return pairs


def _topk_kernel(st_ref, th_ref, e1_ref, e2_ref, cand_ref, t2_ref):
    neg = -jnp.inf

    def blocks(x):
        return [x[r * 8:(r + 1) * 8, :] for r in range(x.shape[0] // 8)]

    def sort_columns(vs):
        vs = list(vs)
        for i, j in _sort_network(len(vs)):
            vs[i], vs[j] = jnp.maximum(vs[i], vs[j]), jnp.minimum(vs[i], vs[j])
        return vs

    def top_rows(x):
        vs = sort_columns(blocks(x))
        rows = []
        for k in range(PEER_TOPK):
            m = jnp.max(vs[0], axis=0, keepdims=True)
            rows.append(m)
            depth = min(len(vs), PEER_TOPK - 1 - k)
            if depth > 0:
                hit = vs[0] == m
                for r in range(depth):
                    below = vs[r + 1] if r + 1 < len(vs) else neg
                    vs[r] = jnp.where(hit, below, vs[r])
        return rows

    cand_ref[len(PEER_PAIRS):, :] = jnp.full((PEER_NCAND - len(PEER_PAIRS), cand_ref.shape[1]), neg, F32)
    for h in range(PEER_HEADS):
        s1 = st_ref[0, 2 * h]
        s2 = st_ref[0, 2 * h + 1]
        a1 = s1 - jnp.max(s1, axis=0, keepdims=True)
        a2 = s2 - jnp.max(s2, axis=0, keepdims=True)
        t1 = top_rows(a1)
        t2 = top_rows(a2)
        for b in range(PEER_TOPK):
            t2_ref[b:b + 1, :] = t2[b]
        t2s = t2_ref[...]
        off = 0
        for a in range(PEER_TOPK):
            n = PEER_TOPK // (a + 1)
            cand_ref[off:off + n, :] = (t1[a] + t2s)[0:n]
            off += n
        cand = cand_ref[...]
        tau = top_rows(cand)[-1]
        z = jnp.sum(jnp.where(cand >= tau, jnp.exp(cand), 0.0), axis=0, keepdims=True)
        th_ref[0, h] = jnp.exp(tau - a1)
        e1_ref[0, h] = jnp.exp(a1) * (GELU_ARG_SCALE / z)
        e2_ref[0, h] = jnp.exp(a2)


def _topk(st, tl):
    b, nhp, nk, l = st.shape
    hspec = pl.BlockSpec((1, PEER_HEADS, nk, tl), lambda i, t: (i, 0, 0, t))
    hshape = jax.ShapeDtypeStruct((b, PEER_HEADS, nk, l), F32)
    return pl.pallas_call(
        _topk_kernel,
        grid=(b, l // tl),
        in_specs=[pl.BlockSpec((1, nhp, nk, tl), lambda i, t: (i, 0, 0, t))],
        out_specs=[hspec, hspec, hspec],
        out_shape=[hshape, hshape, hshape],
        scratch_shapes=[pltpu.VMEM((PEER_NCAND, tl), F32), pltpu.VMEM((PEER_TOPK, tl), F32)],
        compiler_params=_cparams(("parallel", "parallel")),
        name="peer_topk",
    )(st)


def _peer_kernel(*refs, final, d, ich):
    if final:
        h2_ref, u_ref, vt_ref, th_ref, e1_ref, e2_ref, x_ref, mod_ref, fg_ref, o_ref = refs[:10]
    else:
        h2_ref, u_ref, vt_ref, th_ref, e1_ref, e2_ref, x_ref, mod_ref, o_ref = refs[:9]
    acc_ref, at_ref, gw_ref, h2t_ref = refs[-4:]
    c = pl.program_id(2)
    tm = h2_ref.shape[1]
    nt = PEER_NKEYS // 8

    @pl.when(c == 0)
    def _():
        acc_ref[...] = jnp.zeros_like(acc_ref)
        h2t_ref[...] = h2_ref[0].T

    at_ref[...] = _dot(u_ref[0], h2t_ref[...])
    for ii in range(ich):
        for lg in range(tm // 128):
            ls = slice(lg * 128, (lg + 1) * 128)
            w = [None] * nt
            for h in range(PEER_HEADS):
                th = jnp.broadcast_to(th_ref[0, h, ii:ii + 1, ls], (8, 128))
                e1 = jnp.broadcast_to(e1_ref[0, h, ii:ii + 1, ls], (8, 128))
                for jt in range(nt):
                    e2 = e2_ref[0, h, jt * 8:(jt + 1) * 8, ls]
                    term = jnp.where(e2 >= th, e2, 0.0) * e1
                    w[jt] = term if w[jt] is None else w[jt] + term
            for jt in range(0, nt, 2):
                r0 = ii * PEER_NKEYS + jt * 8
                a = at_ref[r0:r0 + 16, ls]
                g = a * (1.0 + lax.erf(a)) * jnp.concatenate(w[jt:jt + 2], axis=0)
                gw_ref[r0:r0 + 16, ls] = g.astype(BF16)
    acc_ref[...] += _dot(vt_ref[0], gw_ref[...])

    @pl.when(c == pl.num_programs(2) - 1)
    def _():
        mod = mod_ref[0]
        xo = x_ref[0] + mod[:, 5 * d:6 * d] * acc_ref[...].T
        if final:
            xo = _rms(xo) * fg_ref[...]
        o_ref[0] = xo


def _peer(h2, u, vt, layer, th, e1, e2, x, mod3, mod_row, final_g, tm, ich=16):
    b, l, d = x.shape
    ne = u.shape[1]
    ec = ich * PEER_NKEYS
    final = final_g is not None
    ispec = pl.BlockSpec((1, PEER_HEADS, ich, tm), lambda i, t, c: (i, 0, c, t))
    in_specs = [pl.BlockSpec((1, tm, d), lambda i, t, c: (i, t, 0)),
                pl.BlockSpec((1, ec, d), lambda i, t, c: (layer, c, 0)),
                pl.BlockSpec((1, d, ec), lambda i, t, c: (layer, 0, c)),
                ispec, ispec,
                pl.BlockSpec((1, PEER_HEADS, PEER_NKEYS, tm), lambda i, t, c: (i, 0, 0, t)),
                pl.BlockSpec((1, tm, d), lambda i, t, c: (i, t, 0)),
                pl.BlockSpec((1, 1, 6 * d), lambda i, t, c: (mod_row(i), 0, 0))]
    args = [h2, u, vt, th, e1, e2, x, mod3]
    if final:
        in_specs.append(pl.BlockSpec((1, d), lambda i, t, c: (0, 0)))
        args.append(final_g)
    return pl.pallas_call(
        functools.partial(_peer_kernel, final=final, d=d, ich=ich),
        grid=(b, l // tm, ne // ec),
        in_specs=in_specs,
        out_specs=pl.BlockSpec((1, tm, d), lambda i, t, c: (i, t, 0)),
        out_shape=jax.ShapeDtypeStruct((b, l, d), F32),
        scratch_shapes=[pltpu.VMEM((d, tm), F32), pltpu.VMEM((ec, tm), F32),
                        pltpu.VMEM((ec, tm), BF16), pltpu.VMEM((d, tm), BF16)],
        compiler_params=_cparams(("parallel", "parallel", "arbitrary")),
        name="peer_final" if final else "peer",
    )(*args)


def _rope_tables(l, dim, width):
    t = jnp.arange(l)
    rows = (t // GRID_W).astype(F32)
    cols = (t % GRID_W).astype(F32)
    half = dim // 2
    inv = jnp.power(ROPE_BASE, -jnp.arange(0, half, 2, dtype=F32) / half)
    ang = jnp.concatenate([rows[:, None] * inv, cols[:, None] * inv], axis=-1)
    pair = (np.arange(width) % dim) // 2
    sign = np.where(np.arange(width) % 2 == 0, -1.0, 1.0).astype(np.float32)
    return jnp.cos(ang)[:, pair], jnp.sin(ang)[:, pair] * sign


def _block_mixer_and_peer(x, ctx, mod3, layer, need_ctx, p, tables, final_g):
    b, l, d = x.shape
    lc = ctx.shape[1]
    row_x = lambda i: i
    row_c = lambda i: b
    lam_init = 0.8 - 0.6 * math.exp(-0.3 * layer)

    lat = _inproj(x, mod3, row_x, p["g1"], p["w_in"], tables, min(l, 512))
    cx = _inproj(ctx, mod3, row_c, p["g1"], p["w_in"], None, lc)
    naq, nak, nav, dfq, dfk, dfv, rtq, rtk, rtv, rtg = lat
    cnaq, cnak, cnav, cdfq, cdfk, cdfv, crtq, crtk, crtv, crtg = cx

    y_na = _na_latent(naq, nak, nav, cnak, cnav, p["rpb"], 4)
    y_df = _diff(dfq, cdfk, cdfv, dfk, dfv, p["lam"], p["subln"], lam_init, 256)
    y_rt, yc_rt = _retention(p["decay"], crtq, crtk, crtv, crtg, rtq, rtk, rtv, rtg)

    def channel_mix(xx, yna, ydf, yrt, row, tm, fg):
        xo, h2, st = _outproj(xx, yna, ydf, yrt, mod3, row, p["w_out"], p["g2"], p["wq"], p["sk"], tm)
        th, e1, e2 = _topk(st, min(tm, 256))
        return _peer(h2, p["u"], p["vt"], layer, th, e1, e2, xo, mod3, row, fg, tm)

    x = channel_mix(x, y_na, y_df, y_rt, row_x, min(l, 512), final_g)
    if need_ctx:
        yc_na = _na_ctx(cnaq, cnak, cnav)
        yc_df = _diff(cdfq, cdfk, cdfv, None, None, p["lam"], p["subln"], lam_init, lc)
        flat = lambda t: t.reshape(1, b * lc, t.shape[-1])
        ctx = channel_mix(flat(ctx), flat(yc_na), flat(yc_df), flat(yc_rt), lambda i: b,
                          min(b * lc, 512), None).reshape(b, lc, d)
    return x, ctx


def kernel(x, c, ctx, c_ctx, w_ada, b_ada, norm1_g, w_in, na_rpb, diff_lambda, diff_subln_g,
           ret_decay_logit, w_out, norm2_g, peer_wq, peer_subkeys, peer_u, peer_v, final_g):
    b, l, d = x.shape
    depth = w_ada.shape[0]
    x = x.astype(F32)
    ctx = ctx.astype(F32)
    npad = -(-(b + 1) // 8) * 8
    cpad = jnp.zeros((npad, d), F32).at[:b].set(c.astype(F32)).at[b].set(c_ctx.astype(F32))
    mod = _ada(cpad, w_ada.astype(F32), b_ada.astype(F32))
    tables = _rope_tables(l, DIFF_DIM, DIFF_QK_W) + _rope_tables(l, RET_QK, RET_QK_W)
    u_all = peer_u.astype(BF16)
    vt_all = jnp.swapaxes(peer_v.astype(BF16), 1, 2)
    for layer in range(depth):
        p = {
            "g1": norm1_g[layer].astype(F32).reshape(1, d),
            "g2": norm2_g[layer].astype(F32).reshape(1, d),
            "w_in": w_in[layer].astype(BF16),
            "rpb": na_rpb[layer],
            "lam": diff_lambda[layer].astype(F32),
            "subln": jnp.tile(diff_subln_g[layer].astype(F32), DIFF_HEADS).reshape(1, DIFF_W),
            "decay": ret_decay_logit[layer],
            "w_out": w_out[layer].astype(BF16),
            "wq": peer_wq[layer].astype(BF16),
            "sk": peer_subkeys[layer].astype(BF16).reshape(2 * PEER_HEADS, PEER_NKEYS, PEER_HALF),
            "u": u_all,
            "vt": vt_all,
        }
        last = layer == depth - 1
        x, ctx = _block_mixer_and_peer(x, ctx, mod[layer].reshape(npad, 1, 6 * d), layer, not last, p,
                                       tables, final_g.astype(F32).reshape(1, d) if last else None)
    return x
```
